```python
import math
import jax, jax.numpy as jnp
from jax import lax
import numpy as np

D_MODEL = 1024
BATCH = 8
SEQ = 8192
DEPTH = 1
DEC_BATCH = 8
DEC_SEQ = 64
PAST_LEN = 2048

CHUNK = 64
Q_BLOCK = 128
D_MIX = D_MODEL
S5_WIDTH = D_MIX // 2
S5_GROUP = 16
S5_GROUPS = S5_WIDTH // S5_GROUP
S5_STATE = 64
DT_MIN = 1e-3
DT_MAX = 1e-1
N_HEADS = 8
QK_NOPE = 64
QK_ROPE = 32
QK_DIM = QK_NOPE + QK_ROPE
V_DIM = 64
ATTN_WIDTH = N_HEADS * V_DIM
Q_LORA = 384
KV_LORA = 256
ROPE_THETA = 10000.0
IN_COLS = S5_WIDTH + Q_LORA + KV_LORA + QK_ROPE
N_EXPERT_GROUPS = 4
EXPERTS_PER_GROUP = 8
N_EXPERTS = N_EXPERT_GROUPS * EXPERTS_PER_GROUP
TOP_K_INNER = 2
D_EXPERT = 256
EPS = 1e-6
NEG_INF = -1e30

kernel_name = 'hymba_s5_mla_hiermoe_stream_step'


def rms_norm(x, g):
    xf = x.astype(jnp.float32)
    y = xf * lax.rsqrt(jnp.mean(xf * xf, axis=-1, keepdims=True) + EPS)
    return (y * g.astype(jnp.float32)).astype(x.dtype)


def rope_tables(pos):
    half = QK_ROPE // 2
    inv = ROPE_THETA ** (-jnp.arange(half, dtype=jnp.float32) / half)
    ang = pos.astype(jnp.float32)[:, None] * inv[None, :]
    return jnp.cos(ang), jnp.sin(ang)


def apply_rope(x, cos, sin):
    xf = x.astype(jnp.float32)
    x1, x2 = jnp.split(xf, 2, axis=-1)
    return jnp.concatenate([x1 * cos - x2 * sin, x1 * sin + x2 * cos], axis=-1).astype(x.dtype)


def _complex_affine_combine(e1, e2):
    a1r, a1i, b1r, b1i = e1
    a2r, a2i, b2r, b2i = e2
    return (a1r * a2r - a1i * a2i,
            a1r * a2i + a1i * a2r,
            a2r * b1r - a2i * b1i + b2r,
            a2r * b1i + a2i * b1r + b2i)


def s5_mixer(u, h0_re, h0_im, a_re, a_im, log_dt, b_re, b_im, c_re, c_im, d, w_glu, b_glu):
    f32 = jnp.float32
    bsz, seq, _ = u.shape
    uf = u.astype(f32).reshape(bsz, seq, S5_GROUPS, S5_GROUP)
    dt = jnp.exp(log_dt.astype(f32))[:, None]
    ar = a_re.astype(f32)
    ai = a_im.astype(f32)
    mag = jnp.exp(dt * ar)
    abar_re = mag * jnp.cos(dt * ai)
    abar_im = mag * jnp.sin(dt * ai)
    den = ar * ar + ai * ai
    num_re = abar_re - 1.0
    coef_re = (num_re * ar + abar_im * ai) / den
    coef_im = (abar_im * ar - num_re * ai) / den
    br = b_re.astype(f32)
    bi = b_im.astype(f32)
    bbar_re = coef_re[..., None] * br - coef_im[..., None] * bi
    bbar_im = coef_re[..., None] * bi + coef_im[..., None] * br
    x_re = jnp.einsum('bsgi,gni->bsgn', uf, bbar_re)
    x_im = jnp.einsum('bsgi,gni->bsgn', uf, bbar_im)
    a_seq_re = jnp.broadcast_to(abar_re[None, None], (1, seq, S5_GROUPS, S5_STATE))
    a_seq_im = jnp.broadcast_to(abar_im[None, None], (1, seq, S5_GROUPS, S5_STATE))
    acum_re, acum_im, h_re, h_im = lax.associative_scan(
        _complex_affine_combine, (a_seq_re, a_seq_im, x_re, x_im), axis=1)
    if h0_re is not None:
        g_re = h0_re.astype(f32)[:, None]
        g_im = h0_im.astype(f32)[:, None]
        h_re, h_im = (h_re + acum_re * g_re - acum_im * g_im,
                      h_im + acum_re * g_im + acum_im * g_re)
    y = (jnp.einsum('bsgn,gon->bsgo', h_re, c_re.astype(f32))
         - jnp.einsum('bsgn,gon->bsgo', h_im, c_im.astype(f32)))
    y = y.reshape(bsz, seq, S5_WIDTH) + d.astype(f32) * uf.reshape(bsz, seq, S5_WIDTH)
    z = jax.nn.gelu(y)
    out = z * jax.nn.sigmoid(z @ w_glu.astype(f32) + b_glu.astype(f32))
    return out.astype(u.dtype), h_re[:, -1].astype(u.dtype), h_im[:, -1].astype(u.dtype)


def chunk_causal_attention(q, k, v, q_pos, k_pos):
    bsz, sq, nh, _ = q.shape
    scale = QK_DIM ** -0.5
    kf = k.astype(jnp.float32)
    vf = v.astype(jnp.float32)
    k_chunk = k_pos // CHUNK

    def block(qb, qpos_b):
        s = jnp.einsum('bqhd,bkhd->bhqk', qb.astype(jnp.float32), kf) * scale
        mask = k_chunk[None, :] <= (qpos_b // CHUNK)[:, None]
        s = jnp.where(mask[None, None], s, NEG_INF)
        p = jax.nn.softmax(s, axis=-1)
        return jnp.einsum('bhqk,bkhd->bqhd', p, vf).astype(q.dtype)

    if sq > Q_BLOCK and sq % Q_BLOCK == 0:
        nb = sq // Q_BLOCK
        qb = q.reshape(bsz, nb, Q_BLOCK, nh, QK_DIM).transpose(1, 0, 2, 3, 4)
        pb = q_pos.reshape(nb, Q_BLOCK)
        out = lax.map(lambda a: block(a[0], a[1]), (qb, pb))
        return out.transpose(1, 0, 2, 3, 4).reshape(bsz, sq, nh, V_DIM)
    return block(q, q_pos)


def hierarchical_moe(x, w_rg, b_rg, w_re, b_re, w_g, w_u, w_d):
    f32 = jnp.float32
    bsz, seq, dm = x.shape
    t = x.reshape(bsz * seq, dm)
    g_prob = jax.nn.softmax((t @ w_rg).astype(f32) + b_rg.astype(f32), axis=-1)
    g_top, g_idx = lax.top_k(g_prob, 1)
    e_logits = ((t @ w_re).astype(f32) + b_re.astype(f32)).reshape(-1, N_EXPERT_GROUPS, EXPERTS_PER_GROUP)
    e_in = jnp.take_along_axis(e_logits, g_idx[:, :, None], axis=1)[:, 0]
    e_top, e_idx = lax.top_k(e_in, TOP_K_INNER)
    e_w = jax.nn.softmax(e_top, axis=-1) * g_top
    expert_id = g_idx * EXPERTS_PER_GROUP + e_idx
    gates = jnp.sum(jax.nn.one_hot(expert_id, N_EXPERTS, dtype=f32) * e_w[..., None], axis=1)
    out = jnp.zeros((bsz * seq, dm), f32)
    for e in range(N_EXPERTS):
        hdn = jax.nn.silu(t @ w_g[e]) * (t @ w_u[e])
        out = out + gates[:, e:e + 1] * (hdn @ w_d[e]).astype(f32)
    return out.astype(x.dtype).reshape(bsz, seq, dm)


def trunk_layer(x, past_ckv, past_krope, h0_re, h0_im, p):
    bsz, seq, _ = x.shape
    past_len = 0 if past_ckv is None else past_ckv.shape[1]
    q_pos = past_len + jnp.arange(seq, dtype=jnp.int32)
    k_pos = jnp.arange(past_len + seq, dtype=jnp.int32)

    xn = rms_norm(x, p['norm_mix_g'])
    proj = xn @ p['w_in']
    u, c_q, c_kv, k_rope_raw = jnp.split(
        proj, [S5_WIDTH, S5_WIDTH + Q_LORA, S5_WIDTH + Q_LORA + KV_LORA], axis=-1)

    s5_out, h_re, h_im = s5_mixer(u, h0_re, h0_im, p['s5_a_re'], p['s5_a_im'], p['s5_log_dt'],
                                  p['s5_b_re'], p['s5_b_im'], p['s5_c_re'], p['s5_c_im'],
                                  p['s5_d'], p['s5_w_glu'], p['s5_b_glu'])

    cos, sin = rope_tables(q_pos)
    q = jnp.einsum('bsc,chd->bshd', rms_norm(c_q, p['q_lora_norm_g']), p['w_uq'])
    q_nope = rms_norm(q[..., :QK_NOPE], p['q_nope_norm_g'])
    q_rope = apply_rope(rms_norm(q[..., QK_NOPE:], p['q_rope_norm_g']), cos[:, None], sin[:, None])
    new_ckv = rms_norm(c_kv, p['kv_lora_norm_g'])
    new_krope = apply_rope(rms_norm(k_rope_raw, p['k_rope_norm_g']), cos, sin)
    if past_ckv is None:
        all_ckv, all_krope = new_ckv, new_krope
    else:
        all_ckv = jnp.concatenate([past_ckv.astype(new_ckv.dtype), new_ckv], axis=1)
        all_krope = jnp.concatenate([past_krope.astype(new_krope.dtype), new_krope], axis=1)
    sk = all_ckv.shape[1]
    k_nope = rms_norm(jnp.einsum('bkc,chd->bkhd', all_ckv, p['w_uk']), p['k_nope_norm_g'])
    v = jnp.einsum('bkc,chd->bkhd', all_ckv, p['w_uv'])
    k = jnp.concatenate([k_nope, jnp.broadcast_to(all_krope[:, :, None, :], (bsz, sk, N_HEADS, QK_ROPE))], axis=-1)
    qf = jnp.concatenate([q_nope, q_rope], axis=-1)
    attn = chunk_causal_attention(qf, k, v, q_pos, k_pos).reshape(bsz, seq, ATTN_WIDTH)

    merged = jnp.concatenate([rms_norm(s5_out, p['out_norm_s5_g']), rms_norm(attn, p['out_norm_attn_g'])], axis=-1)
    h = x + merged @ p['w_out']

    y = h + hierarchical_moe(rms_norm(h, p['norm_ffn_g']), p['w_router_group'], p['b_router_group'],
                             p['w_router_expert'], p['b_router_expert'],
                             p['w_e_gate'], p['w_e_up'], p['w_e_down'])
    return y, new_ckv, new_krope, h_re, h_im


def setup_inputs(seed: int = 0) -> dict:
    key = jax.random.key(seed)
    ks = jax.random.split(key, 40)
    f32 = jnp.float32

    def nrm(k, shape, scale):
        return jax.random.normal(k, shape, f32) * scale

    def gain(k, n):
        return 1.0 + 0.01 * jax.random.normal(k, (DEPTH, n), f32)

    return {
        'x_prompt': nrm(ks[0], (BATCH, SEQ, D_MODEL), 1.0),
        'x_sample': nrm(ks[1], (DEC_BATCH, DEC_SEQ, D_MODEL), 1.0),
        'cache_ckv': nrm(ks[2], (DEPTH, DEC_BATCH, PAST_LEN, KV_LORA), 1.0),
        'cache_krope': nrm(ks[3], (DEPTH, DEC_BATCH, PAST_LEN, QK_ROPE), 1.0),
        'state_s5_re': nrm(ks[4], (DEPTH, DEC_BATCH, S5_GROUPS, S5_STATE), 0.5),
        'state_s5_im': nrm(ks[5], (DEPTH, DEC_BATCH, S5_GROUPS, S5_STATE), 0.5),
        'norm_mix_g': gain(ks[6], D_MODEL),
        'w_in': nrm(ks[7], (DEPTH, D_MODEL, IN_COLS), D_MODEL ** -0.5),
        's5_a_re': -0.5 + 0.01 * nrm(ks[8], (DEPTH, S5_GROUPS, S5_STATE), 1.0),
        's5_a_im': jnp.broadcast_to(math.pi * jnp.arange(S5_STATE, dtype=f32), (DEPTH, S5_GROUPS, S5_STATE)),
        's5_log_dt': jax.random.uniform(ks[9], (DEPTH, S5_GROUPS), f32, math.log(DT_MIN), math.log(DT_MAX)),
        's5_b_re': nrm(ks[10], (DEPTH, S5_GROUPS, S5_STATE, S5_GROUP), (2 * S5_GROUP) ** -0.5),
        's5_b_im': nrm(ks[11], (DEPTH, S5_GROUPS, S5_STATE, S5_GROUP), (2 * S5_GROUP) ** -0.5),
        's5_c_re': nrm(ks[12], (DEPTH, S5_GROUPS, S5_GROUP, S5_STATE), (2 * S5_STATE) ** -0.5),
        's5_c_im': nrm(ks[13], (DEPTH, S5_GROUPS, S5_GROUP, S5_STATE), (2 * S5_STATE) ** -0.5),
        's5_d': nrm(ks[14], (DEPTH, S5_WIDTH), 1.0),
        's5_w_glu': nrm(ks[15], (DEPTH, S5_WIDTH, S5_WIDTH), S5_WIDTH ** -0.5),
        's5_b_glu': nrm(ks[16], (DEPTH, S5_WIDTH), 0.01),
        'q_lora_norm_g': gain(ks[17], Q_LORA),
        'w_uq': nrm(ks[18], (DEPTH, Q_LORA, N_HEADS, QK_DIM), Q_LORA ** -0.5),
        'kv_lora_norm_g': gain(ks[19], KV_LORA),
        'w_uk': nrm(ks[20], (DEPTH, KV_LORA, N_HEADS, QK_NOPE), KV_LORA ** -0.5),
        'w_uv': nrm(ks[21], (DEPTH, KV_LORA, N_HEADS, V_DIM), KV_LORA ** -0.5),
        'q_nope_norm_g': gain(ks[22], QK_NOPE),
        'q_rope_norm_g': gain(ks[23], QK_ROPE),
        'k_nope_norm_g': gain(ks[24], QK_NOPE),
        'k_rope_norm_g': gain(ks[25], QK_ROPE),
        'out_norm_s5_g': gain(ks[26], S5_WIDTH),
        'out_norm_attn_g': gain(ks[27], ATTN_WIDTH),
        'w_out': nrm(ks[28], (DEPTH, D_MIX, D_MODEL), D_MIX ** -0.5),
        'norm_ffn_g': gain(ks[29], D_MODEL),
        'w_router_group': nrm(ks[30], (DEPTH, D_MODEL, N_EXPERT_GROUPS), D_MODEL ** -0.5),
        'b_router_group': nrm(ks[31], (DEPTH, N_EXPERT_GROUPS), 0.01),
        'w_router_expert': nrm(ks[32], (DEPTH, D_MODEL, N_EXPERTS), D_MODEL ** -0.5),
        'b_router_expert': nrm(ks[33], (DEPTH, N_EXPERTS), 0.01),
        'w_e_gate': nrm(ks[34], (DEPTH, N_EXPERTS, D_MODEL, D_EXPERT), D_MODEL ** -0.5),
        'w_e_up': nrm(ks[35], (DEPTH, N_EXPERTS, D_MODEL, D_EXPERT), D_MODEL ** -0.5),
        'w_e_down': nrm(ks[36], (DEPTH, N_EXPERTS, D_EXPERT, D_MODEL), D_EXPERT ** -0.5),
    }


def reference(x_prompt, x_sample, cache_ckv, cache_krope, state_s5_re, state_s5_im,
              norm_mix_g, w_in, s5_a_re, s5_a_im, s5_log_dt, s5_b_re, s5_b_im, s5_c_re, s5_c_im,
              s5_d, s5_w_glu, s5_b_glu, q_lora_norm_g, w_uq, kv_lora_norm_g, w_uk, w_uv,
              q_nope_norm_g, q_rope_norm_g, k_nope_norm_g, k_rope_norm_g,
              out_norm_s5_g, out_norm_attn_g, w_out, norm_ffn_g,
              w_router_group, b_router_group, w_router_expert, b_router_expert,
              w_e_gate, w_e_up, w_e_down):
    y_prompt = x_prompt
    y_sample = x_sample
    ckv_p, krope_p, s5re_p, s5im_p = [], [], [], []
    ckv_s, krope_s, s5re_s, s5im_s = [], [], [], []
    for l in range(DEPTH):
        p = {
            'norm_mix_g': norm_mix_g[l], 'w_in': w_in[l],
            's5_a_re': s5_a_re[l], 's5_a_im': s5_a_im[l], 's5_log_dt': s5_log_dt[l],
            's5_b_re': s5_b_re[l], 's5_b_im': s5_b_im[l], 's5_c_re': s5_c_re[l], 's5_c_im': s5_c_im[l],
            's5_d': s5_d[l], 's5_w_glu': s5_w_glu[l], 's5_b_glu': s5_b_glu[l],
            'q_lora_norm_g': q_lora_norm_g[l], 'w_uq': w_uq[l], 'kv_lora_norm_g': kv_lora_norm_g[l],
            'w_uk': w_uk[l], 'w_uv': w_uv[l],
            'q_nope_norm_g': q_nope_norm_g[l], 'q_rope_norm_g': q_rope_norm_g[l],
            'k_nope_norm_g': k_nope_norm_g[l], 'k_rope_norm_g': k_rope_norm_g[l],
            'out_norm_s5_g': out_norm_s5_g[l], 'out_norm_attn_g': out_norm_attn_g[l], 'w_out': w_out[l],
            'norm_ffn_g': norm_ffn_g[l],
            'w_router_group': w_router_group[l], 'b_router_group': b_router_group[l],
            'w_router_expert': w_router_expert[l], 'b_router_expert': b_router_expert[l],
            'w_e_gate': w_e_gate[l], 'w_e_up': w_e_up[l], 'w_e_down': w_e_down[l],
        }
        y_prompt, c1, k1, r1, i1 = trunk_layer(y_prompt, None, None, None, None, p)
        y_sample, c2, k2, r2, i2 = trunk_layer(y_sample, cache_ckv[l], cache_krope[l],
                                               state_s5_re[l], state_s5_im[l], p)
        ckv_p.append(c1); krope_p.append(k1); s5re_p.append(r1); s5im_p.append(i1)
        ckv_s.append(c2); krope_s.append(k2); s5re_s.append(r2); s5im_s.append(i2)
    new_ckv_prompt = jnp.stack(ckv_p)
    new_krope_prompt = jnp.stack(krope_p)
    new_s5_re_prompt = jnp.stack(s5re_p)
    new_s5_im_prompt = jnp.stack(s5im_p)
    new_ckv_sample = jnp.stack(ckv_s)
    new_krope_sample = jnp.stack(krope_s)
    new_s5_re_sample = jnp.stack(s5re_s)
    new_s5_im_sample = jnp.stack(s5im_s)
    return (y_prompt, y_sample, new_ckv_prompt, new_krope_prompt, new_s5_re_prompt, new_s5_im_prompt,
            new_ckv_sample, new_krope_sample, new_s5_re_sample, new_s5_im_sample)
```

```python
import functools
import math

import jax
import jax.numpy as jnp
from jax import lax
from jax.experimental import pallas as pl
from jax.experimental.pallas import tpu as pltpu

F32 = jnp.float32
BF16 = jnp.bfloat16

CHUNK = 64
S5_GROUP = 16
S5_STATE = 64
N_HEADS = 8
QK_NOPE = 64
QK_ROPE = 32
QK_DIM = QK_NOPE + QK_ROPE
V_DIM = 64
ROPE_THETA = 10000.0
N_EXPERT_GROUPS = 4
EXPERTS_PER_GROUP = 8
N_EXPERTS = N_EXPERT_GROUPS * EXPERTS_PER_GROUP
EPS = 1e-6
NEG_INF = -1e30

LANES = 128
SUBLANES = 8
HEAD_PAD = LANES
VMEM_LIMIT = 48 * 1024 * 1024


def _tile(n, cap):
    for t in range(min(cap, n), 15, -1):
        if n % t == 0 and t % 16 == 0:
            return t
    return n


def _rms(x, g):
    return x * lax.rsqrt(jnp.mean(x * x, axis=-1, keepdims=True) + EPS) * g


def _dot(a, b):
    return jnp.dot(a, b, preferred_element_type=F32)


def _lane_iota(shape):
    return lax.broadcasted_iota(jnp.int32, shape, len(shape) - 1)


def _proj_kernel(x_ref, gmix_ref, win_ref, gq_ref, wuq_ref, bdq_ref, qgain_ref, gkv_ref, gkr_ref, tab_ref,
                 u_ref, q_ref, ckv_ref, kr_ref, *, s5_width, q_lora, kv_lora):
    x = x_ref[0]
    xn = _rms(x, gmix_ref[...])
    proj = _dot(xn.astype(BF16), win_ref[...])
    u_ref[...] = proj[:, :s5_width].astype(BF16)

    o = s5_width
    cqn = _rms(proj[:, o:o + q_lora], gq_ref[...])
    o += q_lora
    ckv_ref[0] = _rms(proj[:, o:o + kv_lora], gkv_ref[...])
    o += kv_lora

    tab = tab_ref[...]
    lane = _lane_iota((x.shape[0], LANES))

    kr = proj[:, o:o + LANES]
    ms = jnp.sum(jnp.where(lane < QK_ROPE, kr * kr, 0.0), axis=-1, keepdims=True) * (1.0 / QK_ROPE)
    t = kr * lax.rsqrt(ms + EPS) * gkr_ref[...] * tab[:, LANES:]
    kro = t + pltpu.roll(t, LANES - QK_ROPE, axis=1)
    kr_ref[0] = kro[:, :QK_ROPE]

    qa = _dot(cqn.astype(BF16), wuq_ref[...])
    qsq = (qa * qa).astype(BF16)
    pair = 2 * HEAD_PAD
    ms = jnp.concatenate([_dot(qsq[:, c * pair:(c + 1) * pair], bdq_ref[...])
                          for c in range(N_HEADS // 2)], axis=-1)
    qn = qa * lax.rsqrt(ms + EPS) * qgain_ref[...]
    is_rope = (lane >= QK_NOPE) & (lane < QK_DIM)
    for h in range(N_HEADS):
        th = qn[:, h * HEAD_PAD:(h + 1) * HEAD_PAD] * tab[:, :LANES]
        rolled = pltpu.roll(th, LANES - QK_ROPE, axis=1)
        oh = jnp.where(lane < QK_DIM, th + jnp.where(is_rope, rolled, 0.0), 0.0)
        q_ref[0, h] = oh.astype(BF16)


def _proj_call(x, tab, w, *, tm):
    bsz, seq, dm = x.shape
    s5_width, q_lora, kv_lora = w['s5_width'], w['q_lora'], w['kv_lora']
    in_cols = w['w_in'].shape[1]
    hp = N_HEADS * HEAD_PAD
    full = lambda shape: pl.BlockSpec(shape, lambda b, i: (0,) * len(shape))
    kern = functools.partial(_proj_kernel, s5_width=s5_width, q_lora=q_lora, kv_lora=kv_lora)
    return pl.pallas_call(
        kern,
        grid=(bsz, seq // tm),
        in_specs=[
            pl.BlockSpec((1, tm, dm), lambda b, i: (b, i, 0)),
            full((1, dm)), full((dm, in_cols)), full((1, q_lora)), full((q_lora, hp)),
            full((2 * HEAD_PAD, 2 * HEAD_PAD)), full((1, hp)), full((1, kv_lora)), full((1, LANES)),
            pl.BlockSpec((tm, 2 * LANES), lambda b, i: (i, 0)),
        ],
        out_specs=[
            pl.BlockSpec((tm, s5_width), lambda b, i: (i, b)),
            pl.BlockSpec((1, N_HEADS, tm, HEAD_PAD), lambda b, i: (b, 0, i, 0)),
            pl.BlockSpec((1, tm, kv_lora), lambda b, i: (b, i, 0)),
            pl.BlockSpec((1, tm, QK_ROPE), lambda b, i: (b, i, 0)),
        ],
        out_shape=[
            jax.ShapeDtypeStruct((seq, bsz * s5_width), BF16),
            jax.ShapeDtypeStruct((bsz, N_HEADS, seq, HEAD_PAD), BF16),
            jax.ShapeDtypeStruct((bsz, seq, kv_lora), F32),
            jax.ShapeDtypeStruct((bsz, seq, QK_ROPE), F32),
        ],
        compiler_params=pltpu.CompilerParams(
            dimension_semantics=("parallel", "parallel"), vmem_limit_bytes=VMEM_LIMIT),
    )(x, w['g_mix'], w['w_in'], w['g_q'], w['w_uq'], w['bd_q'], w['q_gain'], w['g_kv'], w['g_kr'], tab)


def _kv_kernel(ckv_ref, krp_ref, wuk_ref, wuv_ref, bdk_ref, kg_ref, k_ref, v_ref):
    c = ckv_ref[0].astype(BF16)
    ka = _dot(c, wuk_ref[...])
    va = _dot(c, wuv_ref[...])
    ksq = (ka * ka).astype(BF16)
    pair = 2 * HEAD_PAD
    ms = jnp.concatenate([_dot(ksq[:, c2 * pair:(c2 + 1) * pair], bdk_ref[...])
                          for c2 in range(N_HEADS // 2)], axis=-1)
    kn = ka * lax.rsqrt(ms + EPS) * kg_ref[...]
    krp = krp_ref[0]
    lane = _lane_iota(krp.shape)
    ones_col = jnp.where(lane == V_DIM, 1.0, 0.0)
    for h in range(N_HEADS):
        k_ref[0, h] = (kn[:, h * HEAD_PAD:(h + 1) * HEAD_PAD] + krp).astype(BF16)
        v_ref[0, h] = (va[:, h * HEAD_PAD:(h + 1) * HEAD_PAD] + ones_col).astype(BF16)


def _kv_call(ckv, krp, w, *, tm):
    bsz, sk, kv_lora = ckv.shape
    hp = N_HEADS * HEAD_PAD
    full = lambda shape: pl.BlockSpec(shape, lambda b, i: (0,) * len(shape))
    head_spec = pl.BlockSpec((1, N_HEADS, tm, HEAD_PAD), lambda b, i: (b, 0, i, 0))
    return pl.pallas_call(
        _kv_kernel,
        grid=(bsz, sk // tm),
        in_specs=[
            pl.BlockSpec((1, tm, kv_lora), lambda b, i: (b, i, 0)),
            pl.BlockSpec((1, tm, LANES), lambda b, i: (b, i, 0)),
            full((kv_lora, hp)), full((kv_lora, hp)), full((2 * HEAD_PAD, 2 * HEAD_PAD)), full((1, hp)),
        ],
        out_specs=[head_spec, head_spec],
        out_shape=[jax.ShapeDtypeStruct((bsz, N_HEADS, sk, HEAD_PAD), BF16)] * 2,
        compiler_params=pltpu.CompilerParams(
            dimension_semantics=("parallel", "parallel"), vmem_limit_bytes=VMEM_LIMIT),
    )(ckv, krp, w['w_uk'], w['w_uv'], w['bd_k'], w['k_gain'])


def _attn_kernel(q_ref, k_ref, v_ref, o_ref, *, tq, tk, q_pos0, sk_valid):
    sq = q_ref.shape[2]
    nq = sq // tq

    def kv_step(hh, q, p0, j, carry, masked):
        m, acc = carry
        k0 = pl.multiple_of(j * tk, tk)
        kb = k_ref[0, hh, pl.ds(k0, tk), :]
        s = lax.dot_general(q, kb, (((1,), (1,)), ((), ())), preferred_element_type=F32)
        if masked:
            q_chunk = (p0 + lax.broadcasted_iota(jnp.int32, (tq, 1), 0)) // CHUNK
            k_pos = k0 + lax.broadcasted_iota(jnp.int32, (1, tk), 1)
            s = jnp.where((k_pos // CHUNK <= q_chunk) & (k_pos < sk_valid), s, NEG_INF)
        m_new = jnp.maximum(m, jnp.max(s, axis=-1, keepdims=True))
        alpha = jnp.exp(m - m_new)
        p = jnp.exp(s - m_new)
        acc = alpha * acc + _dot(p.astype(BF16), v_ref[0, hh, pl.ds(k0, tk), :])
        return m_new, acc

    for hh in range(2):
        def q_body(qi, carry, hh=hh):
            q0 = pl.multiple_of(qi * tq, tq)
            q = q_ref[0, hh, pl.ds(q0, tq), :]
            p0 = q_pos0 + q0
            n_open = jnp.minimum((p0 // CHUNK + 1) * CHUNK, sk_valid) // tk
            v1 = jnp.minimum(((p0 + tq - 1) // CHUNK + 1) * CHUNK, sk_valid)
            n_all = (v1 + tk - 1) // tk
            init = (jnp.full((tq, 1), NEG_INF, F32), jnp.zeros((tq, HEAD_PAD), F32))
            st = lax.fori_loop(0, n_open, lambda j, c: kv_step(hh, q, p0, j, c, False), init)
            _, acc = lax.fori_loop(n_open, n_all, lambda j, c: kv_step(hh, q, p0, j, c, True), st)
            o = acc[:, :V_DIM] / acc[:, V_DIM:V_DIM + 1]
            o_ref[0, pl.ds(q0, tq), hh * V_DIM:(hh + 1) * V_DIM] = o.astype(o_ref.dtype)
            return carry

        lax.fori_loop(0, nq, q_body, 0)


def _attn_call(q, k, v, *, tq, tk, q_pos0, sk_valid):
    bsz, nh, sq, _ = q.shape
    sk = k.shape[2]
    kern = functools.partial(_attn_kernel, tq=tq, tk=tk, q_pos0=q_pos0, sk_valid=sk_valid)
    return pl.pallas_call(
        kern,
        grid=(bsz, nh // 2),
        in_specs=[
            pl.BlockSpec((1, 2, sq, HEAD_PAD), lambda b, h: (b, h, 0, 0)),
            pl.BlockSpec((1, 2, sk, HEAD_PAD), lambda b, h: (b, h, 0, 0)),
            pl.BlockSpec((1, 2, sk, HEAD_PAD), lambda b, h: (b, h, 0, 0)),
        ],
        out_specs=pl.BlockSpec((1, sq, 2 * V_DIM), lambda b, h: (b, 0, h)),
        out_shape=jax.ShapeDtypeStruct((bsz, sq, nh * V_DIM), BF16),
        compiler_params=pltpu.CompilerParams(
            dimension_semantics=("parallel", "parallel"), vmem_limit_bytes=VMEM_LIMIT),
    )(q, k, v)


def _s5_kernel(u_ref, h0_ref, lre_ref, lim_ref, bbd_ref, cbd_ref, d_ref, wglu_ref, bglu_ref,
               y_ref, hout_ref, hbuf, hstate, *, ts, n_half, col_chunk):
    i = pl.program_id(0)
    half = n_half
    width = hbuf.shape[1]
    uw = u_ref.shape[1] // 2

    @pl.when(i == 0)
    def _():
        hstate[...] = h0_ref[...]

    u = u_ref[...]
    for c in range(2):
        hbuf[:, c * 2 * half:(c + 1) * 2 * half] = _dot(u[:, c * uw:(c + 1) * uw], bbd_ref[c])

    for c in range(2):
        for cc in range(half // col_chunk):
            re0 = c * 2 * half + cc * col_chunk
            im0 = re0 + half
            l0 = c * half + cc * col_chunk
            ar = lre_ref[:, l0:l0 + col_chunk]
            ai = lim_ref[:, l0:l0 + col_chunk]

            def step(s, carry, re0=re0, im0=im0, ar=ar, ai=ai):
                hr, hi = carry
                r0 = pl.multiple_of(s * SUBLANES, SUBLANES)
                nr = ar * hr - ai * hi + hbuf[pl.ds(r0, SUBLANES), re0:re0 + col_chunk]
                ni = ar * hi + ai * hr + hbuf[pl.ds(r0, SUBLANES), im0:im0 + col_chunk]
                hbuf[pl.ds(r0, SUBLANES), re0:re0 + col_chunk] = nr
                hbuf[pl.ds(r0, SUBLANES), im0:im0 + col_chunk] = ni
                return nr, ni

            hr, hi = lax.fori_loop(
                0, ts, step,
                (hstate[:, re0:re0 + col_chunk], hstate[:, im0:im0 + col_chunk]), unroll=8)
            hstate[:, re0:re0 + col_chunk] = hr
            hstate[:, im0:im0 + col_chunk] = hi

    hout_ref[...] = hstate[...]

    y = jnp.concatenate(
        [_dot(hbuf[:, c * 2 * half:(c + 1) * 2 * half].astype(BF16), cbd_ref[c]) for c in range(2)], axis=-1)
    y = y + d_ref[...] * u.astype(F32)
    z = jax.nn.gelu(y)
    gate = _dot(z.astype(BF16), wglu_ref[...]) + bglu_ref[...]
    y_ref[...] = (z * jax.nn.sigmoid(gate)).astype(y_ref.dtype)


def _s5_call(u_tb, h0, w, *, ts):
    rows, s5_width = u_tb.shape
    n_state = w['lam_re'].shape[1]
    half = n_state // 2
    width = 2 * n_state
    n_steps = rows // (ts * SUBLANES)
    full = lambda shape: pl.BlockSpec(shape, lambda i: (0,) * len(shape))
    kern = functools.partial(_s5_kernel, ts=ts, n_half=half, col_chunk=512)
    return pl.pallas_call(
        kern,
        grid=(n_steps,),
        in_specs=[
            pl.BlockSpec((ts * SUBLANES, s5_width), lambda i: (i, 0)),
            full((SUBLANES, width)), full((SUBLANES, n_state)), full((SUBLANES, n_state)),
            full((2, s5_width // 2, 2 * half)), full((2, 2 * half, s5_width // 2)),
            full((1, s5_width)), full((s5_width, s5_width)), full((1, s5_width)),
        ],
        out_specs=[
            pl.BlockSpec((ts * SUBLANES, s5_width), lambda i: (i, 0)),
            full((SUBLANES, width)),
        ],
        out_shape=[
            jax.ShapeDtypeStruct((rows, s5_width), BF16),
            jax.ShapeDtypeStruct((SUBLANES, width), F32),
        ],
        scratch_shapes=[
            pltpu.VMEM((ts * SUBLANES, width), F32),
            pltpu.VMEM((SUBLANES, width), F32),
        ],
        compiler_params=pltpu.CompilerParams(
            dimension_semantics=("arbitrary",), vmem_limit_bytes=VMEM_LIMIT),
    )(u_tb, h0, w['lam_re'], w['lam_im'], w['b_bd'], w['c_bd'], w['s5_d'], w['w_glu'], w['b_glu'])


def _out_kernel(x_ref, s5_ref, at_ref, gs5_ref, gat_ref, wout_ref, gffn_ref, wr_ref, br_ref,
                h_ref, hn_ref, gate_ref):
    s5n = _rms(s5_ref[...].astype(F32), gs5_ref[...])
    atn = _rms(at_ref[0].astype(F32), gat_ref[...])
    merged = jnp.concatenate([s5n, atn], axis=-1).astype(BF16)
    h = x_ref[0] + _dot(merged, wout_ref[...])
    h_ref[0] = h
    hn = _rms(h, gffn_ref[...]).astype(BF16)
    hn_ref[0] = hn

    logits = _dot(hn, wr_ref[...]) + br_ref[...]
    lane_i = _lane_iota(logits.shape)
    lane = lane_i.astype(F32)
    lane_group = (lane_i // EXPERTS_PER_GROUP).astype(F32)
    big = float(LANES)
    is_g = (lane_i >= N_EXPERTS) & (lane_i < N_EXPERTS + N_EXPERT_GROUPS)
    gl = jnp.where(is_g, logits, NEG_INF)
    gmax = jnp.max(gl, axis=-1, keepdims=True)
    g_idx = jnp.min(jnp.where(gl == gmax, lane, big), axis=-1, keepdims=True) - N_EXPERTS
    g_top = 1.0 / jnp.sum(jnp.where(is_g, jnp.exp(gl - gmax), 0.0), axis=-1, keepdims=True)
    in_group = (lane_i < N_EXPERTS) & (lane_group == g_idx)
    el = jnp.where(in_group, logits, NEG_INF)
    v1 = jnp.max(el, axis=-1, keepdims=True)
    i1 = jnp.min(jnp.where(el == v1, lane, big), axis=-1, keepdims=True)
    el2 = jnp.where(lane == i1, NEG_INF, el)
    v2 = jnp.max(el2, axis=-1, keepdims=True)
    i2 = jnp.min(jnp.where(el2 == v2, lane, big), axis=-1, keepdims=True)
    e21 = jnp.exp(v2 - v1)
    w1 = g_top / (1.0 + e21)
    w2 = g_top * e21 / (1.0 + e21)
    gate_ref[0] = jnp.where(lane == i1, w1, 0.0) + jnp.where(lane == i2, w2, 0.0)


def _out_call(x, s5_tb, attn, w, *, tm):
    bsz, seq, dm = x.shape
    s5_width = s5_tb.shape[1] // bsz
    aw = attn.shape[2]
    full = lambda shape: pl.BlockSpec(shape, lambda b, i: (0,) * len(shape))
    tok = lambda width: pl.BlockSpec((1, tm, width), lambda b, i: (b, i, 0))
    return pl.pallas_call(
        _out_kernel,
        grid=(bsz, seq // tm),
        in_specs=[
            tok(dm),
            pl.BlockSpec((tm, s5_width), lambda b, i: (i, b)),
            tok(aw),
            full((1, s5_width)), full((1, aw)), full((s5_width + aw, dm)), full((1, dm)),
            full((dm, LANES)), full((1, LANES)),
        ],
        out_specs=[tok(dm), tok(dm), tok(LANES)],
        out_shape=[
            jax.ShapeDtypeStruct((bsz, seq, dm), F32),
            jax.ShapeDtypeStruct((bsz, seq, dm), BF16),
            jax.ShapeDtypeStruct((bsz, seq, LANES), F32),
        ],
        compiler_params=pltpu.CompilerParams(
            dimension_semantics=("parallel", "parallel"), vmem_limit_bytes=VMEM_LIMIT),
    )(x, s5_tb, attn, w['g_s5'], w['g_attn'], w['w_out'], w['g_ffn'], w['w_router'], w['b_router'])


def _moe_kernel(hn_ref, gate_ref, h_ref, wgu_ref, wd_ref, o_ref, *, d_expert):
    e = pl.program_id(1)

    @pl.when(e == 0)
    def _():
        o_ref[...] = h_ref[...]

    gu = _dot(hn_ref[...], wgu_ref[0])
    hdn = jax.nn.silu(gu[:, :d_expert]) * gu[:, d_expert:]
    y = _dot(hdn.astype(BF16), wd_ref[0])
    gate = gate_ref[...]
    g = jnp.sum(jnp.where(_lane_iota(gate.shape) == e, gate, 0.0), axis=-1, keepdims=True)
    o_ref[...] += g * y


def _moe_call(hn, gates, h, w, *, tm):
    t, dm = hn.shape
    d_expert = w['w_d'].shape[1]
    kern = functools.partial(_moe_kernel, d_expert=d_expert)
    return pl.pallas_call(
        kern,
        grid=(t // tm, N_EXPERTS),
        in_specs=[
            pl.BlockSpec((tm, dm), lambda i, e: (i, 0)),
            pl.BlockSpec((tm, LANES), lambda i, e: (i, 0)),
            pl.BlockSpec((tm, dm), lambda i, e: (i, 0)),
            pl.BlockSpec((1, dm, 2 * d_expert), lambda i, e: (e, 0, 0)),
            pl.BlockSpec((1, d_expert, dm), lambda i, e: (e, 0, 0)),
        ],
        out_specs=pl.BlockSpec((tm, dm), lambda i, e: (i, 0)),
        out_shape=jax.ShapeDtypeStruct((t, dm), F32),
        compiler_params=pltpu.CompilerParams(
            dimension_semantics=("parallel", "arbitrary"), vmem_limit_bytes=VMEM_LIMIT),
    )(hn, gates, h, w['w_gu'], w['w_d'])


def _prepare_weights(p):
    dm, in_cols = p['w_in'].shape
    s5_width = p['s5_d'].shape[0]
    q_lora = p['q_lora_norm_g'].shape[0]
    kv_lora = p['kv_lora_norm_g'].shape[0]
    half_r = QK_ROPE // 2
    w = {'s5_width': s5_width, 'q_lora': q_lora, 'kv_lora': kv_lora}

    def partner(a):
        return jnp.concatenate([-a[..., half_r:], a[..., :half_r]], axis=-1)

    def swap(a):
        return jnp.concatenate([a[..., half_r:], a[..., :half_r]], axis=-1)

    w_kr = p['w_in'][:, in_cols - QK_ROPE:]
    w['w_in'] = jnp.concatenate(
        [p['w_in'], partner(w_kr), jnp.zeros((dm, LANES - 2 * QK_ROPE), F32)], axis=-1).astype(BF16)
    w['g_mix'] = p['norm_mix_g'][None]
    w['g_q'] = p['q_lora_norm_g'][None]
    w['g_kv'] = p['kv_lora_norm_g'][None]
    gkr = p['k_rope_norm_g']
    w['g_kr'] = jnp.concatenate([gkr, swap(gkr), jnp.zeros((LANES - 2 * QK_ROPE,), F32)])[None]

    wq = p['w_uq']
    wq_r = wq[..., QK_NOPE:]
    w['w_uq'] = jnp.concatenate([wq, partner(wq_r)], axis=-1).reshape(q_lora, N_HEADS * HEAD_PAD).astype(BF16)
    gr = p['q_rope_norm_g']
    w['q_gain'] = jnp.tile(jnp.concatenate([p['q_nope_norm_g'], gr, swap(gr)]), N_HEADS)[None]
    idx = jnp.arange(HEAD_PAD)
    row_nope = (idx < QK_NOPE)[:, None]
    row_rope = ((idx >= QK_NOPE) & (idx < QK_DIM))[:, None]
    col_nope = (idx < QK_NOPE)[None, :]
    m_q = jnp.where(row_nope & col_nope, 1.0 / QK_NOPE, 0.0) + jnp.where(row_rope & ~col_nope, 1.0 / QK_ROPE, 0.0)
    m_k = jnp.where(row_nope, 1.0 / QK_NOPE, 0.0) * jnp.ones((1, HEAD_PAD), F32)
    eye2 = jnp.eye(2, dtype=F32)
    w['bd_q'] = jnp.kron(eye2, m_q).astype(BF16)
    w['bd_k'] = jnp.kron(eye2, m_k).astype(BF16)

    pad_h = lambda a: jnp.pad(a, ((0, 0), (0, 0), (0, HEAD_PAD - a.shape[-1])))
    w['w_uk'] = pad_h(p['w_uk']).reshape(kv_lora, N_HEADS * HEAD_PAD).astype(BF16)
    w['w_uv'] = pad_h(p['w_uv']).reshape(kv_lora, N_HEADS * HEAD_PAD).astype(BF16)
    w['k_gain'] = jnp.tile(jnp.pad(p['k_nope_norm_g'], (0, HEAD_PAD - QK_NOPE)), N_HEADS)[None]

    dt = jnp.exp(p['s5_log_dt'])[:, None]
    ar, ai = p['s5_a_re'], p['s5_a_im']
    mag = jnp.exp(dt * ar)
    abar_re = mag * jnp.cos(dt * ai)
    abar_im = mag * jnp.sin(dt * ai)
    den = ar * ar + ai * ai
    num_re = abar_re - 1.0
    coef_re = (num_re * ar + abar_im * ai) / den
    coef_im = (abar_im * ar - num_re * ai) / den
    br, bi = p['s5_b_re'], p['s5_b_im']
    bbar_re = coef_re[..., None] * br - coef_im[..., None] * bi
    bbar_im = coef_re[..., None] * bi + coef_im[..., None] * br
    g = ar.shape[0]
    n_state = g * S5_STATE
    half = n_state // 2
    eye_g = jnp.eye(g, dtype=F32)
    b_re = jnp.einsum('gni,gh->gihn', bbar_re, eye_g).reshape(s5_width, n_state)
    b_im = jnp.einsum('gni,gh->gihn', bbar_im, eye_g).reshape(s5_width, n_state)
    c_re = jnp.einsum('gon,gh->gnho', p['s5_c_re'], eye_g).reshape(n_state, s5_width)
    c_im = jnp.einsum('gon,gh->gnho', p['s5_c_im'], eye_g).reshape(n_state, s5_width)
    uw = s5_width // 2
    w['b_bd'] = jnp.stack([
        jnp.concatenate([b_re[c * uw:(c + 1) * uw, c * half:(c + 1) * half],
                         b_im[c * uw:(c + 1) * uw, c * half:(c + 1) * half]], axis=1) for c in range(2)]).astype(BF16)
    w['c_bd'] = jnp.stack([
        jnp.concatenate([c_re[c * half:(c + 1) * half, c * uw:(c + 1) * uw],
                         -c_im[c * half:(c + 1) * half, c * uw:(c + 1) * uw]], axis=0) for c in range(2)]).astype(BF16)
    w['lam_re'] = jnp.broadcast_to(abar_re.reshape(1, n_state), (SUBLANES, n_state))
    w['lam_im'] = jnp.broadcast_to(abar_im.reshape(1, n_state), (SUBLANES, n_state))
    w['s5_d'] = p['s5_d'][None]
    w['w_glu'] = p['s5_w_glu'].astype(BF16)
    w['b_glu'] = p['s5_b_glu'][None]

    w['g_s5'] = p['out_norm_s5_g'][None]
    w['g_attn'] = p['out_norm_attn_g'][None]
    w['w_out'] = p['w_out'].astype(BF16)
    w['g_ffn'] = p['norm_ffn_g'][None]
    n_r = N_EXPERTS + N_EXPERT_GROUPS
    w['w_router'] = jnp.pad(jnp.concatenate([p['w_router_expert'], p['w_router_group']], axis=1),
                            ((0, 0), (0, LANES - n_r))).astype(BF16)
    w['b_router'] = jnp.pad(jnp.concatenate([p['b_router_expert'], p['b_router_group']]), (0, LANES - n_r))[None]
    w['w_gu'] = jnp.concatenate([p['w_e_gate'], p['w_e_up']], axis=-1).astype(BF16)
    w['w_d'] = p['w_e_down'].astype(BF16)
    return w


def _rope_table(pos):
    half = QK_ROPE // 2
    inv = ROPE_THETA ** (-jnp.arange(half, dtype=F32) / half)
    ang = pos.astype(F32)[:, None] * inv[None, :]
    cc = jnp.tile(jnp.cos(ang), (1, 2))
    ss = jnp.tile(jnp.sin(ang), (1, 2))
    n = pos.shape[0]
    scale = QK_DIM ** -0.5
    tab_q = scale * jnp.concatenate([jnp.ones((n, QK_NOPE), F32), cc, ss], axis=-1)
    tab_k = jnp.concatenate([cc, ss, jnp.zeros((n, LANES - 2 * QK_ROPE), F32)], axis=-1)
    return jnp.concatenate([tab_q, tab_k], axis=-1)


def _s5_state_in(re, im):
    b = re.shape[0]
    re = re.reshape(b, -1)
    im = im.reshape(b, -1)
    half = re.shape[1] // 2
    return jnp.concatenate([re[:, :half], im[:, :half], re[:, half:], im[:, half:]], axis=1)


def _s5_state_out(h, g):
    b = h.shape[0]
    half = h.shape[1] // 4
    re = jnp.concatenate([h[:, :half], h[:, 2 * half:3 * half]], axis=1).reshape(b, g, S5_STATE)
    im = jnp.concatenate([h[:, half:2 * half], h[:, 3 * half:]], axis=1).reshape(b, g, S5_STATE)
    return re, im


def _trunk_layer(x, past_ckv, past_krope, h0_re, h0_im, w):
    bsz, seq, dm = x.shape
    assert bsz == SUBLANES, "the S5 scan keeps the batch on the sublanes"
    past_len = 0 if past_ckv is None else past_ckv.shape[1]
    g = w['lam_re'].shape[1] // S5_STATE
    tm = _tile(seq, 512)

    tab = _rope_table(past_len + jnp.arange(seq, dtype=jnp.int32))
    u_tb, q, new_ckv, new_krope = _proj_call(x, tab, w, tm=tm)

    if h0_re is None:
        h0 = jnp.zeros((bsz, 2 * g * S5_STATE), F32)
    else:
        h0 = _s5_state_in(h0_re.astype(F32), h0_im.astype(F32))
    s5_tb, h_last = _s5_call(u_tb.reshape(seq * bsz, -1), h0, w, ts=min(seq, 64))
    h_re, h_im = _s5_state_out(h_last, g)

    if past_ckv is None:
        all_ckv, all_krope = new_ckv, new_krope
    else:
        all_ckv = jnp.concatenate([past_ckv.astype(F32), new_ckv], axis=1)
        all_krope = jnp.concatenate([past_krope.astype(F32), new_krope], axis=1)
    sk = all_ckv.shape[1]
    sk_pad = -(-sk // LANES) * LANES
    all_ckv = jnp.pad(all_ckv, ((0, 0), (0, sk_pad - sk), (0, 0)))
    krp = jnp.pad(all_krope, ((0, 0), (0, sk_pad - sk), (QK_NOPE, LANES - QK_DIM)))
    k, v = _kv_call(all_ckv, krp, w, tm=_tile(sk_pad, 512))
    if past_len == 0:
        tq = tk = _tile(seq, 512)
    else:
        tq, tk = _tile(seq, 512), sk_pad
    attn = _attn_call(q, k, v, tq=tq, tk=tk, q_pos0=past_len, sk_valid=sk)

    h, hn, gates = _out_call(x, s5_tb.reshape(seq, -1), attn, w, tm=tm)
    t = bsz * seq
    y = _moe_call(hn.reshape(t, dm), gates.reshape(t, LANES), h.reshape(t, dm), w, tm=_tile(t, 1024))
    return y.reshape(bsz, seq, dm), new_ckv, new_krope, h_re, h_im


def kernel(x_prompt, x_sample, cache_ckv, cache_krope, state_s5_re, state_s5_im, norm_mix_g, w_in, s5_a_re, s5_a_im, s5_log_dt, s5_b_re, s5_b_im, s5_c_re, s5_c_im, s5_d, s5_w_glu, s5_b_glu, q_lora_norm_g, w_uq, kv_lora_norm_g, w_uk, w_uv, q_nope_norm_g, q_rope_norm_g, k_nope_norm_g, k_rope_norm_g, out_norm_s5_g, out_norm_attn_g, w_out, norm_ffn_g, w_router_group, b_router_group, w_router_expert, b_router_expert, w_e_gate, w_e_up, w_e_down):
    params = dict(
        norm_mix_g=norm_mix_g, w_in=w_in, s5_a_re=s5_a_re, s5_a_im=s5_a_im, s5_log_dt=s5_log_dt,
        s5_b_re=s5_b_re, s5_b_im=s5_b_im, s5_c_re=s5_c_re, s5_c_im=s5_c_im, s5_d=s5_d,
        s5_w_glu=s5_w_glu, s5_b_glu=s5_b_glu, q_lora_norm_g=q_lora_norm_g, w_uq=w_uq,
        kv_lora_norm_g=kv_lora_norm_g, w_uk=w_uk, w_uv=w_uv, q_nope_norm_g=q_nope_norm_g,
        q_rope_norm_g=q_rope_norm_g, k_nope_norm_g=k_nope_norm_g, k_rope_norm_g=k_rope_norm_g,
        out_norm_s5_g=out_norm_s5_g, out_norm_attn_g=out_norm_attn_g, w_out=w_out, norm_ffn_g=norm_ffn_g,
        w_router_group=w_router_group, b_router_group=b_router_group,
        w_router_expert=w_router_expert, b_router_expert=b_router_expert,
        w_e_gate=w_e_gate, w_e_up=w_e_up, w_e_down=w_e_down)
    depth = w_in.shape[0]
    y_p, y_s = x_prompt, x_sample
    outs = [[] for _ in range(8)]
    for l in range(depth):
        w = _prepare_weights({k: a[l] for k, a in params.items()})
        y_p, c1, k1, r1, i1 = _trunk_layer(y_p, None, None, None, None, w)
        y_s, c2, k2, r2, i2 = _trunk_layer(y_s, cache_ckv[l], cache_krope[l], state_s5_re[l], state_s5_im[l], w)
        for lst, a in zip(outs, (c1, k1, r1, i1, c2, k2, r2, i2)):
            lst.append(a)
    return (y_p, y_s) + tuple(jnp.stack(lst) for lst in outs)
```

```python
import functools
import math

import jax
import jax.numpy as jnp
from jax import lax
from jax.experimental import pallas as pl
from jax.experimental.pallas import tpu as pltpu

F32 = jnp.float32
BF16 = jnp.bfloat16

CHUNK = 64
S5_GROUP = 16
S5_STATE = 64
N_HEADS = 8
QK_NOPE = 64
QK_ROPE = 32
QK_DIM = QK_NOPE + QK_ROPE
V_DIM = 64
ROPE_THETA = 10000.0
N_EXPERT_GROUPS = 4
EXPERTS_PER_GROUP = 8
N_EXPERTS = N_EXPERT_GROUPS * EXPERTS_PER_GROUP
EPS = 1e-6
NEG_INF = -1e30

LANES = 128
SUBLANES = 8
HEAD_PAD = LANES
ATTN_TQ = 512
ATTN_TK = 512
KV_TILE = 512
ATTN_COLS = 256
VT_ROWS = 80
VMEM_LIMIT = 48 * 1024 * 1024


def _tile(n, cap):
    for t in range(min(cap, n), 15, -1):
        if n % t == 0 and t % 16 == 0:
            return t
    return n


def _rms(x, g):
    return x * lax.rsqrt(jnp.mean(x * x, axis=-1, keepdims=True) + EPS) * g


def _dot(a, b):
    return jnp.dot(a, b, preferred_element_type=F32)


def _lane_iota(shape):
    return lax.broadcasted_iota(jnp.int32, shape, len(shape) - 1)


def _proj_kernel(x_ref, gmix_ref, win_ref, gq_ref, wuq_ref, bdq_ref, qgain_ref, gkv_ref, gkr_ref, tab_ref,
                 u_ref, q_ref, ckv_ref, kr_ref, *, s5_width, q_lora, kv_lora):
    x = x_ref[0]
    xn = _rms(x, gmix_ref[...])
    proj = _dot(xn.astype(BF16), win_ref[...])
    u_ref[...] = proj[:, :s5_width].astype(BF16)

    o = s5_width
    cqn = _rms(proj[:, o:o + q_lora], gq_ref[...])
    o += q_lora
    ckv_ref[0] = _rms(proj[:, o:o + kv_lora], gkv_ref[...])
    o += kv_lora

    tab = tab_ref[...]
    lane = _lane_iota((x.shape[0], LANES))

    kr = proj[:, o:o + LANES]
    ms = jnp.sum(jnp.where(lane < QK_ROPE, kr * kr, 0.0), axis=-1, keepdims=True) * (1.0 / QK_ROPE)
    t = kr * lax.rsqrt(ms + EPS) * gkr_ref[...] * tab[:, LANES:]
    kro = t + pltpu.roll(t, LANES - QK_ROPE, axis=1)
    kr_ref[0] = kro[:, :QK_ROPE]

    qa = _dot(cqn.astype(BF16), wuq_ref[...])
    qsq = (qa * qa).astype(BF16)
    pair = 2 * HEAD_PAD
    ms = jnp.concatenate([_dot(qsq[:, c * pair:(c + 1) * pair], bdq_ref[...])
                          for c in range(N_HEADS // 2)], axis=-1)
    qn = qa * lax.rsqrt(ms + EPS) * qgain_ref[...]
    is_rope = (lane >= QK_NOPE) & (lane < QK_DIM)
    for h in range(N_HEADS):
        th = qn[:, h * HEAD_PAD:(h + 1) * HEAD_PAD] * tab[:, :LANES]
        rolled = pltpu.roll(th, LANES - QK_ROPE, axis=1)
        oh = jnp.where(lane < QK_DIM, th + jnp.where(is_rope, rolled, 0.0), 0.0)
        q_ref[0, h] = oh.astype(BF16)


def _proj_call(x, tab, w, *, tm):
    bsz, seq, dm = x.shape
    s5_width, q_lora, kv_lora = w['s5_width'], w['q_lora'], w['kv_lora']
    in_cols = w['w_in'].shape[1]
    hp = N_HEADS * HEAD_PAD
    full = lambda shape: pl.BlockSpec(shape, lambda b, i: (0,) * len(shape))
    kern = functools.partial(_proj_kernel, s5_width=s5_width, q_lora=q_lora, kv_lora=kv_lora)
    return pl.pallas_call(
        kern,
        grid=(bsz, seq // tm),
        in_specs=[
            pl.BlockSpec((1, tm, dm), lambda b, i: (b, i, 0)),
            full((1, dm)), full((dm, in_cols)), full((1, q_lora)), full((q_lora, hp)),
            full((2 * HEAD_PAD, 2 * HEAD_PAD)), full((1, hp)), full((1, kv_lora)), full((1, LANES)),
            pl.BlockSpec((tm, 2 * LANES), lambda b, i: (i, 0)),
        ],
        out_specs=[
            pl.BlockSpec((tm, s5_width), lambda b, i: (i, b)),
            pl.BlockSpec((1, N_HEADS, tm, HEAD_PAD), lambda b, i: (b, 0, i, 0)),
            pl.BlockSpec((1, tm, kv_lora), lambda b, i: (b, i, 0)),
            pl.BlockSpec((1, tm, QK_ROPE), lambda b, i: (b, i, 0)),
        ],
        out_shape=[
            jax.ShapeDtypeStruct((seq, bsz * s5_width), BF16),
            jax.ShapeDtypeStruct((bsz, N_HEADS, seq, HEAD_PAD), BF16),
            jax.ShapeDtypeStruct((bsz, seq, kv_lora), F32),
            jax.ShapeDtypeStruct((bsz, seq, QK_ROPE), F32),
        ],
        compiler_params=pltpu.CompilerParams(
            dimension_semantics=("parallel", "parallel"), vmem_limit_bytes=VMEM_LIMIT),
    )(x, w['g_mix'], w['w_in'], w['g_q'], w['w_uq'], w['bd_q'], w['q_gain'], w['g_kv'], w['g_kr'], tab)


def _kv_kernel(ckv_ref, krp_ref, wuk_ref, wuv_ref, bdk_ref, kg_ref, k_ref, v_ref):
    c = ckv_ref[0].astype(BF16)
    ka = _dot(c, wuk_ref[...])
    va = _dot(c, wuv_ref[...])
    ksq = (ka * ka).astype(BF16)
    pair = 2 * HEAD_PAD
    ms = jnp.concatenate([_dot(ksq[:, c2 * pair:(c2 + 1) * pair], bdk_ref[...])
                          for c2 in range(N_HEADS // 2)], axis=-1)
    kn = ka * lax.rsqrt(ms + EPS) * kg_ref[...]
    krp = krp_ref[0]
    lane = _lane_iota(krp.shape)
    ones_col = jnp.where(lane == V_DIM, 1.0, 0.0)
    for h in range(N_HEADS):
        k_ref[0, h] = (kn[:, h * HEAD_PAD:(h + 1) * HEAD_PAD] + krp).astype(BF16)
        vt = (va[:, h * HEAD_PAD:(h + 1) * HEAD_PAD] + ones_col).T
        v_ref[0, h] = vt[:VT_ROWS].astype(BF16)


def _kv_call(ckv, krp, w, *, tm):
    bsz, sk, kv_lora = ckv.shape
    hp = N_HEADS * HEAD_PAD
    full = lambda shape: pl.BlockSpec(shape, lambda b, i: (0,) * len(shape))
    return pl.pallas_call(
        _kv_kernel,
        grid=(bsz, sk // tm),
        in_specs=[
            pl.BlockSpec((1, tm, kv_lora), lambda b, i: (b, i, 0)),
            pl.BlockSpec((1, tm, LANES), lambda b, i: (b, i, 0)),
            full((kv_lora, hp)), full((kv_lora, hp)), full((2 * HEAD_PAD, 2 * HEAD_PAD)), full((1, hp)),
        ],
        out_specs=[
            pl.BlockSpec((1, N_HEADS, tm, HEAD_PAD), lambda b, i: (b, 0, i, 0)),
            pl.BlockSpec((1, N_HEADS, VT_ROWS, tm), lambda b, i: (b, 0, 0, i)),
        ],
        out_shape=[
            jax.ShapeDtypeStruct((bsz, N_HEADS, sk, HEAD_PAD), BF16),
            jax.ShapeDtypeStruct((bsz, N_HEADS, VT_ROWS, sk), BF16),
        ],
        compiler_params=pltpu.CompilerParams(
            dimension_semantics=("parallel", "parallel"), vmem_limit_bytes=VMEM_LIMIT),
    )(ckv, krp, w['w_uk'], w['w_uv'], w['bd_k'], w['k_gain'])


def _attn_kernel(q_ref, k_ref, vt_ref, o_ref, *, tq, tk, q_pos0, sk_valid):
    sq = q_ref.shape[2]
    nq = sq // tq

    n_heads = q_ref.shape[1]

    cw = min(tq, ATTN_COLS)
    streams = [(hh, c) for hh in range(n_heads) for c in range(tq // cw)]

    def kv_step(qs, p0, j, carry, masked):
        k0 = pl.multiple_of(j * tk, tk)
        kbs = [k_ref[0, hh, pl.ds(k0, tk), :] for hh in range(n_heads)]
        ss = [lax.dot_general(kbs[hh], qs[i], (((1,), (1,)), ((), ())), preferred_element_type=F32)
              for i, (hh, c) in enumerate(streams)]
        if masked:
            k_pos = k0 + lax.broadcasted_iota(jnp.int32, (tk, 1), 0)
        new = []
        for i, (hh, c) in enumerate(streams):
            m, acc = carry[i]
            s = ss[i]
            if masked:
                q_chunk = (p0 + c * cw + lax.broadcasted_iota(jnp.int32, (1, cw), 1)) // CHUNK
                s = jnp.where((k_pos // CHUNK <= q_chunk) & (k_pos < sk_valid), s, NEG_INF)
            m_new = jnp.maximum(m, jnp.max(s, axis=0, keepdims=True))
            alpha = jnp.exp2(m - m_new)
            p = jnp.exp2(s - m_new)
            acc = alpha * acc + _dot(vt_ref[0, hh, :, pl.ds(k0, tk)], p.astype(BF16))
            new.append((m_new, acc))
        return tuple(new)

    def q_body(qi, carry):
        q0 = pl.multiple_of(qi * tq, tq)
        qs = [q_ref[0, hh, pl.ds(q0 + c * cw, cw), :] for hh, c in streams]
        p0 = q_pos0 + q0
        n_open = jnp.minimum((p0 // CHUNK + 1) * CHUNK, sk_valid) // tk
        v1 = jnp.minimum(((p0 + tq - 1) // CHUNK + 1) * CHUNK, sk_valid)
        n_all = (v1 + tk - 1) // tk
        init = tuple((jnp.full((1, cw), NEG_INF, F32), jnp.zeros((VT_ROWS, cw), F32)) for _ in streams)
        st = lax.fori_loop(0, n_open, lambda j, c: kv_step(qs, p0, j, c, False), init)
        st = lax.fori_loop(n_open, n_all, lambda j, c: kv_step(qs, p0, j, c, True), st)
        for c in range(tq // cw):
            ot = jnp.concatenate([st[i][1][:V_DIM] / st[i][1][V_DIM:V_DIM + 1]
                                  for i, (hh, c2) in enumerate(streams) if c2 == c], axis=0)
            o_ref[0, pl.ds(q0 + c * cw, cw), :] = ot.T.astype(o_ref.dtype)
        return carry

    lax.fori_loop(0, nq, q_body, 0)


def _attn_call(q, k, vt, *, tq, tk, q_pos0, sk_valid):
    bsz, nh, sq, _ = q.shape
    sk = k.shape[2]
    assert sq % tq == 0 and sk % tk == 0 and tq % LANES == 0 and tk % LANES == 0
    kern = functools.partial(_attn_kernel, tq=tq, tk=tk, q_pos0=q_pos0, sk_valid=sk_valid)
    return pl.pallas_call(
        kern,
        grid=(bsz, nh // 2),
        in_specs=[
            pl.BlockSpec((1, 2, sq, HEAD_PAD), lambda b, h: (b, h, 0, 0)),
            pl.BlockSpec((1, 2, sk, HEAD_PAD), lambda b, h: (b, h, 0, 0)),
            pl.BlockSpec((1, 2, VT_ROWS, sk), lambda b, h: (b, h, 0, 0)),
        ],
        out_specs=pl.BlockSpec((1, sq, 2 * V_DIM), lambda b, h: (b, 0, h)),
        out_shape=jax.ShapeDtypeStruct((bsz, sq, nh * V_DIM), BF16),
        compiler_params=pltpu.CompilerParams(
            dimension_semantics=("parallel", "parallel"), vmem_limit_bytes=VMEM_LIMIT),
    )(q, k, vt)


def _s5_kernel(u_ref, h0_ref, lre_ref, lim_ref, bbd_ref, cbd_ref, d_ref, wglu_ref, bglu_ref,
               y_ref, hout_ref, hbuf, hstate, *, ts, n_half, col_chunk):
    i = pl.program_id(0)
    half = n_half
    width = hbuf.shape[1]
    uw = u_ref.shape[1] // 2

    @pl.when(i == 0)
    def _():
        hstate[...] = h0_ref[...]

    u = u_ref[...]
    for c in range(2):
        hbuf[:, c * 2 * half:(c + 1) * 2 * half] = _dot(u[:, c * uw:(c + 1) * uw], bbd_ref[c])

    for c in range(2):
        for cc in range(half // col_chunk):
            re0 = c * 2 * half + cc * col_chunk
            im0 = re0 + half
            l0 = c * half + cc * col_chunk
            ar = lre_ref[:, l0:l0 + col_chunk]
            ai = lim_ref[:, l0:l0 + col_chunk]

            def step(s, carry, re0=re0, im0=im0, ar=ar, ai=ai):
                hr, hi = carry
                r0 = pl.multiple_of(s * SUBLANES, SUBLANES)
                nr = ar * hr - ai * hi + hbuf[pl.ds(r0, SUBLANES), re0:re0 + col_chunk]
                ni = ar * hi + ai * hr + hbuf[pl.ds(r0, SUBLANES), im0:im0 + col_chunk]
                hbuf[pl.ds(r0, SUBLANES), re0:re0 + col_chunk] = nr
                hbuf[pl.ds(r0, SUBLANES), im0:im0 + col_chunk] = ni
                return nr, ni

            hr, hi = lax.fori_loop(
                0, ts, step,
                (hstate[:, re0:re0 + col_chunk], hstate[:, im0:im0 + col_chunk]), unroll=8)
            hstate[:, re0:re0 + col_chunk] = hr
            hstate[:, im0:im0 + col_chunk] = hi

    hout_ref[...] = hstate[...]

    y = jnp.concatenate(
        [_dot(hbuf[:, c * 2 * half:(c + 1) * 2 * half].astype(BF16), cbd_ref[c]) for c in range(2)], axis=-1)
    y = y + d_ref[...] * u.astype(F32)
    z = jax.nn.gelu(y)
    gate = _dot(z.astype(BF16), wglu_ref[...]) + bglu_ref[...]
    y_ref[...] = (z * jax.nn.sigmoid(gate)).astype(y_ref.dtype)


def _s5_call(u_tb, h0, w, *, ts):
    rows, s5_width = u_tb.shape
    n_state = w['lam_re'].shape[1]
    half = n_state // 2
    width = 2 * n_state
    n_steps = rows // (ts * SUBLANES)
    full = lambda shape: pl.BlockSpec(shape, lambda i: (0,) * len(shape))
    kern = functools.partial(_s5_kernel, ts=ts, n_half=half, col_chunk=512)
    return pl.pallas_call(
        kern,
        grid=(n_steps,),
        in_specs=[
            pl.BlockSpec((ts * SUBLANES, s5_width), lambda i: (i, 0)),
            full((SUBLANES, width)), full((SUBLANES, n_state)), full((SUBLANES, n_state)),
            full((2, s5_width // 2, 2 * half)), full((2, 2 * half, s5_width // 2)),
            full((1, s5_width)), full((s5_width, s5_width)), full((1, s5_width)),
        ],
        out_specs=[
            pl.BlockSpec((ts * SUBLANES, s5_width), lambda i: (i, 0)),
            full((SUBLANES, width)),
        ],
        out_shape=[
            jax.ShapeDtypeStruct((rows, s5_width), BF16),
            jax.ShapeDtypeStruct((SUBLANES, width), F32),
        ],
        scratch_shapes=[
            pltpu.VMEM((ts * SUBLANES, width), F32),
            pltpu.VMEM((SUBLANES, width), F32),
        ],
        compiler_params=pltpu.CompilerParams(
            dimension_semantics=("arbitrary",), vmem_limit_bytes=VMEM_LIMIT),
    )(u_tb, h0, w['lam_re'], w['lam_im'], w['b_bd'], w['c_bd'], w['s5_d'], w['w_glu'], w['b_glu'])


def _out_kernel(x_ref, s5_ref, at_ref, gs5_ref, gat_ref, wout_ref, gffn_ref, wr_ref, br_ref,
                h_ref, hn_ref, gate_ref):
    s5n = _rms(s5_ref[...].astype(F32), gs5_ref[...])
    atn = _rms(at_ref[0].astype(F32), gat_ref[...])
    merged = jnp.concatenate([s5n, atn], axis=-1).astype(BF16)
    h = x_ref[0] + _dot(merged, wout_ref[...])
    h_ref[0] = h
    hn = _rms(h, gffn_ref[...]).astype(BF16)
    hn_ref[0] = hn

    logits = _dot(hn, wr_ref[...]) + br_ref[...]
    lane_i = _lane_iota(logits.shape)
    lane = lane_i.astype(F32)
    lane_group = (lane_i // EXPERTS_PER_GROUP).astype(F32)
    big = float(LANES)
    is_g = (lane_i >= N_EXPERTS) & (lane_i < N_EXPERTS + N_EXPERT_GROUPS)
    gl = jnp.where(is_g, logits, NEG_INF)
    gmax = jnp.max(gl, axis=-1, keepdims=True)
    g_idx = jnp.min(jnp.where(gl == gmax, lane, big), axis=-1, keepdims=True) - N_EXPERTS
    g_top = 1.0 / jnp.sum(jnp.where(is_g, jnp.exp(gl - gmax), 0.0), axis=-1, keepdims=True)
    in_group = (lane_i < N_EXPERTS) & (lane_group == g_idx)
    el = jnp.where(in_group, logits, NEG_INF)
    v1 = jnp.max(el, axis=-1, keepdims=True)
    i1 = jnp.min(jnp.where(el == v1, lane, big), axis=-1, keepdims=True)
    el2 = jnp.where(lane == i1, NEG_INF, el)
    v2 = jnp.max(el2, axis=-1, keepdims=True)
    i2 = jnp.min(jnp.where(el2 == v2, lane, big), axis=-1, keepdims=True)
    e21 = jnp.exp(v2 - v1)
    w1 = g_top / (1.0 + e21)
    w2 = g_top * e21 / (1.0 + e21)
    gate_ref[0] = jnp.where(lane == i1, w1, 0.0) + jnp.where(lane == i2, w2, 0.0)


def _out_call(x, s5_tb, attn, w, *, tm):
    bsz, seq, dm = x.shape
    s5_width = s5_tb.shape[1] // bsz
    aw = attn.shape[2]
    full = lambda shape: pl.BlockSpec(shape, lambda b, i: (0,) * len(shape))
    tok = lambda width: pl.BlockSpec((1, tm, width), lambda b, i: (b, i, 0))
    return pl.pallas_call(
        _out_kernel,
        grid=(bsz, seq // tm),
        in_specs=[
            tok(dm),
            pl.BlockSpec((tm, s5_width), lambda b, i: (i, b)),
            tok(aw),
            full((1, s5_width)), full((1, aw)), full((s5_width + aw, dm)), full((1, dm)),
            full((dm, LANES)), full((1, LANES)),
        ],
        out_specs=[tok(dm), tok(dm), tok(LANES)],
        out_shape=[
            jax.ShapeDtypeStruct((bsz, seq, dm), F32),
            jax.ShapeDtypeStruct((bsz, seq, dm), BF16),
            jax.ShapeDtypeStruct((bsz, seq, LANES), F32),
        ],
        compiler_params=pltpu.CompilerParams(
            dimension_semantics=("parallel", "parallel"), vmem_limit_bytes=VMEM_LIMIT),
    )(x, s5_tb, attn, w['g_s5'], w['g_attn'], w['w_out'], w['g_ffn'], w['w_router'], w['b_router'])


def _moe_kernel(hn_ref, gate_ref, h_ref, wgu_ref, wd_ref, o_ref, *, d_expert):
    e = pl.program_id(1)

    @pl.when(e == 0)
    def _():
        o_ref[...] = h_ref[...]

    gu = _dot(hn_ref[...], wgu_ref[0])
    hdn = jax.nn.silu(gu[:, :d_expert]) * gu[:, d_expert:]
    y = _dot(hdn.astype(BF16), wd_ref[0])
    gate = gate_ref[...]
    g = jnp.sum(jnp.where(_lane_iota(gate.shape) == e, gate, 0.0), axis=-1, keepdims=True)
    o_ref[...] += g * y


def _moe_call(hn, gates, h, w, *, tm):
    t, dm = hn.shape
    d_expert = w['w_d'].shape[1]
    kern = functools.partial(_moe_kernel, d_expert=d_expert)
    return pl.pallas_call(
        kern,
        grid=(t // tm, N_EXPERTS),
        in_specs=[
            pl.BlockSpec((tm, dm), lambda i, e: (i, 0)),
            pl.BlockSpec((tm, LANES), lambda i, e: (i, 0)),
            pl.BlockSpec((tm, dm), lambda i, e: (i, 0)),
            pl.BlockSpec((1, dm, 2 * d_expert), lambda i, e: (e, 0, 0)),
            pl.BlockSpec((1, d_expert, dm), lambda i, e: (e, 0, 0)),
        ],
        out_specs=pl.BlockSpec((tm, dm), lambda i, e: (i, 0)),
        out_shape=jax.ShapeDtypeStruct((t, dm), F32),
        compiler_params=pltpu.CompilerParams(
            dimension_semantics=("parallel", "arbitrary"), vmem_limit_bytes=VMEM_LIMIT),
    )(hn, gates, h, w['w_gu'], w['w_d'])


def _prepare_weights(p):
    dm, in_cols = p['w_in'].shape
    s5_width = p['s5_d'].shape[0]
    q_lora = p['q_lora_norm_g'].shape[0]
    kv_lora = p['kv_lora_norm_g'].shape[0]
    half_r = QK_ROPE // 2
    w = {'s5_width': s5_width, 'q_lora': q_lora, 'kv_lora': kv_lora}

    def partner(a):
        return jnp.concatenate([-a[..., half_r:], a[..., :half_r]], axis=-1)

    def swap(a):
        return jnp.concatenate([a[..., half_r:], a[..., :half_r]], axis=-1)

    w_kr = p['w_in'][:, in_cols - QK_ROPE:]
    w['w_in'] = jnp.concatenate(
        [p['w_in'], partner(w_kr), jnp.zeros((dm, LANES - 2 * QK_ROPE), F32)], axis=-1).astype(BF16)
    w['g_mix'] = p['norm_mix_g'][None]
    w['g_q'] = p['q_lora_norm_g'][None]
    w['g_kv'] = p['kv_lora_norm_g'][None]
    gkr = p['k_rope_norm_g']
    w['g_kr'] = jnp.concatenate([gkr, swap(gkr), jnp.zeros((LANES - 2 * QK_ROPE,), F32)])[None]

    wq = p['w_uq']
    wq_r = wq[..., QK_NOPE:]
    w['w_uq'] = jnp.concatenate([wq, partner(wq_r)], axis=-1).reshape(q_lora, N_HEADS * HEAD_PAD).astype(BF16)
    gr = p['q_rope_norm_g']
    w['q_gain'] = jnp.tile(jnp.concatenate([p['q_nope_norm_g'], gr, swap(gr)]), N_HEADS)[None]
    idx = jnp.arange(HEAD_PAD)
    row_nope = (idx < QK_NOPE)[:, None]
    row_rope = ((idx >= QK_NOPE) & (idx < QK_DIM))[:, None]
    col_nope = (idx < QK_NOPE)[None, :]
    m_q = jnp.where(row_nope & col_nope, 1.0 / QK_NOPE, 0.0) + jnp.where(row_rope & ~col_nope, 1.0 / QK_ROPE, 0.0)
    m_k = jnp.where(row_nope, 1.0 / QK_NOPE, 0.0) * jnp.ones((1, HEAD_PAD), F32)
    eye2 = jnp.eye(2, dtype=F32)
    w['bd_q'] = jnp.kron(eye2, m_q).astype(BF16)
    w['bd_k'] = jnp.kron(eye2, m_k).astype(BF16)

    pad_h = lambda a: jnp.pad(a, ((0, 0), (0, 0), (0, HEAD_PAD - a.shape[-1])))
    w['w_uk'] = pad_h(p['w_uk']).reshape(kv_lora, N_HEADS * HEAD_PAD).astype(BF16)
    w['w_uv'] = pad_h(p['w_uv']).reshape(kv_lora, N_HEADS * HEAD_PAD).astype(BF16)
    w['k_gain'] = jnp.tile(jnp.pad(p['k_nope_norm_g'], (0, HEAD_PAD - QK_NOPE)), N_HEADS)[None]

    dt = jnp.exp(p['s5_log_dt'])[:, None]
    ar, ai = p['s5_a_re'], p['s5_a_im']
    mag = jnp.exp(dt * ar)
    abar_re = mag * jnp.cos(dt * ai)
    abar_im = mag * jnp.sin(dt * ai)
    den = ar * ar + ai * ai
    num_re = abar_re - 1.0
    coef_re = (num_re * ar + abar_im * ai) / den
    coef_im = (abar_im * ar - num_re * ai) / den
    br, bi = p['s5_b_re'], p['s5_b_im']
    bbar_re = coef_re[..., None] * br - coef_im[..., None] * bi
    bbar_im = coef_re[..., None] * bi + coef_im[..., None] * br
    g = ar.shape[0]
    n_state = g * S5_STATE
    half = n_state // 2
    eye_g = jnp.eye(g, dtype=F32)
    b_re = jnp.einsum('gni,gh->gihn', bbar_re, eye_g).reshape(s5_width, n_state)
    b_im = jnp.einsum('gni,gh->gihn', bbar_im, eye_g).reshape(s5_width, n_state)
    c_re = jnp.einsum('gon,gh->gnho', p['s5_c_re'], eye_g).reshape(n_state, s5_width)
    c_im = jnp.einsum('gon,gh->gnho', p['s5_c_im'], eye_g).reshape(n_state, s5_width)
    uw = s5_width // 2
    w['b_bd'] = jnp.stack([
        jnp.concatenate([b_re[c * uw:(c + 1) * uw, c * half:(c + 1) * half],
                         b_im[c * uw:(c + 1) * uw, c * half:(c + 1) * half]], axis=1) for c in range(2)]).astype(BF16)
    w['c_bd'] = jnp.stack([
        jnp.concatenate([c_re[c * half:(c + 1) * half, c * uw:(c + 1) * uw],
                         -c_im[c * half:(c + 1) * half, c * uw:(c + 1) * uw]], axis=0) for c in range(2)]).astype(BF16)
    w['lam_re'] = jnp.broadcast_to(abar_re.reshape(1, n_state), (SUBLANES, n_state))
    w['lam_im'] = jnp.broadcast_to(abar_im.reshape(1, n_state), (SUBLANES, n_state))
    w['s5_d'] = p['s5_d'][None]
    w['w_glu'] = p['s5_w_glu'].astype(BF16)
    w['b_glu'] = p['s5_b_glu'][None]

    w['g_s5'] = p['out_norm_s5_g'][None]
    w['g_attn'] = p['out_norm_attn_g'][None]
    w['w_out'] = p['w_out'].astype(BF16)
    w['g_ffn'] = p['norm_ffn_g'][None]
    n_r = N_EXPERTS + N_EXPERT_GROUPS
    w['w_router'] = jnp.pad(jnp.concatenate([p['w_router_expert'], p['w_router_group']], axis=1),
                            ((0, 0), (0, LANES - n_r))).astype(BF16)
    w['b_router'] = jnp.pad(jnp.concatenate([p['b_router_expert'], p['b_router_group']]), (0, LANES - n_r))[None]
    w['w_gu'] = jnp.concatenate([p['w_e_gate'], p['w_e_up']], axis=-1).astype(BF16)
    w['w_d'] = p['w_e_down'].astype(BF16)
    return w


def _rope_table(pos):
    half = QK_ROPE // 2
    inv = ROPE_THETA ** (-jnp.arange(half, dtype=F32) / half)
    ang = pos.astype(F32)[:, None] * inv[None, :]
    cc = jnp.tile(jnp.cos(ang), (1, 2))
    ss = jnp.tile(jnp.sin(ang), (1, 2))
    n = pos.shape[0]
    scale = QK_DIM ** -0.5 * math.log2(math.e)
    tab_q = scale * jnp.concatenate([jnp.ones((n, QK_NOPE), F32), cc, ss], axis=-1)
    tab_k = jnp.concatenate([cc, ss, jnp.zeros((n, LANES - 2 * QK_ROPE), F32)], axis=-1)
    return jnp.concatenate([tab_q, tab_k], axis=-1)


def _s5_state_in(re, im):
    b = re.shape[0]
    re = re.reshape(b, -1)
    im = im.reshape(b, -1)
    half = re.shape[1] // 2
    return jnp.concatenate([re[:, :half], im[:, :half], re[:, half:], im[:, half:]], axis=1)


def _s5_state_out(h, g):
    b = h.shape[0]
    half = h.shape[1] // 4
    re = jnp.concatenate([h[:, :half], h[:, 2 * half:3 * half]], axis=1).reshape(b, g, S5_STATE)
    im = jnp.concatenate([h[:, half:2 * half], h[:, 3 * half:]], axis=1).reshape(b, g, S5_STATE)
    return re, im


def _trunk_layer(x, past_ckv, past_krope, h0_re, h0_im, w):
    bsz, seq, dm = x.shape
    assert bsz == SUBLANES, "the S5 scan keeps the batch on the sublanes"
    past_len = 0 if past_ckv is None else past_ckv.shape[1]
    g = w['lam_re'].shape[1] // S5_STATE
    tm = _tile(seq, 512)

    tab = _rope_table(past_len + jnp.arange(seq, dtype=jnp.int32))
    u_tb, q, new_ckv, new_krope = _proj_call(x, tab, w, tm=tm)

    if h0_re is None:
        h0 = jnp.zeros((bsz, 2 * g * S5_STATE), F32)
    else:
        h0 = _s5_state_in(h0_re.astype(F32), h0_im.astype(F32))
    s5_tb, h_last = _s5_call(u_tb.reshape(seq * bsz, -1), h0, w, ts=min(seq, 64))
    h_re, h_im = _s5_state_out(h_last, g)

    if past_ckv is None:
        all_ckv, all_krope = new_ckv, new_krope
    else:
        all_ckv = jnp.concatenate([past_ckv.astype(F32), new_ckv], axis=1)
        all_krope = jnp.concatenate([past_krope.astype(F32), new_krope], axis=1)
    sk = all_ckv.shape[1]
    sk_pad = -(-sk // LANES) * LANES
    all_ckv = jnp.pad(all_ckv, ((0, 0), (0, sk_pad - sk), (0, 0)))
    krp = jnp.pad(all_krope, ((0, 0), (0, sk_pad - sk), (QK_NOPE, LANES - QK_DIM)))
    whole = sk_pad % ATTN_TK != 0
    k, vt = _kv_call(all_ckv, krp, w, tm=sk_pad if whole else KV_TILE)
    sq_pad = -(-seq // LANES) * LANES
    q = jnp.pad(q, ((0, 0), (0, 0), (0, sq_pad - seq), (0, 0)))
    tq = ATTN_TQ if sq_pad % ATTN_TQ == 0 else LANES
    attn = _attn_call(q, k, vt, tq=tq, tk=sk_pad if whole else ATTN_TK, q_pos0=past_len, sk_valid=sk)[:, :seq]

    h, hn, gates = _out_call(x, s5_tb.reshape(seq, -1), attn, w, tm=tm)
    t = bsz * seq
    y = _moe_call(hn.reshape(t, dm), gates.reshape(t, LANES), h.reshape(t, dm), w, tm=_tile(t, 1024))
    return y.reshape(bsz, seq, dm), new_ckv, new_krope, h_re, h_im


def kernel(x_prompt, x_sample, cache_ckv, cache_krope, state_s5_re, state_s5_im, norm_mix_g, w_in, s5_a_re, s5_a_im, s5_log_dt, s5_b_re, s5_b_im, s5_c_re, s5_c_im, s5_d, s5_w_glu, s5_b_glu, q_lora_norm_g, w_uq, kv_lora_norm_g, w_uk, w_uv, q_nope_norm_g, q_rope_norm_g, k_nope_norm_g, k_rope_norm_g, out_norm_s5_g, out_norm_attn_g, w_out, norm_ffn_g, w_router_group, b_router_group, w_router_expert, b_router_expert, w_e_gate, w_e_up, w_e_down):
    params = dict(
        norm_mix_g=norm_mix_g, w_in=w_in, s5_a_re=s5_a_re, s5_a_im=s5_a_im, s5_log_dt=s5_log_dt,
        s5_b_re=s5_b_re, s5_b_im=s5_b_im, s5_c_re=s5_c_re, s5_c_im=s5_c_im, s5_d=s5_d,
        s5_w_glu=s5_w_glu, s5_b_glu=s5_b_glu, q_lora_norm_g=q_lora_norm_g, w_uq=w_uq,
        kv_lora_norm_g=kv_lora_norm_g, w_uk=w_uk, w_uv=w_uv, q_nope_norm_g=q_nope_norm_g,
        q_rope_norm_g=q_rope_norm_g, k_nope_norm_g=k_nope_norm_g, k_rope_norm_g=k_rope_norm_g,
        out_norm_s5_g=out_norm_s5_g, out_norm_attn_g=out_norm_attn_g, w_out=w_out, norm_ffn_g=norm_ffn_g,
        w_router_group=w_router_group, b_router_group=b_router_group,
        w_router_expert=w_router_expert, b_router_expert=b_router_expert,
        w_e_gate=w_e_gate, w_e_up=w_e_up, w_e_down=w_e_down)
    depth = w_in.shape[0]
    y_p, y_s = x_prompt, x_sample
    outs = [[] for _ in range(8)]
    for l in range(depth):
        w = _prepare_weights({k: a[l] for k, a in params.items()})
        y_p, c1, k1, r1, i1 = _trunk_layer(y_p, None, None, None, None, w)
        y_s, c2, k2, r2, i2 = _trunk_layer(y_s, cache_ckv[l], cache_krope[l], state_s5_re[l], state_s5_im[l], w)
        for lst, a in zip(outs, (c1, k1, r1, i1, c2, k2, r2, i2)):
            lst.append(a)
    return (y_p, y_s) + tuple(jnp.stack(lst) for lst in outs)
```

```python
import functools
import math

import jax
import jax.numpy as jnp
from jax import lax
from jax.experimental import pallas as pl
from jax.experimental.pallas import tpu as pltpu

F32 = jnp.float32
BF16 = jnp.bfloat16

CHUNK = 64
S5_GROUP = 16
S5_STATE = 64
N_HEADS = 8
QK_NOPE = 64
QK_ROPE = 32
QK_DIM = QK_NOPE + QK_ROPE
V_DIM = 64
ROPE_THETA = 10000.0
N_EXPERT_GROUPS = 4
EXPERTS_PER_GROUP = 8
N_EXPERTS = N_EXPERT_GROUPS * EXPERTS_PER_GROUP
EPS = 1e-6
NEG_INF = -1e30

LANES = 128
SUBLANES = 8
HEAD_PAD = LANES
ATTN_TQ = 512
ATTN_TK = 512
KV_TILE = 512
MOE_BLOCK = 2048
MOE_CHUNK = 256
ATTN_COLS = 256
VT_ROWS = 80
VMEM_LIMIT = 48 * 1024 * 1024


def _tile(n, cap):
    for t in range(min(cap, n), 15, -1):
        if n % t == 0 and t % 16 == 0:
            return t
    return n


def _rms(x, g):
    return x * lax.rsqrt(jnp.mean(x * x, axis=-1, keepdims=True) + EPS) * g


def _dot(a, b):
    return jnp.dot(a, b, preferred_element_type=F32)


def _lane_iota(shape):
    return lax.broadcasted_iota(jnp.int32, shape, len(shape) - 1)


def _proj_kernel(x_ref, gmix_ref, win_ref, gq_ref, wuq_ref, bdq_ref, qgain_ref, gkv_ref, gkr_ref, tab_ref,
                 u_ref, q_ref, ckv_ref, kr_ref, *, s5_width, q_lora, kv_lora):
    x = x_ref[0]
    xn = _rms(x, gmix_ref[...])
    proj = _dot(xn.astype(BF16), win_ref[...])
    u_ref[...] = proj[:, :s5_width].astype(BF16)

    o = s5_width
    cqn = _rms(proj[:, o:o + q_lora], gq_ref[...])
    o += q_lora
    ckv_ref[0] = _rms(proj[:, o:o + kv_lora], gkv_ref[...])
    o += kv_lora

    tab = tab_ref[...]
    lane = _lane_iota((x.shape[0], LANES))

    kr = proj[:, o:o + LANES]
    ms = jnp.sum(jnp.where(lane < QK_ROPE, kr * kr, 0.0), axis=-1, keepdims=True) * (1.0 / QK_ROPE)
    t = kr * lax.rsqrt(ms + EPS) * gkr_ref[...] * tab[:, LANES:]
    kro = t + pltpu.roll(t, LANES - QK_ROPE, axis=1)
    kr_ref[0] = kro[:, :QK_ROPE]

    qa = _dot(cqn.astype(BF16), wuq_ref[...])
    qsq = (qa * qa).astype(BF16)
    pair = 2 * HEAD_PAD
    ms = jnp.concatenate([_dot(qsq[:, c * pair:(c + 1) * pair], bdq_ref[...])
                          for c in range(N_HEADS // 2)], axis=-1)
    qn = qa * lax.rsqrt(ms + EPS) * qgain_ref[...]
    is_rope = (lane >= QK_NOPE) & (lane < QK_DIM)
    for h in range(N_HEADS):
        th = qn[:, h * HEAD_PAD:(h + 1) * HEAD_PAD] * tab[:, :LANES]
        rolled = pltpu.roll(th, LANES - QK_ROPE, axis=1)
        oh = jnp.where(lane < QK_DIM, th + jnp.where(is_rope, rolled, 0.0), 0.0)
        q_ref[0, h] = oh.astype(BF16)


def _proj_call(x, tab, w, *, tm):
    bsz, seq, dm = x.shape
    s5_width, q_lora, kv_lora = w['s5_width'], w['q_lora'], w['kv_lora']
    in_cols = w['w_in'].shape[1]
    hp = N_HEADS * HEAD_PAD
    full = lambda shape: pl.BlockSpec(shape, lambda b, i: (0,) * len(shape))
    kern = functools.partial(_proj_kernel, s5_width=s5_width, q_lora=q_lora, kv_lora=kv_lora)
    return pl.pallas_call(
        kern,
        grid=(bsz, seq // tm),
        in_specs=[
            pl.BlockSpec((1, tm, dm), lambda b, i: (b, i, 0)),
            full((1, dm)), full((dm, in_cols)), full((1, q_lora)), full((q_lora, hp)),
            full((2 * HEAD_PAD, 2 * HEAD_PAD)), full((1, hp)), full((1, kv_lora)), full((1, LANES)),
            pl.BlockSpec((tm, 2 * LANES), lambda b, i: (i, 0)),
        ],
        out_specs=[
            pl.BlockSpec((tm, s5_width), lambda b, i: (i, b)),
            pl.BlockSpec((1, N_HEADS, tm, HEAD_PAD), lambda b, i: (b, 0, i, 0)),
            pl.BlockSpec((1, tm, kv_lora), lambda b, i: (b, i, 0)),
            pl.BlockSpec((1, tm, QK_ROPE), lambda b, i: (b, i, 0)),
        ],
        out_shape=[
            jax.ShapeDtypeStruct((seq, bsz * s5_width), BF16),
            jax.ShapeDtypeStruct((bsz, N_HEADS, seq, HEAD_PAD), BF16),
            jax.ShapeDtypeStruct((bsz, seq, kv_lora), F32),
            jax.ShapeDtypeStruct((bsz, seq, QK_ROPE), F32),
        ],
        compiler_params=pltpu.CompilerParams(
            dimension_semantics=("parallel", "parallel"), vmem_limit_bytes=VMEM_LIMIT),
    )(x, w['g_mix'], w['w_in'], w['g_q'], w['w_uq'], w['bd_q'], w['q_gain'], w['g_kv'], w['g_kr'], tab)


def _kv_kernel(ckv_ref, krp_ref, wuk_ref, wuv_ref, bdk_ref, kg_ref, k_ref, v_ref):
    c = ckv_ref[0].astype(BF16)
    ka = _dot(c, wuk_ref[...])
    va = _dot(c, wuv_ref[...])
    ksq = (ka * ka).astype(BF16)
    pair = 2 * HEAD_PAD
    ms = jnp.concatenate([_dot(ksq[:, c2 * pair:(c2 + 1) * pair], bdk_ref[...])
                          for c2 in range(N_HEADS // 2)], axis=-1)
    kn = ka * lax.rsqrt(ms + EPS) * kg_ref[...]
    krp = krp_ref[0]
    lane = _lane_iota(krp.shape)
    ones_col = jnp.where(lane == V_DIM, 1.0, 0.0)
    for h in range(N_HEADS):
        k_ref[0, h] = (kn[:, h * HEAD_PAD:(h + 1) * HEAD_PAD] + krp).astype(BF16)
        vt = (va[:, h * HEAD_PAD:(h + 1) * HEAD_PAD] + ones_col).T
        v_ref[0, h] = vt[:VT_ROWS].astype(BF16)


def _kv_call(ckv, krp, w, *, tm):
    bsz, sk, kv_lora = ckv.shape
    hp = N_HEADS * HEAD_PAD
    full = lambda shape: pl.BlockSpec(shape, lambda b, i: (0,) * len(shape))
    return pl.pallas_call(
        _kv_kernel,
        grid=(bsz, sk // tm),
        in_specs=[
            pl.BlockSpec((1, tm, kv_lora), lambda b, i: (b, i, 0)),
            pl.BlockSpec((1, tm, LANES), lambda b, i: (b, i, 0)),
            full((kv_lora, hp)), full((kv_lora, hp)), full((2 * HEAD_PAD, 2 * HEAD_PAD)), full((1, hp)),
        ],
        out_specs=[
            pl.BlockSpec((1, N_HEADS, tm, HEAD_PAD), lambda b, i: (b, 0, i, 0)),
            pl.BlockSpec((1, N_HEADS, VT_ROWS, tm), lambda b, i: (b, 0, 0, i)),
        ],
        out_shape=[
            jax.ShapeDtypeStruct((bsz, N_HEADS, sk, HEAD_PAD), BF16),
            jax.ShapeDtypeStruct((bsz, N_HEADS, VT_ROWS, sk), BF16),
        ],
        compiler_params=pltpu.CompilerParams(
            dimension_semantics=("parallel", "parallel"), vmem_limit_bytes=VMEM_LIMIT),
    )(ckv, krp, w['w_uk'], w['w_uv'], w['bd_k'], w['k_gain'])


def _attn_kernel(q_ref, k_ref, vt_ref, o_ref, *, tq, tk, q_pos0, sk_valid):
    sq = q_ref.shape[2]
    nq = sq // tq

    n_heads = q_ref.shape[1]

    cw = min(tq, ATTN_COLS)
    streams = [(hh, c) for hh in range(n_heads) for c in range(tq // cw)]

    def kv_step(qs, p0, j, carry, masked):
        k0 = pl.multiple_of(j * tk, tk)
        kbs = [k_ref[0, hh, pl.ds(k0, tk), :] for hh in range(n_heads)]
        ss = [lax.dot_general(kbs[hh], qs[i], (((1,), (1,)), ((), ())), preferred_element_type=F32)
              for i, (hh, c) in enumerate(streams)]
        if masked:
            k_pos = k0 + lax.broadcasted_iota(jnp.int32, (tk, 1), 0)
        new = []
        for i, (hh, c) in enumerate(streams):
            m, acc = carry[i]
            s = ss[i]
            if masked:
                q_chunk = (p0 + c * cw + lax.broadcasted_iota(jnp.int32, (1, cw), 1)) // CHUNK
                s = jnp.where((k_pos // CHUNK <= q_chunk) & (k_pos < sk_valid), s, NEG_INF)
            m_new = jnp.maximum(m, jnp.max(s, axis=0, keepdims=True))
            alpha = jnp.exp2(m - m_new)
            p = jnp.exp2(s - m_new)
            acc = alpha * acc + _dot(vt_ref[0, hh, :, pl.ds(k0, tk)], p.astype(BF16))
            new.append((m_new, acc))
        return tuple(new)

    def q_body(qi, carry):
        q0 = pl.multiple_of(qi * tq, tq)
        qs = [q_ref[0, hh, pl.ds(q0 + c * cw, cw), :] for hh, c in streams]
        p0 = q_pos0 + q0
        n_open = jnp.minimum((p0 // CHUNK + 1) * CHUNK, sk_valid) // tk
        v1 = jnp.minimum(((p0 + tq - 1) // CHUNK + 1) * CHUNK, sk_valid)
        n_all = (v1 + tk - 1) // tk
        init = tuple((jnp.full((1, cw), NEG_INF, F32), jnp.zeros((VT_ROWS, cw), F32)) for _ in streams)
        st = lax.fori_loop(0, n_open, lambda j, c: kv_step(qs, p0, j, c, False), init)
        st = lax.fori_loop(n_open, n_all, lambda j, c: kv_step(qs, p0, j, c, True), st)
        for c in range(tq // cw):
            ot = jnp.concatenate([st[i][1][:V_DIM] / st[i][1][V_DIM:V_DIM + 1]
                                  for i, (hh, c2) in enumerate(streams) if c2 == c], axis=0)
            o_ref[0, pl.ds(q0 + c * cw, cw), :] = ot.T.astype(o_ref.dtype)
        return carry

    lax.fori_loop(0, nq, q_body, 0)


def _attn_call(q, k, vt, *, tq, tk, q_pos0, sk_valid):
    bsz, nh, sq, _ = q.shape
    sk = k.shape[2]
    assert sq % tq == 0 and sk % tk == 0 and tq % LANES == 0 and tk % LANES == 0
    kern = functools.partial(_attn_kernel, tq=tq, tk=tk, q_pos0=q_pos0, sk_valid=sk_valid)
    return pl.pallas_call(
        kern,
        grid=(bsz, nh // 2),
        in_specs=[
            pl.BlockSpec((1, 2, sq, HEAD_PAD), lambda b, h: (b, h, 0, 0)),
            pl.BlockSpec((1, 2, sk, HEAD_PAD), lambda b, h: (b, h, 0, 0)),
            pl.BlockSpec((1, 2, VT_ROWS, sk), lambda b, h: (b, h, 0, 0)),
        ],
        out_specs=pl.BlockSpec((1, sq, 2 * V_DIM), lambda b, h: (b, 0, h)),
        out_shape=jax.ShapeDtypeStruct((bsz, sq, nh * V_DIM), BF16),
        compiler_params=pltpu.CompilerParams(
            dimension_semantics=("parallel", "parallel"), vmem_limit_bytes=VMEM_LIMIT),
    )(q, k, vt)


def _s5_kernel(u_ref, h0_ref, lre_ref, lim_ref, bbd_ref, cbd_ref, d_ref, wglu_ref, bglu_ref,
               y_ref, hout_ref, hbuf, hstate, *, ts, n_half, col_chunk):
    i = pl.program_id(0)
    half = n_half
    width = hbuf.shape[1]
    uw = u_ref.shape[1] // 2

    @pl.when(i == 0)
    def _():
        hstate[...] = h0_ref[...]

    u = u_ref[...]
    for c in range(2):
        hbuf[:, c * 2 * half:(c + 1) * 2 * half] = _dot(u[:, c * uw:(c + 1) * uw], bbd_ref[c])

    for c in range(2):
        for cc in range(half // col_chunk):
            re0 = c * 2 * half + cc * col_chunk
            im0 = re0 + half
            l0 = c * half + cc * col_chunk
            ar = lre_ref[:, l0:l0 + col_chunk]
            ai = lim_ref[:, l0:l0 + col_chunk]

            def step(s, carry, re0=re0, im0=im0, ar=ar, ai=ai):
                hr, hi = carry
                r0 = pl.multiple_of(s * SUBLANES, SUBLANES)
                nr = ar * hr - ai * hi + hbuf[pl.ds(r0, SUBLANES), re0:re0 + col_chunk]
                ni = ar * hi + ai * hr + hbuf[pl.ds(r0, SUBLANES), im0:im0 + col_chunk]
                hbuf[pl.ds(r0, SUBLANES), re0:re0 + col_chunk] = nr
                hbuf[pl.ds(r0, SUBLANES), im0:im0 + col_chunk] = ni
                return nr, ni

            hr, hi = lax.fori_loop(
                0, ts, step,
                (hstate[:, re0:re0 + col_chunk], hstate[:, im0:im0 + col_chunk]), unroll=8)
            hstate[:, re0:re0 + col_chunk] = hr
            hstate[:, im0:im0 + col_chunk] = hi

    hout_ref[...] = hstate[...]

    y = jnp.concatenate(
        [_dot(hbuf[:, c * 2 * half:(c + 1) * 2 * half].astype(BF16), cbd_ref[c]) for c in range(2)], axis=-1)
    y = y + d_ref[...] * u.astype(F32)
    z = jax.nn.gelu(y)
    gate = _dot(z.astype(BF16), wglu_ref[...]) + bglu_ref[...]
    y_ref[...] = (z * jax.nn.sigmoid(gate)).astype(y_ref.dtype)


def _s5_call(u_tb, h0, w, *, ts):
    rows, s5_width = u_tb.shape
    n_state = w['lam_re'].shape[1]
    half = n_state // 2
    width = 2 * n_state
    n_steps = rows // (ts * SUBLANES)
    full = lambda shape: pl.BlockSpec(shape, lambda i: (0,) * len(shape))
    kern = functools.partial(_s5_kernel, ts=ts, n_half=half, col_chunk=512)
    return pl.pallas_call(
        kern,
        grid=(n_steps,),
        in_specs=[
            pl.BlockSpec((ts * SUBLANES, s5_width), lambda i: (i, 0)),
            full((SUBLANES, width)), full((SUBLANES, n_state)), full((SUBLANES, n_state)),
            full((2, s5_width // 2, 2 * half)), full((2, 2 * half, s5_width // 2)),
            full((1, s5_width)), full((s5_width, s5_width)), full((1, s5_width)),
        ],
        out_specs=[
            pl.BlockSpec((ts * SUBLANES, s5_width), lambda i: (i, 0)),
            full((SUBLANES, width)),
        ],
        out_shape=[
            jax.ShapeDtypeStruct((rows, s5_width), BF16),
            jax.ShapeDtypeStruct((SUBLANES, width), F32),
        ],
        scratch_shapes=[
            pltpu.VMEM((ts * SUBLANES, width), F32),
            pltpu.VMEM((SUBLANES, width), F32),
        ],
        compiler_params=pltpu.CompilerParams(
            dimension_semantics=("arbitrary",), vmem_limit_bytes=VMEM_LIMIT),
    )(u_tb, h0, w['lam_re'], w['lam_im'], w['b_bd'], w['c_bd'], w['s5_d'], w['w_glu'], w['b_glu'])


def _out_kernel(x_ref, s5_ref, at_ref, gs5_ref, gat_ref, wout_ref, gffn_ref, wr_ref, br_ref,
                h_ref, hn_ref, route_ref):
    s5n = _rms(s5_ref[...].astype(F32), gs5_ref[...])
    atn = _rms(at_ref[0].astype(F32), gat_ref[...])
    merged = jnp.concatenate([s5n, atn], axis=-1).astype(BF16)
    h = x_ref[0] + _dot(merged, wout_ref[...])
    h_ref[0] = h
    hn32 = _rms(h, gffn_ref[...])
    hn_ref[0] = hn32
    hn = hn32.astype(BF16)

    logits = _dot(hn, wr_ref[...]) + br_ref[...]
    lane_i = _lane_iota(logits.shape)
    lane = lane_i.astype(F32)
    lane_group = (lane_i // EXPERTS_PER_GROUP).astype(F32)
    big = float(LANES)
    is_g = (lane_i >= N_EXPERTS) & (lane_i < N_EXPERTS + N_EXPERT_GROUPS)
    gl = jnp.where(is_g, logits, NEG_INF)
    gmax = jnp.max(gl, axis=-1, keepdims=True)
    g_idx = jnp.min(jnp.where(gl == gmax, lane, big), axis=-1, keepdims=True) - N_EXPERTS
    g_top = 1.0 / jnp.sum(jnp.where(is_g, jnp.exp(gl - gmax), 0.0), axis=-1, keepdims=True)
    in_group = (lane_i < N_EXPERTS) & (lane_group == g_idx)
    el = jnp.where(in_group, logits, NEG_INF)
    v1 = jnp.max(el, axis=-1, keepdims=True)
    i1 = jnp.min(jnp.where(el == v1, lane, big), axis=-1, keepdims=True)
    el2 = jnp.where(lane == i1, NEG_INF, el)
    v2 = jnp.max(el2, axis=-1, keepdims=True)
    i2 = jnp.min(jnp.where(el2 == v2, lane, big), axis=-1, keepdims=True)
    e21 = jnp.exp(v2 - v1)
    w1 = g_top / (1.0 + e21)
    w2 = g_top * e21 / (1.0 + e21)
    route_ref[0] = (jnp.where(lane_i == 0, i1, 0.0) + jnp.where(lane_i == 1, i2, 0.0)
                    + jnp.where(lane_i == 2, w1, 0.0) + jnp.where(lane_i == 3, w2, 0.0))


def _out_call(x, s5_tb, attn, w, *, tm):
    bsz, seq, dm = x.shape
    s5_width = s5_tb.shape[1] // bsz
    aw = attn.shape[2]
    full = lambda shape: pl.BlockSpec(shape, lambda b, i: (0,) * len(shape))
    tok = lambda width: pl.BlockSpec((1, tm, width), lambda b, i: (b, i, 0))
    return pl.pallas_call(
        _out_kernel,
        grid=(bsz, seq // tm),
        in_specs=[
            tok(dm),
            pl.BlockSpec((tm, s5_width), lambda b, i: (i, b)),
            tok(aw),
            full((1, s5_width)), full((1, aw)), full((s5_width + aw, dm)), full((1, dm)),
            full((dm, LANES)), full((1, LANES)),
        ],
        out_specs=[tok(dm), tok(dm), tok(LANES)],
        out_shape=[
            jax.ShapeDtypeStruct((bsz, seq, dm), F32),
            jax.ShapeDtypeStruct((bsz, seq, dm), F32),
            jax.ShapeDtypeStruct((bsz, seq, LANES), F32),
        ],
        compiler_params=pltpu.CompilerParams(
            dimension_semantics=("parallel", "parallel"), vmem_limit_bytes=VMEM_LIMIT),
    )(x, s5_tb, attn, w['g_s5'], w['g_attn'], w['w_out'], w['g_ffn'], w['w_router'], w['b_router'])


def _moe_kernel(offs_ref, toks_ref, gws_ref, hn_ref, h_hbm, wgu_ref, wd_ref, o_ref, gbuf, ybuf, sem,
                *, d_expert, tb, mc):
    b = pl.program_id(0)
    e = pl.program_id(1)
    tok_rows = SUBLANES

    @pl.when(e == 0)
    def _():
        cp = pltpu.make_async_copy(h_hbm.at[pl.ds(b * tb * tok_rows, tb * tok_rows)], o_ref, sem)
        cp.start()
        gbuf[...] = jnp.zeros_like(gbuf)
        cp.wait()

    off = offs_ref[0, 0, e]
    cnt = offs_ref[0, 0, e + 1] - off

    def chunk(c, carry):
        base = off + c * mc
        n = jnp.minimum(mc, cnt - c * mc)

        def gather8(g, carry):
            for u in range(SUBLANES):
                i = g * SUBLANES + u
                tok = toks_ref[0, 0, base + jnp.minimum(i, n - 1)]
                gbuf[pl.ds(pl.multiple_of(i * tok_rows, tok_rows), tok_rows), :] = (
                    hn_ref[pl.ds(pl.multiple_of(tok * tok_rows, tok_rows), tok_rows), :])
            return carry

        lax.fori_loop(0, (n + SUBLANES - 1) // SUBLANES, gather8, 0)

        x = jnp.concatenate([gbuf[pl.ds(j, mc, stride=tok_rows), :].astype(BF16) for j in range(tok_rows)],
                            axis=-1)
        gu = _dot(x, wgu_ref[0])
        hdn = jax.nn.silu(gu[:, :d_expert]) * gu[:, d_expert:]
        y = _dot(hdn.astype(BF16), wd_ref[0])
        for j in range(tok_rows):
            ybuf[pl.ds(j, mc, stride=tok_rows), :] = y[:, j * LANES:(j + 1) * LANES]

        def add_rows(first, count):
            idx = [first + u for u in range(count)]
            dst = [pl.multiple_of(toks_ref[0, 0, base + i] * tok_rows, tok_rows) for i in idx]
            val = [o_ref[pl.ds(d, tok_rows), :]
                   + gws_ref[0, 0, base + i] * ybuf[pl.ds(pl.multiple_of(i * tok_rows, tok_rows), tok_rows), :]
                   for i, d in zip(idx, dst)]
            for d, v in zip(dst, val):
                o_ref[pl.ds(d, tok_rows), :] = v

        n_full = n // SUBLANES

        def scatter8(g, carry):
            add_rows(g * SUBLANES, SUBLANES)
            return carry

        def scatter1(i, carry):
            add_rows(i, 1)
            return carry

        lax.fori_loop(0, n_full, scatter8, 0)
        lax.fori_loop(n_full * SUBLANES, n, scatter1, 0)
        return carry

    lax.fori_loop(0, (cnt + mc - 1) // mc, chunk, 0)


def _moe_call(offs, toks, gws, hn, h, w, *, tb, mc):
    t, dm = h.shape
    assert dm == SUBLANES * LANES, "a token row must be exactly one (8,128) f32 tile"
    nb = t // tb
    d_expert = w['w_d'].shape[1]
    tok_rows = SUBLANES
    kern = functools.partial(_moe_kernel, d_expert=d_expert, tb=tb, mc=mc)
    smem = lambda n: pl.BlockSpec((1, 1, n), lambda b, e: (b, 0, 0), memory_space=pltpu.SMEM)
    out = pl.pallas_call(
        kern,
        grid=(nb, N_EXPERTS),
        in_specs=[
            smem(LANES), smem(2 * tb), smem(2 * tb),
            pl.BlockSpec((tb * tok_rows, LANES), lambda b, e: (b, 0)),
            pl.BlockSpec(memory_space=pl.ANY),
            pl.BlockSpec((1, dm, 2 * d_expert), lambda b, e: (e, 0, 0)),
            pl.BlockSpec((1, d_expert, dm), lambda b, e: (e, 0, 0)),
        ],
        out_specs=pl.BlockSpec((tb * tok_rows, LANES), lambda b, e: (b, 0)),
        out_shape=jax.ShapeDtypeStruct((t * tok_rows, LANES), F32),
        scratch_shapes=[
            pltpu.VMEM((mc * tok_rows, LANES), F32),
            pltpu.VMEM((mc * tok_rows, LANES), F32),
            pltpu.SemaphoreType.DMA(()),
        ],
        compiler_params=pltpu.CompilerParams(
            dimension_semantics=("parallel", "arbitrary"), vmem_limit_bytes=VMEM_LIMIT),
    )(offs, toks, gws, hn.reshape(t * tok_rows, LANES), h.reshape(t * tok_rows, LANES), w['w_gu'], w['w_d'])
    return out.reshape(t, dm)


def _route_tables(route, tb):
    t = route.shape[0]
    nb = t // tb
    ids = route[:, :2].astype(jnp.int32).reshape(nb, 2 * tb)
    wts = route[:, 2:4].reshape(nb, 2 * tb)
    order = jnp.argsort(ids, axis=1, stable=True)
    toks = (order // 2).astype(jnp.int32)
    gws = jnp.take_along_axis(wts, order, axis=1)
    counts = jnp.sum(ids[:, :, None] == jnp.arange(N_EXPERTS, dtype=jnp.int32), axis=1, dtype=jnp.int32)
    offs = jnp.concatenate([jnp.zeros((nb, 1), jnp.int32), jnp.cumsum(counts, axis=1)], axis=1)
    offs = jnp.pad(offs, ((0, 0), (0, LANES - offs.shape[1])))
    return offs[:, None, :], toks[:, None, :], gws[:, None, :]


def _prepare_weights(p):
    dm, in_cols = p['w_in'].shape
    s5_width = p['s5_d'].shape[0]
    q_lora = p['q_lora_norm_g'].shape[0]
    kv_lora = p['kv_lora_norm_g'].shape[0]
    half_r = QK_ROPE // 2
    w = {'s5_width': s5_width, 'q_lora': q_lora, 'kv_lora': kv_lora}

    def partner(a):
        return jnp.concatenate([-a[..., half_r:], a[..., :half_r]], axis=-1)

    def swap(a):
        return jnp.concatenate([a[..., half_r:], a[..., :half_r]], axis=-1)

    w_kr = p['w_in'][:, in_cols - QK_ROPE:]
    w['w_in'] = jnp.concatenate(
        [p['w_in'], partner(w_kr), jnp.zeros((dm, LANES - 2 * QK_ROPE), F32)], axis=-1).astype(BF16)
    w['g_mix'] = p['norm_mix_g'][None]
    w['g_q'] = p['q_lora_norm_g'][None]
    w['g_kv'] = p['kv_lora_norm_g'][None]
    gkr = p['k_rope_norm_g']
    w['g_kr'] = jnp.concatenate([gkr, swap(gkr), jnp.zeros((LANES - 2 * QK_ROPE,), F32)])[None]

    wq = p['w_uq']
    wq_r = wq[..., QK_NOPE:]
    w['w_uq'] = jnp.concatenate([wq, partner(wq_r)], axis=-1).reshape(q_lora, N_HEADS * HEAD_PAD).astype(BF16)
    gr = p['q_rope_norm_g']
    w['q_gain'] = jnp.tile(jnp.concatenate([p['q_nope_norm_g'], gr, swap(gr)]), N_HEADS)[None]
    idx = jnp.arange(HEAD_PAD)
    row_nope = (idx < QK_NOPE)[:, None]
    row_rope = ((idx >= QK_NOPE) & (idx < QK_DIM))[:, None]
    col_nope = (idx < QK_NOPE)[None, :]
    m_q = jnp.where(row_nope & col_nope, 1.0 / QK_NOPE, 0.0) + jnp.where(row_rope & ~col_nope, 1.0 / QK_ROPE, 0.0)
    m_k = jnp.where(row_nope, 1.0 / QK_NOPE, 0.0) * jnp.ones((1, HEAD_PAD), F32)
    eye2 = jnp.eye(2, dtype=F32)
    w['bd_q'] = jnp.kron(eye2, m_q).astype(BF16)
    w['bd_k'] = jnp.kron(eye2, m_k).astype(BF16)

    pad_h = lambda a: jnp.pad(a, ((0, 0), (0, 0), (0, HEAD_PAD - a.shape[-1])))
    w['w_uk'] = pad_h(p['w_uk']).reshape(kv_lora, N_HEADS * HEAD_PAD).astype(BF16)
    w['w_uv'] = pad_h(p['w_uv']).reshape(kv_lora, N_HEADS * HEAD_PAD).astype(BF16)
    w['k_gain'] = jnp.tile(jnp.pad(p['k_nope_norm_g'], (0, HEAD_PAD - QK_NOPE)), N_HEADS)[None]

    dt = jnp.exp(p['s5_log_dt'])[:, None]
    ar, ai = p['s5_a_re'], p['s5_a_im']
    mag = jnp.exp(dt * ar)
    abar_re = mag * jnp.cos(dt * ai)
    abar_im = mag * jnp.sin(dt * ai)
    den = ar * ar + ai * ai
    num_re = abar_re - 1.0
    coef_re = (num_re * ar + abar_im * ai) / den
    coef_im = (abar_im * ar - num_re * ai) / den
    br, bi = p['s5_b_re'], p['s5_b_im']
    bbar_re = coef_re[..., None] * br - coef_im[..., None] * bi
    bbar_im = coef_re[..., None] * bi + coef_im[..., None] * br
    g = ar.shape[0]
    n_state = g * S5_STATE
    half = n_state // 2
    eye_g = jnp.eye(g, dtype=F32)
    b_re = jnp.einsum('gni,gh->gihn', bbar_re, eye_g).reshape(s5_width, n_state)
    b_im = jnp.einsum('gni,gh->gihn', bbar_im, eye_g).reshape(s5_width, n_state)
    c_re = jnp.einsum('gon,gh->gnho', p['s5_c_re'], eye_g).reshape(n_state, s5_width)
    c_im = jnp.einsum('gon,gh->gnho', p['s5_c_im'], eye_g).reshape(n_state, s5_width)
    uw = s5_width // 2
    w['b_bd'] = jnp.stack([
        jnp.concatenate([b_re[c * uw:(c + 1) * uw, c * half:(c + 1) * half],
                         b_im[c * uw:(c + 1) * uw, c * half:(c + 1) * half]], axis=1) for c in range(2)]).astype(BF16)
    w['c_bd'] = jnp.stack([
        jnp.concatenate([c_re[c * half:(c + 1) * half, c * uw:(c + 1) * uw],
                         -c_im[c * half:(c + 1) * half, c * uw:(c + 1) * uw]], axis=0) for c in range(2)]).astype(BF16)
    w['lam_re'] = jnp.broadcast_to(abar_re.reshape(1, n_state), (SUBLANES, n_state))
    w['lam_im'] = jnp.broadcast_to(abar_im.reshape(1, n_state), (SUBLANES, n_state))
    w['s5_d'] = p['s5_d'][None]
    w['w_glu'] = p['s5_w_glu'].astype(BF16)
    w['b_glu'] = p['s5_b_glu'][None]

    w['g_s5'] = p['out_norm_s5_g'][None]
    w['g_attn'] = p['out_norm_attn_g'][None]
    w['w_out'] = p['w_out'].astype(BF16)
    w['g_ffn'] = p['norm_ffn_g'][None]
    n_r = N_EXPERTS + N_EXPERT_GROUPS
    w['w_router'] = jnp.pad(jnp.concatenate([p['w_router_expert'], p['w_router_group']], axis=1),
                            ((0, 0), (0, LANES - n_r))).astype(BF16)
    w['b_router'] = jnp.pad(jnp.concatenate([p['b_router_expert'], p['b_router_group']]), (0, LANES - n_r))[None]
    w['w_gu'] = jnp.concatenate([p['w_e_gate'], p['w_e_up']], axis=-1).astype(BF16)
    w['w_d'] = p['w_e_down'].astype(BF16)
    return w


def _rope_table(pos):
    half = QK_ROPE // 2
    inv = ROPE_THETA ** (-jnp.arange(half, dtype=F32) / half)
    ang = pos.astype(F32)[:, None] * inv[None, :]
    cc = jnp.tile(jnp.cos(ang), (1, 2))
    ss = jnp.tile(jnp.sin(ang), (1, 2))
    n = pos.shape[0]
    scale = QK_DIM ** -0.5 * math.log2(math.e)
    tab_q = scale * jnp.concatenate([jnp.ones((n, QK_NOPE), F32), cc, ss], axis=-1)
    tab_k = jnp.concatenate([cc, ss, jnp.zeros((n, LANES - 2 * QK_ROPE), F32)], axis=-1)
    return jnp.concatenate([tab_q, tab_k], axis=-1)


def _s5_state_in(re, im):
    b = re.shape[0]
    re = re.reshape(b, -1)
    im = im.reshape(b, -1)
    half = re.shape[1] // 2
    return jnp.concatenate([re[:, :half], im[:, :half], re[:, half:], im[:, half:]], axis=1)


def _s5_state_out(h, g):
    b = h.shape[0]
    half = h.shape[1] // 4
    re = jnp.concatenate([h[:, :half], h[:, 2 * half:3 * half]], axis=1).reshape(b, g, S5_STATE)
    im = jnp.concatenate([h[:, half:2 * half], h[:, 3 * half:]], axis=1).reshape(b, g, S5_STATE)
    return re, im


def _trunk_layer(x, past_ckv, past_krope, h0_re, h0_im, w):
    bsz, seq, dm = x.shape
    assert bsz == SUBLANES, "the S5 scan keeps the batch on the sublanes"
    past_len = 0 if past_ckv is None else past_ckv.shape[1]
    g = w['lam_re'].shape[1] // S5_STATE
    tm = _tile(seq, 512)

    tab = _rope_table(past_len + jnp.arange(seq, dtype=jnp.int32))
    u_tb, q, new_ckv, new_krope = _proj_call(x, tab, w, tm=tm)

    if h0_re is None:
        h0 = jnp.zeros((bsz, 2 * g * S5_STATE), F32)
    else:
        h0 = _s5_state_in(h0_re.astype(F32), h0_im.astype(F32))
    s5_tb, h_last = _s5_call(u_tb.reshape(seq * bsz, -1), h0, w, ts=min(seq, 64))
    h_re, h_im = _s5_state_out(h_last, g)

    if past_ckv is None:
        all_ckv, all_krope = new_ckv, new_krope
    else:
        all_ckv = jnp.concatenate([past_ckv.astype(F32), new_ckv], axis=1)
        all_krope = jnp.concatenate([past_krope.astype(F32), new_krope], axis=1)
    sk = all_ckv.shape[1]
    sk_pad = -(-sk // LANES) * LANES
    all_ckv = jnp.pad(all_ckv, ((0, 0), (0, sk_pad - sk), (0, 0)))
    krp = jnp.pad(all_krope, ((0, 0), (0, sk_pad - sk), (QK_NOPE, LANES - QK_DIM)))
    whole = sk_pad % ATTN_TK != 0
    k, vt = _kv_call(all_ckv, krp, w, tm=sk_pad if whole else KV_TILE)
    sq_pad = -(-seq // LANES) * LANES
    q = jnp.pad(q, ((0, 0), (0, 0), (0, sq_pad - seq), (0, 0)))
    tq = ATTN_TQ if sq_pad % ATTN_TQ == 0 else LANES
    attn = _attn_call(q, k, vt, tq=tq, tk=sk_pad if whole else ATTN_TK, q_pos0=past_len, sk_valid=sk)[:, :seq]

    h, hn, route = _out_call(x, s5_tb.reshape(seq, -1), attn, w, tm=tm)
    t = bsz * seq
    tb = _tile(t, MOE_BLOCK)
    offs, toks, gws = _route_tables(route.reshape(t, LANES), tb)
    y = _moe_call(offs, toks, gws, hn.reshape(t, dm), h.reshape(t, dm), w, tb=tb, mc=MOE_CHUNK)
    return y.reshape(bsz, seq, dm), new_ckv, new_krope, h_re, h_im


def kernel(x_prompt, x_sample, cache_ckv, cache_krope, state_s5_re, state_s5_im, norm_mix_g, w_in, s5_a_re, s5_a_im, s5_log_dt, s5_b_re, s5_b_im, s5_c_re, s5_c_im, s5_d, s5_w_glu, s5_b_glu, q_lora_norm_g, w_uq, kv_lora_norm_g, w_uk, w_uv, q_nope_norm_g, q_rope_norm_g, k_nope_norm_g, k_rope_norm_g, out_norm_s5_g, out_norm_attn_g, w_out, norm_ffn_g, w_router_group, b_router_group, w_router_expert, b_router_expert, w_e_gate, w_e_up, w_e_down):
    params = dict(
        norm_mix_g=norm_mix_g, w_in=w_in, s5_a_re=s5_a_re, s5_a_im=s5_a_im, s5_log_dt=s5_log_dt,
        s5_b_re=s5_b_re, s5_b_im=s5_b_im, s5_c_re=s5_c_re, s5_c_im=s5_c_im, s5_d=s5_d,
        s5_w_glu=s5_w_glu, s5_b_glu=s5_b_glu, q_lora_norm_g=q_lora_norm_g, w_uq=w_uq,
        kv_lora_norm_g=kv_lora_norm_g, w_uk=w_uk, w_uv=w_uv, q_nope_norm_g=q_nope_norm_g,
        q_rope_norm_g=q_rope_norm_g, k_nope_norm_g=k_nope_norm_g, k_rope_norm_g=k_rope_norm_g,
        out_norm_s5_g=out_norm_s5_g, out_norm_attn_g=out_norm_attn_g, w_out=w_out, norm_ffn_g=norm_ffn_g,
        w_router_group=w_router_group, b_router_group=b_router_group,
        w_router_expert=w_router_expert, b_router_expert=b_router_expert,
        w_e_gate=w_e_gate, w_e_up=w_e_up, w_e_down=w_e_down)
    depth = w_in.shape[0]
    y_p, y_s = x_prompt, x_sample
    outs = [[] for _ in range(8)]
    for l in range(depth):
        w = _prepare_weights({k: a[l] for k, a in params.items()})
        y_p, c1, k1, r1, i1 = _trunk_layer(y_p, None, None, None, None, w)
        y_s, c2, k2, r2, i2 = _trunk_layer(y_s, cache_ckv[l], cache_krope[l], state_s5_re[l], state_s5_im[l], w)
        for lst, a in zip(outs, (c1, k1, r1, i1, c2, k2, r2, i2)):
            lst.append(a)
    return (y_p, y_s) + tuple(jnp.stack(lst) for lst in outs)
```

```python
import functools
import math

import jax
import jax.numpy as jnp
from jax import lax
from jax.experimental import pallas as pl
from jax.experimental.pallas import tpu as pltpu

F32 = jnp.float32
BF16 = jnp.bfloat16

CHUNK = 64
S5_GROUP = 16
S5_STATE = 64
N_HEADS = 8
QK_NOPE = 64
QK_ROPE = 32
QK_DIM = QK_NOPE + QK_ROPE
V_DIM = 64
ROPE_THETA = 10000.0
N_EXPERT_GROUPS = 4
EXPERTS_PER_GROUP = 8
N_EXPERTS = N_EXPERT_GROUPS * EXPERTS_PER_GROUP
EPS = 1e-6
NEG_INF = -1e30

LANES = 128
SUBLANES = 8
HEAD_PAD = LANES
ATTN_TQ = 512
ATTN_TK = 512
KV_TILE = 512
MOE_BLOCK = 2048
MOE_CHUNK = 256
ATTN_COLS = 256
VT_ROWS = 80
VMEM_LIMIT = 48 * 1024 * 1024


def _tile(n, cap):
    for t in range(min(cap, n), 15, -1):
        if n % t == 0 and t % 16 == 0:
            return t
    return n


def _rms(x, g):
    return x * lax.rsqrt(jnp.mean(x * x, axis=-1, keepdims=True) + EPS) * g


def _dot(a, b):
    return jnp.dot(a, b, preferred_element_type=F32)


def _lane_iota(shape):
    return lax.broadcasted_iota(jnp.int32, shape, len(shape) - 1)


def _proj_kernel(x_ref, gmix_ref, win_ref, gq_ref, wuq_ref, bdq_ref, qgain_ref, gkv_ref, gkr_ref, tab_ref,
                 u_ref, q_ref, ckv_ref, kr_ref, *, s5_width, q_lora, kv_lora):
    x = x_ref[0]
    xn = _rms(x, gmix_ref[...])
    proj = _dot(xn.astype(BF16), win_ref[...])
    u_ref[...] = proj[:, :s5_width].astype(BF16)

    o = s5_width
    cqn = _rms(proj[:, o:o + q_lora], gq_ref[...])
    o += q_lora
    ckv_ref[0] = _rms(proj[:, o:o + kv_lora], gkv_ref[...])
    o += kv_lora

    tab = tab_ref[...]
    lane = _lane_iota((x.shape[0], LANES))

    kr = proj[:, o:o + LANES]
    ms = jnp.sum(jnp.where(lane < QK_ROPE, kr * kr, 0.0), axis=-1, keepdims=True) * (1.0 / QK_ROPE)
    t = kr * lax.rsqrt(ms + EPS) * gkr_ref[...] * tab[:, LANES:]
    kro = t + pltpu.roll(t, LANES - QK_ROPE, axis=1)
    kr_ref[0] = kro[:, :QK_ROPE]

    qa = _dot(cqn.astype(BF16), wuq_ref[...])
    qsq = (qa * qa).astype(BF16)
    pair = 2 * HEAD_PAD
    ms = jnp.concatenate([_dot(qsq[:, c * pair:(c + 1) * pair], bdq_ref[...])
                          for c in range(N_HEADS // 2)], axis=-1)
    qn = qa * lax.rsqrt(ms + EPS) * qgain_ref[...]
    is_rope = (lane >= QK_NOPE) & (lane < QK_DIM)
    for h in range(N_HEADS):
        th = qn[:, h * HEAD_PAD:(h + 1) * HEAD_PAD] * tab[:, :LANES]
        rolled = pltpu.roll(th, LANES - QK_ROPE, axis=1)
        oh = jnp.where(lane < QK_DIM, th + jnp.where(is_rope, rolled, 0.0), 0.0)
        q_ref[0, h] = oh.astype(BF16)


def _proj_call(x, tab, w, *, tm):
    bsz, seq, dm = x.shape
    s5_width, q_lora, kv_lora = w['s5_width'], w['q_lora'], w['kv_lora']
    in_cols = w['w_in'].shape[1]
    hp = N_HEADS * HEAD_PAD
    full = lambda shape: pl.BlockSpec(shape, lambda b, i: (0,) * len(shape))
    kern = functools.partial(_proj_kernel, s5_width=s5_width, q_lora=q_lora, kv_lora=kv_lora)
    return pl.pallas_call(
        kern,
        grid=(bsz, seq // tm),
        in_specs=[
            pl.BlockSpec((1, tm, dm), lambda b, i: (b, i, 0)),
            full((1, dm)), full((dm, in_cols)), full((1, q_lora)), full((q_lora, hp)),
            full((2 * HEAD_PAD, 2 * HEAD_PAD)), full((1, hp)), full((1, kv_lora)), full((1, LANES)),
            pl.BlockSpec((tm, 2 * LANES), lambda b, i: (i, 0)),
        ],
        out_specs=[
            pl.BlockSpec((tm, s5_width), lambda b, i: (i, b)),
            pl.BlockSpec((1, N_HEADS, tm, HEAD_PAD), lambda b, i: (b, 0, i, 0)),
            pl.BlockSpec((1, tm, kv_lora), lambda b, i: (b, i, 0)),
            pl.BlockSpec((1, tm, QK_ROPE), lambda b, i: (b, i, 0)),
        ],
        out_shape=[
            jax.ShapeDtypeStruct((seq, bsz * s5_width), BF16),
            jax.ShapeDtypeStruct((bsz, N_HEADS, seq, HEAD_PAD), BF16),
            jax.ShapeDtypeStruct((bsz, seq, kv_lora), F32),
            jax.ShapeDtypeStruct((bsz, seq, QK_ROPE), F32),
        ],
        compiler_params=pltpu.CompilerParams(
            dimension_semantics=("parallel", "parallel"), vmem_limit_bytes=VMEM_LIMIT),
    )(x, w['g_mix'], w['w_in'], w['g_q'], w['w_uq'], w['bd_q'], w['q_gain'], w['g_kv'], w['g_kr'], tab)


def _kv_kernel(ckv_ref, krp_ref, wuk_ref, wuv_ref, bdk_ref, kg_ref, k_ref, v_ref):
    c = ckv_ref[0].astype(BF16)
    ka = _dot(c, wuk_ref[...])
    va = _dot(c, wuv_ref[...])
    ksq = (ka * ka).astype(BF16)
    pair = 2 * HEAD_PAD
    ms = jnp.concatenate([_dot(ksq[:, c2 * pair:(c2 + 1) * pair], bdk_ref[...])
                          for c2 in range(N_HEADS // 2)], axis=-1)
    kn = ka * lax.rsqrt(ms + EPS) * kg_ref[...]
    krp = krp_ref[0]
    lane = _lane_iota(krp.shape)
    ones_col = jnp.where(lane == V_DIM, 1.0, 0.0)
    for h in range(N_HEADS):
        k_ref[0, h] = (kn[:, h * HEAD_PAD:(h + 1) * HEAD_PAD] + krp).astype(BF16)
        vt = (va[:, h * HEAD_PAD:(h + 1) * HEAD_PAD] + ones_col).T
        v_ref[0, h] = vt[:VT_ROWS].astype(BF16)


def _kv_call(ckv, krp, w, *, tm):
    bsz, sk, kv_lora = ckv.shape
    hp = N_HEADS * HEAD_PAD
    full = lambda shape: pl.BlockSpec(shape, lambda b, i: (0,) * len(shape))
    return pl.pallas_call(
        _kv_kernel,
        grid=(bsz, sk // tm),
        in_specs=[
            pl.BlockSpec((1, tm, kv_lora), lambda b, i: (b, i, 0)),
            pl.BlockSpec((1, tm, LANES), lambda b, i: (b, i, 0)),
            full((kv_lora, hp)), full((kv_lora, hp)), full((2 * HEAD_PAD, 2 * HEAD_PAD)), full((1, hp)),
        ],
        out_specs=[
            pl.BlockSpec((1, N_HEADS, tm, HEAD_PAD), lambda b, i: (b, 0, i, 0)),
            pl.BlockSpec((1, N_HEADS, VT_ROWS, tm), lambda b, i: (b, 0, 0, i)),
        ],
        out_shape=[
            jax.ShapeDtypeStruct((bsz, N_HEADS, sk, HEAD_PAD), BF16),
            jax.ShapeDtypeStruct((bsz, N_HEADS, VT_ROWS, sk), BF16),
        ],
        compiler_params=pltpu.CompilerParams(
            dimension_semantics=("parallel", "parallel"), vmem_limit_bytes=VMEM_LIMIT),
    )(ckv, krp, w['w_uk'], w['w_uv'], w['bd_k'], w['k_gain'])


def _attn_kernel(q_ref, k_ref, vt_ref, o_ref, *, tq, tk, q_pos0, sk_valid):
    sq = q_ref.shape[2]
    nq = sq // tq

    n_heads = q_ref.shape[1]

    cw = min(tq, ATTN_COLS)
    streams = [(hh, c) for hh in range(n_heads) for c in range(tq // cw)]

    def kv_step(qs, p0, j, carry, masked):
        k0 = pl.multiple_of(j * tk, tk)
        kbs = [k_ref[0, hh, pl.ds(k0, tk), :] for hh in range(n_heads)]
        ss = [lax.dot_general(kbs[hh], qs[i], (((1,), (1,)), ((), ())), preferred_element_type=F32)
              for i, (hh, c) in enumerate(streams)]
        if masked:
            k_pos = k0 + lax.broadcasted_iota(jnp.int32, (tk, 1), 0)
        new = []
        for i, (hh, c) in enumerate(streams):
            m, acc = carry[i]
            s = ss[i]
            if masked:
                q_chunk = (p0 + c * cw + lax.broadcasted_iota(jnp.int32, (1, cw), 1)) // CHUNK
                s = jnp.where((k_pos // CHUNK <= q_chunk) & (k_pos < sk_valid), s, NEG_INF)
            m_new = jnp.maximum(m, jnp.max(s, axis=0, keepdims=True))
            alpha = jnp.exp2(m - m_new)
            p = jnp.exp2(s - m_new)
            acc = alpha * acc + _dot(vt_ref[0, hh, :, pl.ds(k0, tk)], p.astype(BF16))
            new.append((m_new, acc))
        return tuple(new)

    def q_body(qi, carry):
        q0 = pl.multiple_of(qi * tq, tq)
        qs = [q_ref[0, hh, pl.ds(q0 + c * cw, cw), :] for hh, c in streams]
        p0 = q_pos0 + q0
        n_open = jnp.minimum((p0 // CHUNK + 1) * CHUNK, sk_valid) // tk
        v1 = jnp.minimum(((p0 + tq - 1) // CHUNK + 1) * CHUNK, sk_valid)
        n_all = (v1 + tk - 1) // tk
        init = tuple((jnp.full((1, cw), NEG_INF, F32), jnp.zeros((VT_ROWS, cw), F32)) for _ in streams)
        st = lax.fori_loop(0, n_open, lambda j, c: kv_step(qs, p0, j, c, False), init)
        st = lax.fori_loop(n_open, n_all, lambda j, c: kv_step(qs, p0, j, c, True), st)
        for c in range(tq // cw):
            ot = jnp.concatenate([st[i][1][:V_DIM] / st[i][1][V_DIM:V_DIM + 1]
                                  for i, (hh, c2) in enumerate(streams) if c2 == c], axis=0)
            o_ref[0, pl.ds(q0 + c * cw, cw), :] = ot.T.astype(o_ref.dtype)
        return carry

    lax.fori_loop(0, nq, q_body, 0)


def _attn_call(q, k, vt, *, tq, tk, q_pos0, sk_valid):
    bsz, nh, sq, _ = q.shape
    sk = k.shape[2]
    assert sq % tq == 0 and sk % tk == 0 and tq % LANES == 0 and tk % LANES == 0
    kern = functools.partial(_attn_kernel, tq=tq, tk=tk, q_pos0=q_pos0, sk_valid=sk_valid)
    return pl.pallas_call(
        kern,
        grid=(bsz, nh // 2),
        in_specs=[
            pl.BlockSpec((1, 2, sq, HEAD_PAD), lambda b, h: (b, h, 0, 0)),
            pl.BlockSpec((1, 2, sk, HEAD_PAD), lambda b, h: (b, h, 0, 0)),
            pl.BlockSpec((1, 2, VT_ROWS, sk), lambda b, h: (b, h, 0, 0)),
        ],
        out_specs=pl.BlockSpec((1, sq, 2 * V_DIM), lambda b, h: (b, 0, h)),
        out_shape=jax.ShapeDtypeStruct((bsz, sq, nh * V_DIM), BF16),
        compiler_params=pltpu.CompilerParams(
            dimension_semantics=("parallel", "parallel"), vmem_limit_bytes=VMEM_LIMIT),
    )(q, k, vt)


def _s5_kernel(u_ref, h0_ref, lre_ref, lim_ref, bbd_ref, cbd_ref, d_ref, wglu_ref, bglu_ref,
               y_ref, hout_ref, hbuf, hstate, *, ts, n_half, col_chunk):
    i = pl.program_id(0)
    half = n_half
    width = hbuf.shape[1]
    uw = u_ref.shape[1] // 2

    @pl.when(i == 0)
    def _():
        hstate[...] = h0_ref[...]

    u = u_ref[...]
    for c in range(2):
        hbuf[:, c * 2 * half:(c + 1) * 2 * half] = _dot(u[:, c * uw:(c + 1) * uw], bbd_ref[c])

    for c in range(2):
        for cc in range(half // col_chunk):
            re0 = c * 2 * half + cc * col_chunk
            im0 = re0 + half
            l0 = c * half + cc * col_chunk
            ar = lre_ref[:, l0:l0 + col_chunk]
            ai = lim_ref[:, l0:l0 + col_chunk]

            def step(s, carry, re0=re0, im0=im0, ar=ar, ai=ai):
                hr, hi = carry
                r0 = pl.multiple_of(s * SUBLANES, SUBLANES)
                nr = ar * hr - ai * hi + hbuf[pl.ds(r0, SUBLANES), re0:re0 + col_chunk]
                ni = ar * hi + ai * hr + hbuf[pl.ds(r0, SUBLANES), im0:im0 + col_chunk]
                hbuf[pl.ds(r0, SUBLANES), re0:re0 + col_chunk] = nr
                hbuf[pl.ds(r0, SUBLANES), im0:im0 + col_chunk] = ni
                return nr, ni

            hr, hi = lax.fori_loop(
                0, ts, step,
                (hstate[:, re0:re0 + col_chunk], hstate[:, im0:im0 + col_chunk]), unroll=8)
            hstate[:, re0:re0 + col_chunk] = hr
            hstate[:, im0:im0 + col_chunk] = hi

    hout_ref[...] = hstate[...]

    y = jnp.concatenate(
        [_dot(hbuf[:, c * 2 * half:(c + 1) * 2 * half].astype(BF16), cbd_ref[c]) for c in range(2)], axis=-1)
    y = y + d_ref[...] * u.astype(F32)
    z = jax.nn.gelu(y)
    gate = _dot(z.astype(BF16), wglu_ref[...]) + bglu_ref[...]
    y_ref[...] = (z * jax.nn.sigmoid(gate)).astype(y_ref.dtype)


def _s5_call(u_tb, h0, w, *, ts):
    rows, s5_width = u_tb.shape
    n_state = w['lam_re'].shape[1]
    half = n_state // 2
    width = 2 * n_state
    n_steps = rows // (ts * SUBLANES)
    full = lambda shape: pl.BlockSpec(shape, lambda i: (0,) * len(shape))
    kern = functools.partial(_s5_kernel, ts=ts, n_half=half, col_chunk=512)
    return pl.pallas_call(
        kern,
        grid=(n_steps,),
        in_specs=[
            pl.BlockSpec((ts * SUBLANES, s5_width), lambda i: (i, 0)),
            full((SUBLANES, width)), full((SUBLANES, n_state)), full((SUBLANES, n_state)),
            full((2, s5_width // 2, 2 * half)), full((2, 2 * half, s5_width // 2)),
            full((1, s5_width)), full((s5_width, s5_width)), full((1, s5_width)),
        ],
        out_specs=[
            pl.BlockSpec((ts * SUBLANES, s5_width), lambda i: (i, 0)),
            full((SUBLANES, width)),
        ],
        out_shape=[
            jax.ShapeDtypeStruct((rows, s5_width), BF16),
            jax.ShapeDtypeStruct((SUBLANES, width), F32),
        ],
        scratch_shapes=[
            pltpu.VMEM((ts * SUBLANES, width), F32),
            pltpu.VMEM((SUBLANES, width), F32),
        ],
        compiler_params=pltpu.CompilerParams(
            dimension_semantics=("arbitrary",), vmem_limit_bytes=VMEM_LIMIT),
    )(u_tb, h0, w['lam_re'], w['lam_im'], w['b_bd'], w['c_bd'], w['s5_d'], w['w_glu'], w['b_glu'])


def _out_kernel(x_ref, s5_ref, at_ref, gs5_ref, gat_ref, wout_ref, gffn_ref, wr_ref, br_ref,
                h_ref, route_ref):
    s5n = _rms(s5_ref[...].astype(F32), gs5_ref[...])
    atn = _rms(at_ref[0].astype(F32), gat_ref[...])
    merged = jnp.concatenate([s5n, atn], axis=-1).astype(BF16)
    h = x_ref[0] + _dot(merged, wout_ref[...])
    tm = h.shape[0]
    for j in range(h.shape[1] // LANES):
        h_ref[pl.ds(j, tm, stride=SUBLANES), :] = h[:, j * LANES:(j + 1) * LANES]
    hn = _rms(h, gffn_ref[...]).astype(BF16)

    logits = _dot(hn, wr_ref[...]) + br_ref[...]
    lane_i = _lane_iota(logits.shape)
    lane = lane_i.astype(F32)
    lane_group = (lane_i // EXPERTS_PER_GROUP).astype(F32)
    big = float(LANES)
    is_g = (lane_i >= N_EXPERTS) & (lane_i < N_EXPERTS + N_EXPERT_GROUPS)
    gl = jnp.where(is_g, logits, NEG_INF)
    gmax = jnp.max(gl, axis=-1, keepdims=True)
    g_idx = jnp.min(jnp.where(gl == gmax, lane, big), axis=-1, keepdims=True) - N_EXPERTS
    g_top = 1.0 / jnp.sum(jnp.where(is_g, jnp.exp(gl - gmax), 0.0), axis=-1, keepdims=True)
    in_group = (lane_i < N_EXPERTS) & (lane_group == g_idx)
    el = jnp.where(in_group, logits, NEG_INF)
    v1 = jnp.max(el, axis=-1, keepdims=True)
    i1 = jnp.min(jnp.where(el == v1, lane, big), axis=-1, keepdims=True)
    el2 = jnp.where(lane == i1, NEG_INF, el)
    v2 = jnp.max(el2, axis=-1, keepdims=True)
    i2 = jnp.min(jnp.where(el2 == v2, lane, big), axis=-1, keepdims=True)
    e21 = jnp.exp(v2 - v1)
    w1 = g_top / (1.0 + e21)
    w2 = g_top * e21 / (1.0 + e21)
    route_ref[0] = (jnp.where(lane_i == 0, i1, 0.0) + jnp.where(lane_i == 1, i2, 0.0)
                    + jnp.where(lane_i == 2, w1, 0.0) + jnp.where(lane_i == 3, w2, 0.0))


def _out_call(x, s5_tb, attn, w, *, tm):
    bsz, seq, dm = x.shape
    s5_width = s5_tb.shape[1] // bsz
    aw = attn.shape[2]
    full = lambda shape: pl.BlockSpec(shape, lambda b, i: (0,) * len(shape))
    tok = lambda width: pl.BlockSpec((1, tm, width), lambda b, i: (b, i, 0))
    return pl.pallas_call(
        _out_kernel,
        grid=(bsz, seq // tm),
        in_specs=[
            tok(dm),
            pl.BlockSpec((tm, s5_width), lambda b, i: (i, b)),
            tok(aw),
            full((1, s5_width)), full((1, aw)), full((s5_width + aw, dm)), full((1, dm)),
            full((dm, LANES)), full((1, LANES)),
        ],
        out_specs=[
            pl.BlockSpec((tm * SUBLANES, LANES), lambda b, i: (b * (seq // tm) + i, 0)),
            tok(LANES),
        ],
        out_shape=[
            jax.ShapeDtypeStruct((bsz * seq * SUBLANES, LANES), F32),
            jax.ShapeDtypeStruct((bsz, seq, LANES), F32),
        ],
        compiler_params=pltpu.CompilerParams(
            dimension_semantics=("parallel", "parallel"), vmem_limit_bytes=VMEM_LIMIT),
    )(x, s5_tb, attn, w['g_s5'], w['g_attn'], w['w_out'], w['g_ffn'], w['w_router'], w['b_router'])


def _moe_kernel(offs_ref, toks_ref, gws_ref, h_hbm, gffn_ref, wgu_ref, wd_ref, o_ref, acc, hn_ref, gbuf, ybuf, sem,
                *, d_expert, tb, mc, norm_rows):
    b = pl.program_id(0)
    e = pl.program_id(1)
    tok_rows = SUBLANES
    dm = tok_rows * LANES

    @pl.when(e == 0)
    def _():
        cp = pltpu.make_async_copy(h_hbm.at[pl.ds(b * tb * tok_rows, tb * tok_rows)], acc, sem)
        cp.start()
        gbuf[...] = jnp.zeros_like(gbuf)
        cp.wait()

        def norm(i, carry):
            r0 = pl.multiple_of(i * norm_rows * tok_rows, norm_rows * tok_rows)
            x3 = acc[pl.ds(r0, norm_rows * tok_rows), :].reshape(norm_rows, tok_rows, LANES)
            ms = jnp.sum(jnp.sum(x3 * x3, axis=2, keepdims=True), axis=1, keepdims=True) * (1.0 / dm)
            hn3 = x3 * lax.rsqrt(ms + EPS) * gffn_ref[...][None]
            hn_ref[pl.ds(r0, norm_rows * tok_rows), :] = hn3.reshape(norm_rows * tok_rows, LANES)
            return carry

        lax.fori_loop(0, tb // norm_rows, norm, 0)

    off = offs_ref[0, 0, e]
    cnt = offs_ref[0, 0, e + 1] - off

    def chunk(c, carry):
        base = off + c * mc
        n = jnp.minimum(mc, cnt - c * mc)

        def gather8(g, carry):
            for u in range(SUBLANES):
                i = g * SUBLANES + u
                tok = toks_ref[0, 0, base + jnp.minimum(i, n - 1)]
                gbuf[pl.ds(pl.multiple_of(i * tok_rows, tok_rows), tok_rows), :] = (
                    hn_ref[pl.ds(pl.multiple_of(tok * tok_rows, tok_rows), tok_rows), :])
            return carry

        lax.fori_loop(0, (n + SUBLANES - 1) // SUBLANES, gather8, 0)

        x = jnp.concatenate([gbuf[pl.ds(j, mc, stride=tok_rows), :].astype(BF16) for j in range(tok_rows)],
                            axis=-1)
        gu = _dot(x, wgu_ref[0])
        hdn = jax.nn.silu(gu[:, :d_expert]) * gu[:, d_expert:]
        y = _dot(hdn.astype(BF16), wd_ref[0])
        for j in range(tok_rows):
            ybuf[pl.ds(j, mc, stride=tok_rows), :] = y[:, j * LANES:(j + 1) * LANES]

        def add_rows(first, count):
            idx = [first + u for u in range(count)]
            dst = [pl.multiple_of(toks_ref[0, 0, base + i] * tok_rows, tok_rows) for i in idx]
            val = [acc[pl.ds(d, tok_rows), :]
                   + gws_ref[0, 0, base + i] * ybuf[pl.ds(pl.multiple_of(i * tok_rows, tok_rows), tok_rows), :]
                   for i, d in zip(idx, dst)]
            for d, v in zip(dst, val):
                acc[pl.ds(d, tok_rows), :] = v

        n_full = n // SUBLANES

        def scatter8(g, carry):
            add_rows(g * SUBLANES, SUBLANES)
            return carry

        def scatter1(i, carry):
            add_rows(i, 1)
            return carry

        lax.fori_loop(0, n_full, scatter8, 0)
        lax.fori_loop(n_full * SUBLANES, n, scatter1, 0)
        return carry

    lax.fori_loop(0, (cnt + mc - 1) // mc, chunk, 0)

    @pl.when(e == pl.num_programs(1) - 1)
    def _():
        for j in range(tok_rows):
            o_ref[:, j * LANES:(j + 1) * LANES] = acc[pl.ds(j, tb, stride=tok_rows), :]


def _moe_call(offs, toks, gws, h_tiles, w, *, tb, mc):
    tok_rows = SUBLANES
    dm = tok_rows * LANES
    t = h_tiles.shape[0] // tok_rows
    assert w['w_gu'].shape[1] == dm, "a token row must be exactly one (8,128) f32 tile"
    nb = t // tb
    d_expert = w['w_d'].shape[1]
    kern = functools.partial(_moe_kernel, d_expert=d_expert, tb=tb, mc=mc, norm_rows=min(tb, 256))
    smem = lambda n: pl.BlockSpec((1, 1, n), lambda b, e: (b, 0, 0), memory_space=pltpu.SMEM)
    return pl.pallas_call(
        kern,
        grid=(nb, N_EXPERTS),
        in_specs=[
            smem(LANES), smem(2 * tb), smem(2 * tb),
            pl.BlockSpec(memory_space=pl.ANY),
            pl.BlockSpec((tok_rows, LANES), lambda b, e: (0, 0)),
            pl.BlockSpec((1, dm, 2 * d_expert), lambda b, e: (e, 0, 0)),
            pl.BlockSpec((1, d_expert, dm), lambda b, e: (e, 0, 0)),
        ],
        out_specs=pl.BlockSpec((tb, dm), lambda b, e: (b, 0)),
        out_shape=jax.ShapeDtypeStruct((t, dm), F32),
        scratch_shapes=[
            pltpu.VMEM((tb * tok_rows, LANES), F32),
            pltpu.VMEM((tb * tok_rows, LANES), F32),
            pltpu.VMEM((mc * tok_rows, LANES), F32),
            pltpu.VMEM((mc * tok_rows, LANES), F32),
            pltpu.SemaphoreType.DMA(()),
        ],
        compiler_params=pltpu.CompilerParams(
            dimension_semantics=("parallel", "arbitrary"), vmem_limit_bytes=VMEM_LIMIT),
    )(offs, toks, gws, h_tiles, w['g_ffn'].reshape(tok_rows, LANES), w['w_gu'], w['w_d'])


def _route_tables(route, tb):
    t = route.shape[0]
    nb = t // tb
    ids = route[:, :2].astype(jnp.int32).reshape(nb, 2 * tb)
    wts = route[:, 2:4].reshape(nb, 2 * tb)
    order = jnp.argsort(ids, axis=1, stable=True)
    toks = (order // 2).astype(jnp.int32)
    gws = jnp.take_along_axis(wts, order, axis=1)
    counts = jnp.sum(ids[:, :, None] == jnp.arange(N_EXPERTS, dtype=jnp.int32), axis=1, dtype=jnp.int32)
    offs = jnp.concatenate([jnp.zeros((nb, 1), jnp.int32), jnp.cumsum(counts, axis=1)], axis=1)
    offs = jnp.pad(offs, ((0, 0), (0, LANES - offs.shape[1])))
    return offs[:, None, :], toks[:, None, :], gws[:, None, :]


def _prepare_weights(p):
    dm, in_cols = p['w_in'].shape
    s5_width = p['s5_d'].shape[0]
    q_lora = p['q_lora_norm_g'].shape[0]
    kv_lora = p['kv_lora_norm_g'].shape[0]
    half_r = QK_ROPE // 2
    w = {'s5_width': s5_width, 'q_lora': q_lora, 'kv_lora': kv_lora}

    def partner(a):
        return jnp.concatenate([-a[..., half_r:], a[..., :half_r]], axis=-1)

    def swap(a):
        return jnp.concatenate([a[..., half_r:], a[..., :half_r]], axis=-1)

    w_kr = p['w_in'][:, in_cols - QK_ROPE:]
    w['w_in'] = jnp.concatenate(
        [p['w_in'], partner(w_kr), jnp.zeros((dm, LANES - 2 * QK_ROPE), F32)], axis=-1).astype(BF16)
    w['g_mix'] = p['norm_mix_g'][None]
    w['g_q'] = p['q_lora_norm_g'][None]
    w['g_kv'] = p['kv_lora_norm_g'][None]
    gkr = p['k_rope_norm_g']
    w['g_kr'] = jnp.concatenate([gkr, swap(gkr), jnp.zeros((LANES - 2 * QK_ROPE,), F32)])[None]

    wq = p['w_uq']
    wq_r = wq[..., QK_NOPE:]
    w['w_uq'] = jnp.concatenate([wq, partner(wq_r)], axis=-1).reshape(q_lora, N_HEADS * HEAD_PAD).astype(BF16)
    gr = p['q_rope_norm_g']
    w['q_gain'] = jnp.tile(jnp.concatenate([p['q_nope_norm_g'], gr, swap(gr)]), N_HEADS)[None]
    idx = jnp.arange(HEAD_PAD)
    row_nope = (idx < QK_NOPE)[:, None]
    row_rope = ((idx >= QK_NOPE) & (idx < QK_DIM))[:, None]
    col_nope = (idx < QK_NOPE)[None, :]
    m_q = jnp.where(row_nope & col_nope, 1.0 / QK_NOPE, 0.0) + jnp.where(row_rope & ~col_nope, 1.0 / QK_ROPE, 0.0)
    m_k = jnp.where(row_nope, 1.0 / QK_NOPE, 0.0) * jnp.ones((1, HEAD_PAD), F32)
    eye2 = jnp.eye(2, dtype=F32)
    w['bd_q'] = jnp.kron(eye2, m_q).astype(BF16)
    w['bd_k'] = jnp.kron(eye2, m_k).astype(BF16)

    pad_h = lambda a: jnp.pad(a, ((0, 0), (0, 0), (0, HEAD_PAD - a.shape[-1])))
    w['w_uk'] = pad_h(p['w_uk']).reshape(kv_lora, N_HEADS * HEAD_PAD).astype(BF16)
    w['w_uv'] = pad_h(p['w_uv']).reshape(kv_lora, N_HEADS * HEAD_PAD).astype(BF16)
    w['k_gain'] = jnp.tile(jnp.pad(p['k_nope_norm_g'], (0, HEAD_PAD - QK_NOPE)), N_HEADS)[None]

    dt = jnp.exp(p['s5_log_dt'])[:, None]
    ar, ai = p['s5_a_re'], p['s5_a_im']
    mag = jnp.exp(dt * ar)
    abar_re = mag * jnp.cos(dt * ai)
    abar_im = mag * jnp.sin(dt * ai)
    den = ar * ar + ai * ai
    num_re = abar_re - 1.0
    coef_re = (num_re * ar + abar_im * ai) / den
    coef_im = (abar_im * ar - num_re * ai) / den
    br, bi = p['s5_b_re'], p['s5_b_im']
    bbar_re = coef_re[..., None] * br - coef_im[..., None] * bi
    bbar_im = coef_re[..., None] * bi + coef_im[..., None] * br
    g = ar.shape[0]
    n_state = g * S5_STATE
    half = n_state // 2
    eye_g = jnp.eye(g, dtype=F32)
    b_re = jnp.einsum('gni,gh->gihn', bbar_re, eye_g).reshape(s5_width, n_state)
    b_im = jnp.einsum('gni,gh->gihn', bbar_im, eye_g).reshape(s5_width, n_state)
    c_re = jnp.einsum('gon,gh->gnho', p['s5_c_re'], eye_g).reshape(n_state, s5_width)
    c_im = jnp.einsum('gon,gh->gnho', p['s5_c_im'], eye_g).reshape(n_state, s5_width)
    uw = s5_width // 2
    w['b_bd'] = jnp.stack([
        jnp.concatenate([b_re[c * uw:(c + 1) * uw, c * half:(c + 1) * half],
                         b_im[c * uw:(c + 1) * uw, c * half:(c + 1) * half]], axis=1) for c in range(2)]).astype(BF16)
    w['c_bd'] = jnp.stack([
        jnp.concatenate([c_re[c * half:(c + 1) * half, c * uw:(c + 1) * uw],
                         -c_im[c * half:(c + 1) * half, c * uw:(c + 1) * uw]], axis=0) for c in range(2)]).astype(BF16)
    w['lam_re'] = jnp.broadcast_to(abar_re.reshape(1, n_state), (SUBLANES, n_state))
    w['lam_im'] = jnp.broadcast_to(abar_im.reshape(1, n_state), (SUBLANES, n_state))
    w['s5_d'] = p['s5_d'][None]
    w['w_glu'] = p['s5_w_glu'].astype(BF16)
    w['b_glu'] = p['s5_b_glu'][None]

    w['g_s5'] = p['out_norm_s5_g'][None]
    w['g_attn'] = p['out_norm_attn_g'][None]
    w['w_out'] = p['w_out'].astype(BF16)
    w['g_ffn'] = p['norm_ffn_g'][None]
    n_r = N_EXPERTS + N_EXPERT_GROUPS
    w['w_router'] = jnp.pad(jnp.concatenate([p['w_router_expert'], p['w_router_group']], axis=1),
                            ((0, 0), (0, LANES - n_r))).astype(BF16)
    w['b_router'] = jnp.pad(jnp.concatenate([p['b_router_expert'], p['b_router_group']]), (0, LANES - n_r))[None]
    w['w_gu'] = jnp.concatenate([p['w_e_gate'], p['w_e_up']], axis=-1).astype(BF16)
    w['w_d'] = p['w_e_down'].astype(BF16)
    return w


def _rope_table(pos):
    half = QK_ROPE // 2
    inv = ROPE_THETA ** (-jnp.arange(half, dtype=F32) / half)
    ang = pos.astype(F32)[:, None] * inv[None, :]
    cc = jnp.tile(jnp.cos(ang), (1, 2))
    ss = jnp.tile(jnp.sin(ang), (1, 2))
    n = pos.shape[0]
    scale = QK_DIM ** -0.5 * math.log2(math.e)
    tab_q = scale * jnp.concatenate([jnp.ones((n, QK_NOPE), F32), cc, ss], axis=-1)
    tab_k = jnp.concatenate([cc, ss, jnp.zeros((n, LANES - 2 * QK_ROPE), F32)], axis=-1)
    return jnp.concatenate([tab_q, tab_k], axis=-1)


def _s5_state_in(re, im):
    b = re.shape[0]
    re = re.reshape(b, -1)
    im = im.reshape(b, -1)
    half = re.shape[1] // 2
    return jnp.concatenate([re[:, :half], im[:, :half], re[:, half:], im[:, half:]], axis=1)


def _s5_state_out(h, g):
    b = h.shape[0]
    half = h.shape[1] // 4
    re = jnp.concatenate([h[:, :half], h[:, 2 * half:3 * half]], axis=1).reshape(b, g, S5_STATE)
    im = jnp.concatenate([h[:, half:2 * half], h[:, 3 * half:]], axis=1).reshape(b, g, S5_STATE)
    return re, im


def _trunk_layer(x, past_ckv, past_krope, h0_re, h0_im, w):
    bsz, seq, dm = x.shape
    assert bsz == SUBLANES, "the S5 scan keeps the batch on the sublanes"
    past_len = 0 if past_ckv is None else past_ckv.shape[1]
    g = w['lam_re'].shape[1] // S5_STATE
    tm = _tile(seq, 512)

    tab = _rope_table(past_len + jnp.arange(seq, dtype=jnp.int32))
    u_tb, q, new_ckv, new_krope = _proj_call(x, tab, w, tm=tm)

    if h0_re is None:
        h0 = jnp.zeros((bsz, 2 * g * S5_STATE), F32)
    else:
        h0 = _s5_state_in(h0_re.astype(F32), h0_im.astype(F32))
    s5_tb, h_last = _s5_call(u_tb.reshape(seq * bsz, -1), h0, w, ts=min(seq, 64))
    h_re, h_im = _s5_state_out(h_last, g)

    if past_ckv is None:
        all_ckv, all_krope = new_ckv, new_krope
    else:
        all_ckv = jnp.concatenate([past_ckv.astype(F32), new_ckv], axis=1)
        all_krope = jnp.concatenate([past_krope.astype(F32), new_krope], axis=1)
    sk = all_ckv.shape[1]
    sk_pad = -(-sk // LANES) * LANES
    all_ckv = jnp.pad(all_ckv, ((0, 0), (0, sk_pad - sk), (0, 0)))
    krp = jnp.pad(all_krope, ((0, 0), (0, sk_pad - sk), (QK_NOPE, LANES - QK_DIM)))
    whole = sk_pad % ATTN_TK != 0
    k, vt = _kv_call(all_ckv, krp, w, tm=sk_pad if whole else KV_TILE)
    sq_pad = -(-seq // LANES) * LANES
    q = jnp.pad(q, ((0, 0), (0, 0), (0, sq_pad - seq), (0, 0)))
    tq = ATTN_TQ if sq_pad % ATTN_TQ == 0 else LANES
    attn = _attn_call(q, k, vt, tq=tq, tk=sk_pad if whole else ATTN_TK, q_pos0=past_len, sk_valid=sk)[:, :seq]

    h_tiles, route = _out_call(x, s5_tb.reshape(seq, -1), attn, w, tm=tm)
    t = bsz * seq
    tb = _tile(t, MOE_BLOCK)
    offs, toks, gws = _route_tables(route.reshape(t, LANES), tb)
    y = _moe_call(offs, toks, gws, h_tiles, w, tb=tb, mc=MOE_CHUNK)
    return y.reshape(bsz, seq, dm), new_ckv, new_krope, h_re, h_im


def kernel(x_prompt, x_sample, cache_ckv, cache_krope, state_s5_re, state_s5_im, norm_mix_g, w_in, s5_a_re, s5_a_im, s5_log_dt, s5_b_re, s5_b_im, s5_c_re, s5_c_im, s5_d, s5_w_glu, s5_b_glu, q_lora_norm_g, w_uq, kv_lora_norm_g, w_uk, w_uv, q_nope_norm_g, q_rope_norm_g, k_nope_norm_g, k_rope_norm_g, out_norm_s5_g, out_norm_attn_g, w_out, norm_ffn_g, w_router_group, b_router_group, w_router_expert, b_router_expert, w_e_gate, w_e_up, w_e_down):
    params = dict(
        norm_mix_g=norm_mix_g, w_in=w_in, s5_a_re=s5_a_re, s5_a_im=s5_a_im, s5_log_dt=s5_log_dt,
        s5_b_re=s5_b_re, s5_b_im=s5_b_im, s5_c_re=s5_c_re, s5_c_im=s5_c_im, s5_d=s5_d,
        s5_w_glu=s5_w_glu, s5_b_glu=s5_b_glu, q_lora_norm_g=q_lora_norm_g, w_uq=w_uq,
        kv_lora_norm_g=kv_lora_norm_g, w_uk=w_uk, w_uv=w_uv, q_nope_norm_g=q_nope_norm_g,
        q_rope_norm_g=q_rope_norm_g, k_nope_norm_g=k_nope_norm_g, k_rope_norm_g=k_rope_norm_g,
        out_norm_s5_g=out_norm_s5_g, out_norm_attn_g=out_norm_attn_g, w_out=w_out, norm_ffn_g=norm_ffn_g,
        w_router_group=w_router_group, b_router_group=b_router_group,
        w_router_expert=w_router_expert, b_router_expert=b_router_expert,
        w_e_gate=w_e_gate, w_e_up=w_e_up, w_e_down=w_e_down)
    depth = w_in.shape[0]
    y_p, y_s = x_prompt, x_sample
    outs = [[] for _ in range(8)]
    for l in range(depth):
        w = _prepare_weights({k: a[l] for k, a in params.items()})
        y_p, c1, k1, r1, i1 = _trunk_layer(y_p, None, None, None, None, w)
        y_s, c2, k2, r2, i2 = _trunk_layer(y_s, cache_ckv[l], cache_krope[l], state_s5_re[l], state_s5_im[l], w)
        for lst, a in zip(outs, (c1, k1, r1, i1, c2, k2, r2, i2)):
            lst.append(a)
    return (y_p, y_s) + tuple(jnp.stack(lst) for lst in outs)
```

```python
import functools
import math

import jax
import jax.numpy as jnp
from jax import lax
from jax.experimental import pallas as pl
from jax.experimental.pallas import tpu as pltpu

F32 = jnp.float32
BF16 = jnp.bfloat16

CHUNK = 64
S5_GROUP = 16
S5_STATE = 64
N_HEADS = 8
QK_NOPE = 64
QK_ROPE = 32
QK_DIM = QK_NOPE + QK_ROPE
V_DIM = 64
ROPE_THETA = 10000.0
N_EXPERT_GROUPS = 4
EXPERTS_PER_GROUP = 8
N_EXPERTS = N_EXPERT_GROUPS * EXPERTS_PER_GROUP
EPS = 1e-6
NEG_INF = -1e30

LANES = 128
SUBLANES = 8
HEAD_PAD = LANES
ATTN_TQ = 512
ATTN_TK = 512
KV_TILE = 512
MOE_BLOCK = 2048
MOE_CHUNK = 256
REF_MARGIN = 1.02
REF_LIMIT = 40.0
ATTN_COLS = 256
VT_ROWS = 80
VMEM_LIMIT = 48 * 1024 * 1024


def _tile(n, cap):
    for t in range(min(cap, n), 15, -1):
        if n % t == 0 and t % 16 == 0:
            return t
    return n


def _rms(x, g):
    return x * lax.rsqrt(jnp.mean(x * x, axis=-1, keepdims=True) + EPS) * g


def _dot(a, b):
    return jnp.dot(a, b, preferred_element_type=F32)


def _lane_iota(shape):
    return lax.broadcasted_iota(jnp.int32, shape, len(shape) - 1)


def _proj_kernel(x_ref, gmix_ref, win_ref, gq_ref, wuq_ref, bdq_ref, qgain_ref, gkv_ref, gkr_ref, tab_ref,
                 u_ref, q_ref, ckv_ref, kr_ref, *, s5_width, q_lora, kv_lora):
    x = x_ref[0]
    xn = _rms(x, gmix_ref[...])
    proj = _dot(xn.astype(BF16), win_ref[...])
    u_ref[...] = proj[:, :s5_width].astype(BF16)

    o = s5_width
    cqn = _rms(proj[:, o:o + q_lora], gq_ref[...])
    o += q_lora
    ckv_ref[0] = _rms(proj[:, o:o + kv_lora], gkv_ref[...])
    o += kv_lora

    tab = tab_ref[...]
    lane = _lane_iota((x.shape[0], LANES))

    kr = proj[:, o:o + LANES]
    ms = jnp.sum(jnp.where(lane < QK_ROPE, kr * kr, 0.0), axis=-1, keepdims=True) * (1.0 / QK_ROPE)
    t = kr * lax.rsqrt(ms + EPS) * gkr_ref[...] * tab[:, LANES:]
    kro = t + pltpu.roll(t, LANES - QK_ROPE, axis=1)
    kr_ref[0] = kro[:, :QK_ROPE]

    qa = _dot(cqn.astype(BF16), wuq_ref[...])
    qsq = (qa * qa).astype(BF16)
    pair = 2 * HEAD_PAD
    ms = jnp.concatenate([_dot(qsq[:, c * pair:(c + 1) * pair], bdq_ref[...])
                          for c in range(N_HEADS // 2)], axis=-1)
    qn = qa * lax.rsqrt(ms + EPS) * qgain_ref[...]
    is_rope = (lane >= QK_NOPE) & (lane < QK_DIM)
    for h in range(N_HEADS):
        th = qn[:, h * HEAD_PAD:(h + 1) * HEAD_PAD] * tab[:, :LANES]
        rolled = pltpu.roll(th, LANES - QK_ROPE, axis=1)
        oh = jnp.where(lane < QK_DIM, th + jnp.where(is_rope, rolled, 0.0), 0.0)
        q_ref[0, h] = oh.astype(BF16)


def _proj_call(x, tab, w, *, tm):
    bsz, seq, dm = x.shape
    s5_width, q_lora, kv_lora = w['s5_width'], w['q_lora'], w['kv_lora']
    in_cols = w['w_in'].shape[1]
    hp = N_HEADS * HEAD_PAD
    full = lambda shape: pl.BlockSpec(shape, lambda b, i: (0,) * len(shape))
    kern = functools.partial(_proj_kernel, s5_width=s5_width, q_lora=q_lora, kv_lora=kv_lora)
    return pl.pallas_call(
        kern,
        grid=(bsz, seq // tm),
        in_specs=[
            pl.BlockSpec((1, tm, dm), lambda b, i: (b, i, 0)),
            full((1, dm)), full((dm, in_cols)), full((1, q_lora)), full((q_lora, hp)),
            full((2 * HEAD_PAD, 2 * HEAD_PAD)), full((1, hp)), full((1, kv_lora)), full((1, LANES)),
            pl.BlockSpec((tm, 2 * LANES), lambda b, i: (i, 0)),
        ],
        out_specs=[
            pl.BlockSpec((tm, s5_width), lambda b, i: (i, b)),
            pl.BlockSpec((1, N_HEADS, tm, HEAD_PAD), lambda b, i: (b, 0, i, 0)),
            pl.BlockSpec((1, tm, kv_lora), lambda b, i: (b, i, 0)),
            pl.BlockSpec((1, tm, QK_ROPE), lambda b, i: (b, i, 0)),
        ],
        out_shape=[
            jax.ShapeDtypeStruct((seq, bsz * s5_width), BF16),
            jax.ShapeDtypeStruct((bsz, N_HEADS, seq, HEAD_PAD), BF16),
            jax.ShapeDtypeStruct((bsz, seq, kv_lora), F32),
            jax.ShapeDtypeStruct((bsz, seq, QK_ROPE), F32),
        ],
        compiler_params=pltpu.CompilerParams(
            dimension_semantics=("parallel", "parallel"), vmem_limit_bytes=VMEM_LIMIT),
    )(x, w['g_mix'], w['w_in'], w['g_q'], w['w_uq'], w['bd_q'], w['q_gain'], w['g_kv'], w['g_kr'], tab)


def _kv_kernel(ckv_ref, krp_ref, wuk_ref, wuv_ref, bdk_ref, kg_ref, k_ref, v_ref):
    c = ckv_ref[0].astype(BF16)
    ka = _dot(c, wuk_ref[...])
    va = _dot(c, wuv_ref[...])
    ksq = (ka * ka).astype(BF16)
    pair = 2 * HEAD_PAD
    ms = jnp.concatenate([_dot(ksq[:, c2 * pair:(c2 + 1) * pair], bdk_ref[...])
                          for c2 in range(N_HEADS // 2)], axis=-1)
    kn = ka * lax.rsqrt(ms + EPS) * kg_ref[...]
    krp = krp_ref[0]
    lane = _lane_iota(krp.shape)
    ones_col = jnp.where(lane == V_DIM, 1.0, 0.0)
    for h in range(N_HEADS):
        k_ref[0, h] = (kn[:, h * HEAD_PAD:(h + 1) * HEAD_PAD] + krp).astype(BF16)
        vt = (va[:, h * HEAD_PAD:(h + 1) * HEAD_PAD] + ones_col).T
        v_ref[0, h] = vt[:VT_ROWS].astype(BF16)


def _kv_call(ckv, krp, w, *, tm):
    bsz, sk, kv_lora = ckv.shape
    hp = N_HEADS * HEAD_PAD
    full = lambda shape: pl.BlockSpec(shape, lambda b, i: (0,) * len(shape))
    return pl.pallas_call(
        _kv_kernel,
        grid=(bsz, sk // tm),
        in_specs=[
            pl.BlockSpec((1, tm, kv_lora), lambda b, i: (b, i, 0)),
            pl.BlockSpec((1, tm, LANES), lambda b, i: (b, i, 0)),
            full((kv_lora, hp)), full((kv_lora, hp)), full((2 * HEAD_PAD, 2 * HEAD_PAD)), full((1, hp)),
        ],
        out_specs=[
            pl.BlockSpec((1, N_HEADS, tm, HEAD_PAD), lambda b, i: (b, 0, i, 0)),
            pl.BlockSpec((1, N_HEADS, VT_ROWS, tm), lambda b, i: (b, 0, 0, i)),
        ],
        out_shape=[
            jax.ShapeDtypeStruct((bsz, N_HEADS, sk, HEAD_PAD), BF16),
            jax.ShapeDtypeStruct((bsz, N_HEADS, VT_ROWS, sk), BF16),
        ],
        compiler_params=pltpu.CompilerParams(
            dimension_semantics=("parallel", "parallel"), vmem_limit_bytes=VMEM_LIMIT),
    )(ckv, krp, w['w_uk'], w['w_uv'], w['bd_k'], w['k_gain'])


def _attn_kernel(q_ref, k_ref, vt_ref, o_ref, *, tq, tk, q_pos0, sk_valid):
    sq = q_ref.shape[2]
    nq = sq // tq

    n_heads = q_ref.shape[1]

    cw = min(tq, ATTN_COLS)
    streams = [(hh, c) for hh in range(n_heads) for c in range(tq // cw)]
    sk = k_ref.shape[2]
    nt = (((1,), (1,)), ((), ()))
    ones_rows = jnp.ones((SUBLANES, HEAD_PAD), BF16)

    def sq_norms(x):
        xf = x.astype(F32)
        return lax.dot_general(ones_rows, (xf * xf).astype(BF16), nt, preferred_element_type=F32)[:1]

    def kmax_body(j, carry):
        k0 = pl.multiple_of(j * tk, tk)
        return tuple(jnp.maximum(carry[hh], sq_norms(k_ref[0, hh, pl.ds(k0, tk), :])) for hh in range(n_heads))

    kmax2 = lax.fori_loop(0, sk // tk, kmax_body, tuple(jnp.zeros((1, tk), F32) for _ in range(n_heads)))
    kmax2 = [jnp.max(v, axis=1, keepdims=True) for v in kmax2]

    def scores(qs, p0, j, masked):
        k0 = pl.multiple_of(j * tk, tk)
        kbs = [k_ref[0, hh, pl.ds(k0, tk), :] for hh in range(n_heads)]
        ss = [lax.dot_general(kbs[hh], qs[i], nt, preferred_element_type=F32)
              for i, (hh, c) in enumerate(streams)]
        if masked:
            k_pos = k0 + lax.broadcasted_iota(jnp.int32, (tk, 1), 0)
            for c in range(tq // cw):
                q_pos = p0 + c * cw + lax.broadcasted_iota(jnp.int32, (1, cw), 1)
                visible = k_pos < jnp.minimum((q_pos // CHUNK + 1) * CHUNK, sk_valid)
                for i, (hh, c2) in enumerate(streams):
                    if c2 == c:
                        ss[i] = jnp.where(visible, ss[i], NEG_INF)
        return ss

    def values(j):
        k0 = pl.multiple_of(j * tk, tk)
        return [vt_ref[0, hh, :, pl.ds(k0, tk)] for hh in range(n_heads)]

    def step_running_max(qs, p0, j, carry, masked):
        ss, vts = scores(qs, p0, j, masked), values(j)
        new = []
        for i, (hh, c) in enumerate(streams):
            m, acc = carry[i]
            m_new = jnp.maximum(m, jnp.max(ss[i], axis=0, keepdims=True))
            acc = jnp.exp2(m - m_new) * acc + _dot(vts[hh], jnp.exp2(ss[i] - m_new).astype(BF16))
            new.append((m_new, acc))
        return tuple(new)

    def step_fixed_ref(qs, refs, p0, j, carry, masked):
        ss, vts = scores(qs, p0, j, masked), values(j)
        return tuple(carry[i] + _dot(vts[hh], jnp.exp2(ss[i] - refs[i]).astype(BF16))
                     for i, (hh, c) in enumerate(streams))

    def q_body(qi, carry):
        q0 = pl.multiple_of(qi * tq, tq)
        qs = [q_ref[0, hh, pl.ds(q0 + c * cw, cw), :] for hh, c in streams]
        p0 = q_pos0 + q0
        n_open = jnp.minimum((p0 // CHUNK + 1) * CHUNK, sk_valid) // tk
        v1 = jnp.minimum(((p0 + tq - 1) // CHUNK + 1) * CHUNK, sk_valid)
        n_all = (v1 + tk - 1) // tk
        refs = [jnp.sqrt(sq_norms(qs[i]) * kmax2[hh]) * REF_MARGIN for i, (hh, c) in enumerate(streams)]
        ref_max = functools.reduce(jnp.maximum, [jnp.max(r) for r in refs])

        def fixed_ref():
            st = tuple(jnp.zeros((VT_ROWS, cw), F32) for _ in streams)
            st = lax.fori_loop(0, n_open, lambda j, c: step_fixed_ref(qs, refs, p0, j, c, False), st)
            return lax.fori_loop(n_open, n_all, lambda j, c: step_fixed_ref(qs, refs, p0, j, c, True), st)

        def running_max():
            st = tuple((jnp.full((1, cw), NEG_INF, F32), jnp.zeros((VT_ROWS, cw), F32)) for _ in streams)
            st = lax.fori_loop(0, n_open, lambda j, c: step_running_max(qs, p0, j, c, False), st)
            st = lax.fori_loop(n_open, n_all, lambda j, c: step_running_max(qs, p0, j, c, True), st)
            return tuple(acc for _, acc in st)

        accs = lax.cond(ref_max < REF_LIMIT, fixed_ref, running_max)
        for c in range(tq // cw):
            ot = jnp.concatenate([accs[i][:V_DIM] / accs[i][V_DIM:V_DIM + 1]
                                  for i, (hh, c2) in enumerate(streams) if c2 == c], axis=0)
            o_ref[0, pl.ds(q0 + c * cw, cw), :] = ot.T.astype(o_ref.dtype)
        return carry

    lax.fori_loop(0, nq, q_body, 0)


def _attn_call(q, k, vt, *, tq, tk, q_pos0, sk_valid):
    bsz, nh, sq, _ = q.shape
    sk = k.shape[2]
    assert sq % tq == 0 and sk % tk == 0 and tq % LANES == 0 and tk % LANES == 0
    kern = functools.partial(_attn_kernel, tq=tq, tk=tk, q_pos0=q_pos0, sk_valid=sk_valid)
    return pl.pallas_call(
        kern,
        grid=(bsz, nh // 2),
        in_specs=[
            pl.BlockSpec((1, 2, sq, HEAD_PAD), lambda b, h: (b, h, 0, 0)),
            pl.BlockSpec((1, 2, sk, HEAD_PAD), lambda b, h: (b, h, 0, 0)),
            pl.BlockSpec((1, 2, VT_ROWS, sk), lambda b, h: (b, h, 0, 0)),
        ],
        out_specs=pl.BlockSpec((1, sq, 2 * V_DIM), lambda b, h: (b, 0, h)),
        out_shape=jax.ShapeDtypeStruct((bsz, sq, nh * V_DIM), BF16),
        compiler_params=pltpu.CompilerParams(
            dimension_semantics=("parallel", "parallel"), vmem_limit_bytes=VMEM_LIMIT),
    )(q, k, vt)


def _s5_kernel(u_ref, h0_ref, lre_ref, lim_ref, bbd_ref, cbd_ref, d_ref, wglu_ref, bglu_ref,
               y_ref, hout_ref, hbuf, hstate, *, ts, n_half, col_chunk):
    i = pl.program_id(0)
    half = n_half
    width = hbuf.shape[1]
    uw = u_ref.shape[1] // 2

    @pl.when(i == 0)
    def _():
        hstate[...] = h0_ref[...]

    u = u_ref[...]
    for c in range(2):
        hbuf[:, c * 2 * half:(c + 1) * 2 * half] = _dot(u[:, c * uw:(c + 1) * uw], bbd_ref[c])

    for c in range(2):
        for cc in range(half // col_chunk):
            re0 = c * 2 * half + cc * col_chunk
            im0 = re0 + half
            l0 = c * half + cc * col_chunk
            ar = lre_ref[:, l0:l0 + col_chunk]
            ai = lim_ref[:, l0:l0 + col_chunk]

            def step(s, carry, re0=re0, im0=im0, ar=ar, ai=ai):
                hr, hi = carry
                r0 = pl.multiple_of(s * SUBLANES, SUBLANES)
                nr = ar * hr - ai * hi + hbuf[pl.ds(r0, SUBLANES), re0:re0 + col_chunk]
                ni = ar * hi + ai * hr + hbuf[pl.ds(r0, SUBLANES), im0:im0 + col_chunk]
                hbuf[pl.ds(r0, SUBLANES), re0:re0 + col_chunk] = nr
                hbuf[pl.ds(r0, SUBLANES), im0:im0 + col_chunk] = ni
                return nr, ni

            hr, hi = lax.fori_loop(
                0, ts, step,
                (hstate[:, re0:re0 + col_chunk], hstate[:, im0:im0 + col_chunk]), unroll=8)
            hstate[:, re0:re0 + col_chunk] = hr
            hstate[:, im0:im0 + col_chunk] = hi

    hout_ref[...] = hstate[...]

    y = jnp.concatenate(
        [_dot(hbuf[:, c * 2 * half:(c + 1) * 2 * half].astype(BF16), cbd_ref[c]) for c in range(2)], axis=-1)
    y = y + d_ref[...] * u.astype(F32)
    z = jax.nn.gelu(y)
    gate = _dot(z.astype(BF16), wglu_ref[...]) + bglu_ref[...]
    y_ref[...] = (z * jax.nn.sigmoid(gate)).astype(y_ref.dtype)


def _s5_call(u_tb, h0, w, *, ts):
    rows, s5_width = u_tb.shape
    n_state = w['lam_re'].shape[1]
    half = n_state // 2
    width = 2 * n_state
    n_steps = rows // (ts * SUBLANES)
    full = lambda shape: pl.BlockSpec(shape, lambda i: (0,) * len(shape))
    kern = functools.partial(_s5_kernel, ts=ts, n_half=half, col_chunk=512)
    return pl.pallas_call(
        kern,
        grid=(n_steps,),
        in_specs=[
            pl.BlockSpec((ts * SUBLANES, s5_width), lambda i: (i, 0)),
            full((SUBLANES, width)), full((SUBLANES, n_state)), full((SUBLANES, n_state)),
            full((2, s5_width // 2, 2 * half)), full((2, 2 * half, s5_width // 2)),
            full((1, s5_width)), full((s5_width, s5_width)), full((1, s5_width)),
        ],
        out_specs=[
            pl.BlockSpec((ts * SUBLANES, s5_width), lambda i: (i, 0)),
            full((SUBLANES, width)),
        ],
        out_shape=[
            jax.ShapeDtypeStruct((rows, s5_width), BF16),
            jax.ShapeDtypeStruct((SUBLANES, width), F32),
        ],
        scratch_shapes=[
            pltpu.VMEM((ts * SUBLANES, width), F32),
            pltpu.VMEM((SUBLANES, width), F32),
        ],
        compiler_params=pltpu.CompilerParams(
            dimension_semantics=("arbitrary",), vmem_limit_bytes=VMEM_LIMIT),
    )(u_tb, h0, w['lam_re'], w['lam_im'], w['b_bd'], w['c_bd'], w['s5_d'], w['w_glu'], w['b_glu'])


def _out_kernel(x_ref, s5_ref, at_ref, gs5_ref, gat_ref, wout_ref, gffn_ref, wr_ref, br_ref,
                h_ref, route_ref):
    s5n = _rms(s5_ref[...].astype(F32), gs5_ref[...])
    atn = _rms(at_ref[0].astype(F32), gat_ref[...])
    merged = jnp.concatenate([s5n, atn], axis=-1).astype(BF16)
    h = x_ref[0] + _dot(merged, wout_ref[...])
    tm = h.shape[0]
    for j in range(h.shape[1] // LANES):
        h_ref[pl.ds(j, tm, stride=SUBLANES), :] = h[:, j * LANES:(j + 1) * LANES]
    hn = _rms(h, gffn_ref[...]).astype(BF16)

    logits = _dot(hn, wr_ref[...]) + br_ref[...]
    lane_i = _lane_iota(logits.shape)
    lane = lane_i.astype(F32)
    lane_group = (lane_i // EXPERTS_PER_GROUP).astype(F32)
    big = float(LANES)
    is_g = (lane_i >= N_EXPERTS) & (lane_i < N_EXPERTS + N_EXPERT_GROUPS)
    gl = jnp.where(is_g, logits, NEG_INF)
    gmax = jnp.max(gl, axis=-1, keepdims=True)
    g_idx = jnp.min(jnp.where(gl == gmax, lane, big), axis=-1, keepdims=True) - N_EXPERTS
    g_top = 1.0 / jnp.sum(jnp.where(is_g, jnp.exp(gl - gmax), 0.0), axis=-1, keepdims=True)
    in_group = (lane_i < N_EXPERTS) & (lane_group == g_idx)
    el = jnp.where(in_group, logits, NEG_INF)
    v1 = jnp.max(el, axis=-1, keepdims=True)
    i1 = jnp.min(jnp.where(el == v1, lane, big), axis=-1, keepdims=True)
    el2 = jnp.where(lane == i1, NEG_INF, el)
    v2 = jnp.max(el2, axis=-1, keepdims=True)
    i2 = jnp.min(jnp.where(el2 == v2, lane, big), axis=-1, keepdims=True)
    e21 = jnp.exp(v2 - v1)
    w1 = g_top / (1.0 + e21)
    w2 = g_top * e21 / (1.0 + e21)
    route_ref[0] = (jnp.where(lane_i == 0, i1, 0.0) + jnp.where(lane_i == 1, i2, 0.0)
                    + jnp.where(lane_i == 2, w1, 0.0) + jnp.where(lane_i == 3, w2, 0.0))


def _out_call(x, s5_tb, attn, w, *, tm):
    bsz, seq, dm = x.shape
    s5_width = s5_tb.shape[1] // bsz
    aw = attn.shape[2]
    full = lambda shape: pl.BlockSpec(shape, lambda b, i: (0,) * len(shape))
    tok = lambda width: pl.BlockSpec((1, tm, width), lambda b, i: (b, i, 0))
    return pl.pallas_call(
        _out_kernel,
        grid=(bsz, seq // tm),
        in_specs=[
            tok(dm),
            pl.BlockSpec((tm, s5_width), lambda b, i: (i, b)),
            tok(aw),
            full((1, s5_width)), full((1, aw)), full((s5_width + aw, dm)), full((1, dm)),
            full((dm, LANES)), full((1, LANES)),
        ],
        out_specs=[
            pl.BlockSpec((tm * SUBLANES, LANES), lambda b, i: (b * (seq // tm) + i, 0)),
            tok(LANES),
        ],
        out_shape=[
            jax.ShapeDtypeStruct((bsz * seq * SUBLANES, LANES), F32),
            jax.ShapeDtypeStruct((bsz, seq, LANES), F32),
        ],
        compiler_params=pltpu.CompilerParams(
            dimension_semantics=("parallel", "parallel"), vmem_limit_bytes=VMEM_LIMIT),
    )(x, s5_tb, attn, w['g_s5'], w['g_attn'], w['w_out'], w['g_ffn'], w['w_router'], w['b_router'])


def _moe_kernel(offs_ref, toks_ref, gws_ref, h_hbm, gffn_ref, wgu_ref, wd_ref, o_ref, acc, hn_ref, gbuf, ybuf, sem,
                *, d_expert, tb, mc, norm_rows):
    b = pl.program_id(0)
    e = pl.program_id(1)
    tok_rows = SUBLANES
    dm = tok_rows * LANES

    @pl.when(e == 0)
    def _():
        cp = pltpu.make_async_copy(h_hbm.at[pl.ds(b * tb * tok_rows, tb * tok_rows)], acc, sem)
        cp.start()
        gbuf[...] = jnp.zeros_like(gbuf)
        cp.wait()

        def norm(i, carry):
            r0 = pl.multiple_of(i * norm_rows * tok_rows, norm_rows * tok_rows)
            x3 = acc[pl.ds(r0, norm_rows * tok_rows), :].reshape(norm_rows, tok_rows, LANES)
            ms = jnp.sum(jnp.sum(x3 * x3, axis=2, keepdims=True), axis=1, keepdims=True) * (1.0 / dm)
            hn3 = x3 * lax.rsqrt(ms + EPS) * gffn_ref[...][None]
            hn_ref[pl.ds(r0, norm_rows * tok_rows), :] = hn3.reshape(norm_rows * tok_rows, LANES)
            return carry

        lax.fori_loop(0, tb // norm_rows, norm, 0)

    off = offs_ref[0, 0, e]
    cnt = offs_ref[0, 0, e + 1] - off

    def chunk(c, carry):
        base = off + c * mc
        n = jnp.minimum(mc, cnt - c * mc)

        def gather8(g, carry):
            for u in range(SUBLANES):
                i = g * SUBLANES + u
                tok = toks_ref[0, 0, base + jnp.minimum(i, n - 1)]
                gbuf[pl.ds(pl.multiple_of(i * tok_rows, tok_rows), tok_rows), :] = (
                    hn_ref[pl.ds(pl.multiple_of(tok * tok_rows, tok_rows), tok_rows), :])
            return carry

        lax.fori_loop(0, (n + SUBLANES - 1) // SUBLANES, gather8, 0)

        x = jnp.concatenate([gbuf[pl.ds(j, mc, stride=tok_rows), :].astype(BF16) for j in range(tok_rows)],
                            axis=-1)
        gu = _dot(x, wgu_ref[0])
        hdn = jax.nn.silu(gu[:, :d_expert]) * gu[:, d_expert:]
        y = _dot(hdn.astype(BF16), wd_ref[0])
        for j in range(tok_rows):
            ybuf[pl.ds(j, mc, stride=tok_rows), :] = y[:, j * LANES:(j + 1) * LANES]

        def add_rows(first, count):
            idx = [first + u for u in range(count)]
            dst = [pl.multiple_of(toks_ref[0, 0, base + i] * tok_rows, tok_rows) for i in idx]
            val = [acc[pl.ds(d, tok_rows), :]
                   + gws_ref[0, 0, base + i] * ybuf[pl.ds(pl.multiple_of(i * tok_rows, tok_rows), tok_rows), :]
                   for i, d in zip(idx, dst)]
            for d, v in zip(dst, val):
                acc[pl.ds(d, tok_rows), :] = v

        n_full = n // SUBLANES

        def scatter8(g, carry):
            add_rows(g * SUBLANES, SUBLANES)
            return carry

        def scatter1(i, carry):
            add_rows(i, 1)
            return carry

        lax.fori_loop(0, n_full, scatter8, 0)
        lax.fori_loop(n_full * SUBLANES, n, scatter1, 0)
        return carry

    lax.fori_loop(0, (cnt + mc - 1) // mc, chunk, 0)

    @pl.when(e == pl.num_programs(1) - 1)
    def _():
        for j in range(tok_rows):
            o_ref[:, j * LANES:(j + 1) * LANES] = acc[pl.ds(j, tb, stride=tok_rows), :]


def _moe_call(offs, toks, gws, h_tiles, w, *, tb, mc):
    tok_rows = SUBLANES
    dm = tok_rows * LANES
    t = h_tiles.shape[0] // tok_rows
    assert w['w_gu'].shape[1] == dm, "a token row must be exactly one (8,128) f32 tile"
    nb = t // tb
    d_expert = w['w_d'].shape[1]
    kern = functools.partial(_moe_kernel, d_expert=d_expert, tb=tb, mc=mc, norm_rows=min(tb, 256))
    smem = lambda n: pl.BlockSpec((1, 1, n), lambda b, e: (b, 0, 0), memory_space=pltpu.SMEM)
    return pl.pallas_call(
        kern,
        grid=(nb, N_EXPERTS),
        in_specs=[
            smem(LANES), smem(2 * tb), smem(2 * tb),
            pl.BlockSpec(memory_space=pl.ANY),
            pl.BlockSpec((tok_rows, LANES), lambda b, e: (0, 0)),
            pl.BlockSpec((1, dm, 2 * d_expert), lambda b, e: (e, 0, 0)),
            pl.BlockSpec((1, d_expert, dm), lambda b, e: (e, 0, 0)),
        ],
        out_specs=pl.BlockSpec((tb, dm), lambda b, e: (b, 0)),
        out_shape=jax.ShapeDtypeStruct((t, dm), F32),
        scratch_shapes=[
            pltpu.VMEM((tb * tok_rows, LANES), F32),
            pltpu.VMEM((tb * tok_rows, LANES), F32),
            pltpu.VMEM((mc * tok_rows, LANES), F32),
            pltpu.VMEM((mc * tok_rows, LANES), F32),
            pltpu.SemaphoreType.DMA(()),
        ],
        compiler_params=pltpu.CompilerParams(
            dimension_semantics=("parallel", "arbitrary"), vmem_limit_bytes=VMEM_LIMIT),
    )(offs, toks, gws, h_tiles, w['g_ffn'].reshape(tok_rows, LANES), w['w_gu'], w['w_d'])


def _route_tables(route, tb):
    t = route.shape[0]
    nb = t // tb
    ids = route[:, :2].astype(jnp.int32).reshape(nb, 2 * tb)
    wts = route[:, 2:4].reshape(nb, 2 * tb)
    order = jnp.argsort(ids, axis=1, stable=True)
    toks = (order // 2).astype(jnp.int32)
    gws = jnp.take_along_axis(wts, order, axis=1)
    counts = jnp.sum(ids[:, :, None] == jnp.arange(N_EXPERTS, dtype=jnp.int32), axis=1, dtype=jnp.int32)
    offs = jnp.concatenate([jnp.zeros((nb, 1), jnp.int32), jnp.cumsum(counts, axis=1)], axis=1)
    offs = jnp.pad(offs, ((0, 0), (0, LANES - offs.shape[1])))
    return offs[:, None, :], toks[:, None, :], gws[:, None, :]


def _prepare_weights(p):
    dm, in_cols = p['w_in'].shape
    s5_width = p['s5_d'].shape[0]
    q_lora = p['q_lora_norm_g'].shape[0]
    kv_lora = p['kv_lora_norm_g'].shape[0]
    half_r = QK_ROPE // 2
    w = {'s5_width': s5_width, 'q_lora': q_lora, 'kv_lora': kv_lora}

    def partner(a):
        return jnp.concatenate([-a[..., half_r:], a[..., :half_r]], axis=-1)

    def swap(a):
        return jnp.concatenate([a[..., half_r:], a[..., :half_r]], axis=-1)

    w_kr = p['w_in'][:, in_cols - QK_ROPE:]
    w['w_in'] = jnp.concatenate(
        [p['w_in'], partner(w_kr), jnp.zeros((dm, LANES - 2 * QK_ROPE), F32)], axis=-1).astype(BF16)
    w['g_mix'] = p['norm_mix_g'][None]
    w['g_q'] = p['q_lora_norm_g'][None]
    w['g_kv'] = p['kv_lora_norm_g'][None]
    gkr = p['k_rope_norm_g']
    w['g_kr'] = jnp.concatenate([gkr, swap(gkr), jnp.zeros((LANES - 2 * QK_ROPE,), F32)])[None]

    wq = p['w_uq']
    wq_r = wq[..., QK_NOPE:]
    w['w_uq'] = jnp.concatenate([wq, partner(wq_r)], axis=-1).reshape(q_lora, N_HEADS * HEAD_PAD).astype(BF16)
    gr = p['q_rope_norm_g']
    w['q_gain'] = jnp.tile(jnp.concatenate([p['q_nope_norm_g'], gr, swap(gr)]), N_HEADS)[None]
    idx = jnp.arange(HEAD_PAD)
    row_nope = (idx < QK_NOPE)[:, None]
    row_rope = ((idx >= QK_NOPE) & (idx < QK_DIM))[:, None]
    col_nope = (idx < QK_NOPE)[None, :]
    m_q = jnp.where(row_nope & col_nope, 1.0 / QK_NOPE, 0.0) + jnp.where(row_rope & ~col_nope, 1.0 / QK_ROPE, 0.0)
    m_k = jnp.where(row_nope, 1.0 / QK_NOPE, 0.0) * jnp.ones((1, HEAD_PAD), F32)
    eye2 = jnp.eye(2, dtype=F32)
    w['bd_q'] = jnp.kron(eye2, m_q).astype(BF16)
    w['bd_k'] = jnp.kron(eye2, m_k).astype(BF16)

    pad_h = lambda a: jnp.pad(a, ((0, 0), (0, 0), (0, HEAD_PAD - a.shape[-1])))
    w['w_uk'] = pad_h(p['w_uk']).reshape(kv_lora, N_HEADS * HEAD_PAD).astype(BF16)
    w['w_uv'] = pad_h(p['w_uv']).reshape(kv_lora, N_HEADS * HEAD_PAD).astype(BF16)
    w['k_gain'] = jnp.tile(jnp.pad(p['k_nope_norm_g'], (0, HEAD_PAD - QK_NOPE)), N_HEADS)[None]

    dt = jnp.exp(p['s5_log_dt'])[:, None]
    ar, ai = p['s5_a_re'], p['s5_a_im']
    mag = jnp.exp(dt * ar)
    abar_re = mag * jnp.cos(dt * ai)
    abar_im = mag * jnp.sin(dt * ai)
    den = ar * ar + ai * ai
    num_re = abar_re - 1.0
    coef_re = (num_re * ar + abar_im * ai) / den
    coef_im = (abar_im * ar - num_re * ai) / den
    br, bi = p['s5_b_re'], p['s5_b_im']
    bbar_re = coef_re[..., None] * br - coef_im[..., None] * bi
    bbar_im = coef_re[..., None] * bi + coef_im[..., None] * br
    g = ar.shape[0]
    n_state = g * S5_STATE
    half = n_state // 2
    eye_g = jnp.eye(g, dtype=F32)
    b_re = jnp.einsum('gni,gh->gihn', bbar_re, eye_g).reshape(s5_width, n_state)
    b_im = jnp.einsum('gni,gh->gihn', bbar_im, eye_g).reshape(s5_width, n_state)
    c_re = jnp.einsum('gon,gh->gnho', p['s5_c_re'], eye_g).reshape(n_state, s5_width)
    c_im = jnp.einsum('gon,gh->gnho', p['s5_c_im'], eye_g).reshape(n_state, s5_width)
    uw = s5_width // 2
    w['b_bd'] = jnp.stack([
        jnp.concatenate([b_re[c * uw:(c + 1) * uw, c * half:(c + 1) * half],
                         b_im[c * uw:(c + 1) * uw, c * half:(c + 1) * half]], axis=1) for c in range(2)]).astype(BF16)
    w['c_bd'] = jnp.stack([
        jnp.concatenate([c_re[c * half:(c + 1) * half, c * uw:(c + 1) * uw],
                         -c_im[c * half:(c + 1) * half, c * uw:(c + 1) * uw]], axis=0) for c in range(2)]).astype(BF16)
    w['lam_re'] = jnp.broadcast_to(abar_re.reshape(1, n_state), (SUBLANES, n_state))
    w['lam_im'] = jnp.broadcast_to(abar_im.reshape(1, n_state), (SUBLANES, n_state))
    w['s5_d'] = p['s5_d'][None]
    w['w_glu'] = p['s5_w_glu'].astype(BF16)
    w['b_glu'] = p['s5_b_glu'][None]

    w['g_s5'] = p['out_norm_s5_g'][None]
    w['g_attn'] = p['out_norm_attn_g'][None]
    w['w_out'] = p['w_out'].astype(BF16)
    w['g_ffn'] = p['norm_ffn_g'][None]
    n_r = N_EXPERTS + N_EXPERT_GROUPS
    w['w_router'] = jnp.pad(jnp.concatenate([p['w_router_expert'], p['w_router_group']], axis=1),
                            ((0, 0), (0, LANES - n_r))).astype(BF16)
    w['b_router'] = jnp.pad(jnp.concatenate([p['b_router_expert'], p['b_router_group']]), (0, LANES - n_r))[None]
    w['w_gu'] = jnp.concatenate([p['w_e_gate'], p['w_e_up']], axis=-1).astype(BF16)
    w['w_d'] = p['w_e_down'].astype(BF16)
    return w


def _rope_table(pos):
    half = QK_ROPE // 2
    inv = ROPE_THETA ** (-jnp.arange(half, dtype=F32) / half)
    ang = pos.astype(F32)[:, None] * inv[None, :]
    cc = jnp.tile(jnp.cos(ang), (1, 2))
    ss = jnp.tile(jnp.sin(ang), (1, 2))
    n = pos.shape[0]
    scale = QK_DIM ** -0.5 * math.log2(math.e)
    tab_q = scale * jnp.concatenate([jnp.ones((n, QK_NOPE), F32), cc, ss], axis=-1)
    tab_k = jnp.concatenate([cc, ss, jnp.zeros((n, LANES - 2 * QK_ROPE), F32)], axis=-1)
    return jnp.concatenate([tab_q, tab_k], axis=-1)


def _s5_state_in(re, im):
    b = re.shape[0]
    re = re.reshape(b, -1)
    im = im.reshape(b, -1)
    half = re.shape[1] // 2
    return jnp.concatenate([re[:, :half], im[:, :half], re[:, half:], im[:, half:]], axis=1)


def _s5_state_out(h, g):
    b = h.shape[0]
    half = h.shape[1] // 4
    re = jnp.concatenate([h[:, :half], h[:, 2 * half:3 * half]], axis=1).reshape(b, g, S5_STATE)
    im = jnp.concatenate([h[:, half:2 * half], h[:, 3 * half:]], axis=1).reshape(b, g, S5_STATE)
    return re, im


def _trunk_layer(x, past_ckv, past_krope, h0_re, h0_im, w):
    bsz, seq, dm = x.shape
    assert bsz == SUBLANES, "the S5 scan keeps the batch on the sublanes"
    past_len = 0 if past_ckv is None else past_ckv.shape[1]
    g = w['lam_re'].shape[1] // S5_STATE
    tm = _tile(seq, 512)

    tab = _rope_table(past_len + jnp.arange(seq, dtype=jnp.int32))
    u_tb, q, new_ckv, new_krope = _proj_call(x, tab, w, tm=tm)

    if h0_re is None:
        h0 = jnp.zeros((bsz, 2 * g * S5_STATE), F32)
    else:
        h0 = _s5_state_in(h0_re.astype(F32), h0_im.astype(F32))
    s5_tb, h_last = _s5_call(u_tb.reshape(seq * bsz, -1), h0, w, ts=min(seq, 64))
    h_re, h_im = _s5_state_out(h_last, g)

    if past_ckv is None:
        all_ckv, all_krope = new_ckv, new_krope
    else:
        all_ckv = jnp.concatenate([past_ckv.astype(F32), new_ckv], axis=1)
        all_krope = jnp.concatenate([past_krope.astype(F32), new_krope], axis=1)
    sk = all_ckv.shape[1]
    sk_pad = -(-sk // LANES) * LANES
    all_ckv = jnp.pad(all_ckv, ((0, 0), (0, sk_pad - sk), (0, 0)))
    krp = jnp.pad(all_krope, ((0, 0), (0, sk_pad - sk), (QK_NOPE, LANES - QK_DIM)))
    whole = sk_pad % ATTN_TK != 0
    k, vt = _kv_call(all_ckv, krp, w, tm=sk_pad if whole else KV_TILE)
    sq_pad = -(-seq // LANES) * LANES
    q = jnp.pad(q, ((0, 0), (0, 0), (0, sq_pad - seq), (0, 0)))
    tq = ATTN_TQ if sq_pad % ATTN_TQ == 0 else LANES
    attn = _attn_call(q, k, vt, tq=tq, tk=sk_pad if whole else ATTN_TK, q_pos0=past_len, sk_valid=sk)[:, :seq]

    h_tiles, route = _out_call(x, s5_tb.reshape(seq, -1), attn, w, tm=tm)
    t = bsz * seq
    tb = _tile(t, MOE_BLOCK)
    offs, toks, gws = _route_tables(route.reshape(t, LANES), tb)
    y = _moe_call(offs, toks, gws, h_tiles, w, tb=tb, mc=MOE_CHUNK)
    return y.reshape(bsz, seq, dm), new_ckv, new_krope, h_re, h_im


def kernel(x_prompt, x_sample, cache_ckv, cache_krope, state_s5_re, state_s5_im, norm_mix_g, w_in, s5_a_re, s5_a_im, s5_log_dt, s5_b_re, s5_b_im, s5_c_re, s5_c_im, s5_d, s5_w_glu, s5_b_glu, q_lora_norm_g, w_uq, kv_lora_norm_g, w_uk, w_uv, q_nope_norm_g, q_rope_norm_g, k_nope_norm_g, k_rope_norm_g, out_norm_s5_g, out_norm_attn_g, w_out, norm_ffn_g, w_router_group, b_router_group, w_router_expert, b_router_expert, w_e_gate, w_e_up, w_e_down):
    params = dict(
        norm_mix_g=norm_mix_g, w_in=w_in, s5_a_re=s5_a_re, s5_a_im=s5_a_im, s5_log_dt=s5_log_dt,
        s5_b_re=s5_b_re, s5_b_im=s5_b_im, s5_c_re=s5_c_re, s5_c_im=s5_c_im, s5_d=s5_d,
        s5_w_glu=s5_w_glu, s5_b_glu=s5_b_glu, q_lora_norm_g=q_lora_norm_g, w_uq=w_uq,
        kv_lora_norm_g=kv_lora_norm_g, w_uk=w_uk, w_uv=w_uv, q_nope_norm_g=q_nope_norm_g,
        q_rope_norm_g=q_rope_norm_g, k_nope_norm_g=k_nope_norm_g, k_rope_norm_g=k_rope_norm_g,
        out_norm_s5_g=out_norm_s5_g, out_norm_attn_g=out_norm_attn_g, w_out=w_out, norm_ffn_g=norm_ffn_g,
        w_router_group=w_router_group, b_router_group=b_router_group,
        w_router_expert=w_router_expert, b_router_expert=b_router_expert,
        w_e_gate=w_e_gate, w_e_up=w_e_up, w_e_down=w_e_down)
    depth = w_in.shape[0]
    y_p, y_s = x_prompt, x_sample
    outs = [[] for _ in range(8)]
    for l in range(depth):
        w = _prepare_weights({k: a[l] for k, a in params.items()})
        y_p, c1, k1, r1, i1 = _trunk_layer(y_p, None, None, None, None, w)
        y_s, c2, k2, r2, i2 = _trunk_layer(y_s, cache_ckv[l], cache_krope[l], state_s5_re[l], state_s5_im[l], w)
        for lst, a in zip(outs, (c1, k1, r1, i1, c2, k2, r2, i2)):
            lst.append(a)
    return (y_p, y_s) + tuple(jnp.stack(lst) for lst in outs)
```

```python
import functools
import math

import jax
import jax.numpy as jnp
from jax import lax
from jax.experimental import pallas as pl
from jax.experimental.pallas import tpu as pltpu

F32 = jnp.float32
BF16 = jnp.bfloat16

CHUNK = 64
S5_GROUP = 16
S5_STATE = 64
N_HEADS = 8
QK_NOPE = 64
QK_ROPE = 32
QK_DIM = QK_NOPE + QK_ROPE
V_DIM = 64
ROPE_THETA = 10000.0
N_EXPERT_GROUPS = 4
EXPERTS_PER_GROUP = 8
N_EXPERTS = N_EXPERT_GROUPS * EXPERTS_PER_GROUP
EPS = 1e-6
NEG_INF = -1e30

LANES = 128
SUBLANES = 8
HEAD_PAD = LANES
ATTN_TQ = 512
ATTN_TK = 512
KV_TILE = 512
MOE_BLOCK = 2048
MOE_CHUNK = 160
MOE_EXPERTS_PER_STEP = 2
REF_MARGIN = 1.02
REF_LIMIT = 40.0
ATTN_COLS = 256
VT_ROWS = 80
VMEM_LIMIT = 48 * 1024 * 1024


def _tile(n, cap):
    for t in range(min(cap, n), 15, -1):
        if n % t == 0 and t % 16 == 0:
            return t
    return n


def _rms(x, g):
    return x * lax.rsqrt(jnp.mean(x * x, axis=-1, keepdims=True) + EPS) * g


def _dot(a, b):
    return jnp.dot(a, b, preferred_element_type=F32)


def _lane_iota(shape):
    return lax.broadcasted_iota(jnp.int32, shape, len(shape) - 1)


def _proj_kernel(x_ref, gmix_ref, win_ref, gq_ref, wuq_ref, bdq_ref, qgain_ref, gkv_ref, gkr_ref, tab_ref,
                 u_ref, q_ref, ckv_ref, kr_ref, *, s5_width, q_lora, kv_lora):
    x = x_ref[0]
    xn = _rms(x, gmix_ref[...])
    proj = _dot(xn.astype(BF16), win_ref[...])
    u_ref[...] = proj[:, :s5_width].astype(BF16)

    o = s5_width
    cqn = _rms(proj[:, o:o + q_lora], gq_ref[...])
    o += q_lora
    ckv_ref[0] = _rms(proj[:, o:o + kv_lora], gkv_ref[...])
    o += kv_lora

    tab = tab_ref[...]
    lane = _lane_iota((x.shape[0], LANES))

    kr = proj[:, o:o + LANES]
    ms = jnp.sum(jnp.where(lane < QK_ROPE, kr * kr, 0.0), axis=-1, keepdims=True) * (1.0 / QK_ROPE)
    t = kr * lax.rsqrt(ms + EPS) * gkr_ref[...] * tab[:, LANES:]
    kro = t + pltpu.roll(t, LANES - QK_ROPE, axis=1)
    kr_ref[0] = kro[:, :QK_ROPE]

    qa = _dot(cqn.astype(BF16), wuq_ref[...])
    qsq = (qa * qa).astype(BF16)
    pair = 2 * HEAD_PAD
    ms = jnp.concatenate([_dot(qsq[:, c * pair:(c + 1) * pair], bdq_ref[...])
                          for c in range(N_HEADS // 2)], axis=-1)
    qn = qa * lax.rsqrt(ms + EPS) * qgain_ref[...]
    is_rope = (lane >= QK_NOPE) & (lane < QK_DIM)
    for h in range(N_HEADS):
        th = qn[:, h * HEAD_PAD:(h + 1) * HEAD_PAD] * tab[:, :LANES]
        rolled = pltpu.roll(th, LANES - QK_ROPE, axis=1)
        oh = jnp.where(lane < QK_DIM, th + jnp.where(is_rope, rolled, 0.0), 0.0)
        q_ref[0, h] = oh.astype(BF16)


def _proj_call(x, tab, w, *, tm):
    bsz, seq, dm = x.shape
    s5_width, q_lora, kv_lora = w['s5_width'], w['q_lora'], w['kv_lora']
    in_cols = w['w_in'].shape[1]
    hp = N_HEADS * HEAD_PAD
    full = lambda shape: pl.BlockSpec(shape, lambda b, i: (0,) * len(shape))
    kern = functools.partial(_proj_kernel, s5_width=s5_width, q_lora=q_lora, kv_lora=kv_lora)
    return pl.pallas_call(
        kern,
        grid=(bsz, seq // tm),
        in_specs=[
            pl.BlockSpec((1, tm, dm), lambda b, i: (b, i, 0)),
            full((1, dm)), full((dm, in_cols)), full((1, q_lora)), full((q_lora, hp)),
            full((2 * HEAD_PAD, 2 * HEAD_PAD)), full((1, hp)), full((1, kv_lora)), full((1, LANES)),
            pl.BlockSpec((tm, 2 * LANES), lambda b, i: (i, 0)),
        ],
        out_specs=[
            pl.BlockSpec((tm, s5_width), lambda b, i: (i, b)),
            pl.BlockSpec((1, N_HEADS, tm, HEAD_PAD), lambda b, i: (b, 0, i, 0)),
            pl.BlockSpec((1, tm, kv_lora), lambda b, i: (b, i, 0)),
            pl.BlockSpec((1, tm, QK_ROPE), lambda b, i: (b, i, 0)),
        ],
        out_shape=[
            jax.ShapeDtypeStruct((seq, bsz * s5_width), BF16),
            jax.ShapeDtypeStruct((bsz, N_HEADS, seq, HEAD_PAD), BF16),
            jax.ShapeDtypeStruct((bsz, seq, kv_lora), F32),
            jax.ShapeDtypeStruct((bsz, seq, QK_ROPE), F32),
        ],
        compiler_params=pltpu.CompilerParams(
            dimension_semantics=("parallel", "parallel"), vmem_limit_bytes=VMEM_LIMIT),
    )(x, w['g_mix'], w['w_in'], w['g_q'], w['w_uq'], w['bd_q'], w['q_gain'], w['g_kv'], w['g_kr'], tab)


def _kv_kernel(ckv_ref, krp_ref, wuk_ref, wuv_ref, bdk_ref, kg_ref, k_ref, v_ref):
    c = ckv_ref[0].astype(BF16)
    ka = _dot(c, wuk_ref[...])
    va = _dot(c, wuv_ref[...])
    ksq = (ka * ka).astype(BF16)
    pair = 2 * HEAD_PAD
    ms = jnp.concatenate([_dot(ksq[:, c2 * pair:(c2 + 1) * pair], bdk_ref[...])
                          for c2 in range(N_HEADS // 2)], axis=-1)
    kn = ka * lax.rsqrt(ms + EPS) * kg_ref[...]
    krp = krp_ref[0]
    lane = _lane_iota(krp.shape)
    ones_col = jnp.where(lane == V_DIM, 1.0, 0.0)
    for h in range(N_HEADS):
        k_ref[0, h] = (kn[:, h * HEAD_PAD:(h + 1) * HEAD_PAD] + krp).astype(BF16)
        vt = (va[:, h * HEAD_PAD:(h + 1) * HEAD_PAD] + ones_col).T
        v_ref[0, h] = vt[:VT_ROWS].astype(BF16)


def _kv_call(ckv, krp, w, *, tm):
    bsz, sk, kv_lora = ckv.shape
    hp = N_HEADS * HEAD_PAD
    full = lambda shape: pl.BlockSpec(shape, lambda b, i: (0,) * len(shape))
    return pl.pallas_call(
        _kv_kernel,
        grid=(bsz, sk // tm),
        in_specs=[
            pl.BlockSpec((1, tm, kv_lora), lambda b, i: (b, i, 0)),
            pl.BlockSpec((1, tm, LANES), lambda b, i: (b, i, 0)),
            full((kv_lora, hp)), full((kv_lora, hp)), full((2 * HEAD_PAD, 2 * HEAD_PAD)), full((1, hp)),
        ],
        out_specs=[
            pl.BlockSpec((1, N_HEADS, tm, HEAD_PAD), lambda b, i: (b, 0, i, 0)),
            pl.BlockSpec((1, N_HEADS, VT_ROWS, tm), lambda b, i: (b, 0, 0, i)),
        ],
        out_shape=[
            jax.ShapeDtypeStruct((bsz, N_HEADS, sk, HEAD_PAD), BF16),
            jax.ShapeDtypeStruct((bsz, N_HEADS, VT_ROWS, sk), BF16),
        ],
        compiler_params=pltpu.CompilerParams(
            dimension_semantics=("parallel", "parallel"), vmem_limit_bytes=VMEM_LIMIT),
    )(ckv, krp, w['w_uk'], w['w_uv'], w['bd_k'], w['k_gain'])


def _attn_kernel(q_ref, k_ref, vt_ref, o_ref, *, tq, tk, q_pos0, sk_valid):
    sq = q_ref.shape[2]
    nq = sq // tq

    n_heads = q_ref.shape[1]

    cw = min(tq, ATTN_COLS)
    streams = [(hh, c) for hh in range(n_heads) for c in range(tq // cw)]
    sk = k_ref.shape[2]
    nt = (((1,), (1,)), ((), ()))
    ones_rows = jnp.ones((SUBLANES, HEAD_PAD), BF16)

    def sq_norms(x):
        xf = x.astype(F32)
        return lax.dot_general(ones_rows, (xf * xf).astype(BF16), nt, preferred_element_type=F32)[:1]

    def kmax_body(j, carry):
        k0 = pl.multiple_of(j * tk, tk)
        return tuple(jnp.maximum(carry[hh], sq_norms(k_ref[0, hh, pl.ds(k0, tk), :])) for hh in range(n_heads))

    kmax2 = lax.fori_loop(0, sk // tk, kmax_body, tuple(jnp.zeros((1, tk), F32) for _ in range(n_heads)))
    kmax2 = [jnp.max(v, axis=1, keepdims=True) for v in kmax2]

    def scores(qs, p0, j, masked):
        k0 = pl.multiple_of(j * tk, tk)
        kbs = [k_ref[0, hh, pl.ds(k0, tk), :] for hh in range(n_heads)]
        ss = [lax.dot_general(kbs[hh], qs[i], nt, preferred_element_type=F32)
              for i, (hh, c) in enumerate(streams)]
        if masked:
            k_pos = k0 + lax.broadcasted_iota(jnp.int32, (tk, 1), 0)
            for c in range(tq // cw):
                q_pos = p0 + c * cw + lax.broadcasted_iota(jnp.int32, (1, cw), 1)
                visible = k_pos < jnp.minimum((q_pos // CHUNK + 1) * CHUNK, sk_valid)
                for i, (hh, c2) in enumerate(streams):
                    if c2 == c:
                        ss[i] = jnp.where(visible, ss[i], NEG_INF)
        return ss

    def values(j):
        k0 = pl.multiple_of(j * tk, tk)
        return [vt_ref[0, hh, :, pl.ds(k0, tk)] for hh in range(n_heads)]

    def step_running_max(qs, p0, j, carry, masked):
        ss, vts = scores(qs, p0, j, masked), values(j)
        new = []
        for i, (hh, c) in enumerate(streams):
            m, acc = carry[i]
            m_new = jnp.maximum(m, jnp.max(ss[i], axis=0, keepdims=True))
            acc = jnp.exp2(m - m_new) * acc + _dot(vts[hh], jnp.exp2(ss[i] - m_new).astype(BF16))
            new.append((m_new, acc))
        return tuple(new)

    def step_fixed_ref(qs, refs, p0, j, carry, masked):
        ss, vts = scores(qs, p0, j, masked), values(j)
        return tuple(carry[i] + _dot(vts[hh], jnp.exp2(ss[i] - refs[i]).astype(BF16))
                     for i, (hh, c) in enumerate(streams))

    def q_body(qi, carry):
        q0 = pl.multiple_of(qi * tq, tq)
        qs = [q_ref[0, hh, pl.ds(q0 + c * cw, cw), :] for hh, c in streams]
        p0 = q_pos0 + q0
        n_open = jnp.minimum((p0 // CHUNK + 1) * CHUNK, sk_valid) // tk
        v1 = jnp.minimum(((p0 + tq - 1) // CHUNK + 1) * CHUNK, sk_valid)
        n_all = (v1 + tk - 1) // tk
        refs = [jnp.sqrt(sq_norms(qs[i]) * kmax2[hh]) * REF_MARGIN for i, (hh, c) in enumerate(streams)]
        ref_max = functools.reduce(jnp.maximum, [jnp.max(r) for r in refs])

        def fixed_ref():
            st = tuple(jnp.zeros((VT_ROWS, cw), F32) for _ in streams)
            st = lax.fori_loop(0, n_open, lambda j, c: step_fixed_ref(qs, refs, p0, j, c, False), st)
            return lax.fori_loop(n_open, n_all, lambda j, c: step_fixed_ref(qs, refs, p0, j, c, True), st)

        def running_max():
            st = tuple((jnp.full((1, cw), NEG_INF, F32), jnp.zeros((VT_ROWS, cw), F32)) for _ in streams)
            st = lax.fori_loop(0, n_open, lambda j, c: step_running_max(qs, p0, j, c, False), st)
            st = lax.fori_loop(n_open, n_all, lambda j, c: step_running_max(qs, p0, j, c, True), st)
            return tuple(acc for _, acc in st)

        accs = lax.cond(ref_max < REF_LIMIT, fixed_ref, running_max)
        for c in range(tq // cw):
            ot = jnp.concatenate([accs[i][:V_DIM] / accs[i][V_DIM:V_DIM + 1]
                                  for i, (hh, c2) in enumerate(streams) if c2 == c], axis=0)
            o_ref[0, pl.ds(q0 + c * cw, cw), :] = ot.T.astype(o_ref.dtype)
        return carry

    lax.fori_loop(0, nq, q_body, 0)


def _attn_call(q, k, vt, *, tq, tk, q_pos0, sk_valid):
    bsz, nh, sq, _ = q.shape
    sk = k.shape[2]
    assert sq % tq == 0 and sk % tk == 0 and tq % LANES == 0 and tk % LANES == 0
    kern = functools.partial(_attn_kernel, tq=tq, tk=tk, q_pos0=q_pos0, sk_valid=sk_valid)
    return pl.pallas_call(
        kern,
        grid=(bsz, nh // 2),
        in_specs=[
            pl.BlockSpec((1, 2, sq, HEAD_PAD), lambda b, h: (b, h, 0, 0)),
            pl.BlockSpec((1, 2, sk, HEAD_PAD), lambda b, h: (b, h, 0, 0)),
            pl.BlockSpec((1, 2, VT_ROWS, sk), lambda b, h: (b, h, 0, 0)),
        ],
        out_specs=pl.BlockSpec((1, sq, 2 * V_DIM), lambda b, h: (b, 0, h)),
        out_shape=jax.ShapeDtypeStruct((bsz, sq, nh * V_DIM), BF16),
        compiler_params=pltpu.CompilerParams(
            dimension_semantics=("parallel", "parallel"), vmem_limit_bytes=VMEM_LIMIT),
    )(q, k, vt)


def _s5_kernel(u_ref, h0_ref, lre_ref, lim_ref, bbd_ref, cbd_ref, d_ref, wglu_ref, bglu_ref,
               y_ref, hout_ref, hbuf, hstate, *, ts, n_half, col_chunk):
    i = pl.program_id(0)
    half = n_half
    width = hbuf.shape[1]
    uw = u_ref.shape[1] // 2

    @pl.when(i == 0)
    def _():
        hstate[...] = h0_ref[...]

    u = u_ref[...]
    for c in range(2):
        hbuf[:, c * 2 * half:(c + 1) * 2 * half] = _dot(u[:, c * uw:(c + 1) * uw], bbd_ref[c])

    for c in range(2):
        for cc in range(half // col_chunk):
            re0 = c * 2 * half + cc * col_chunk
            im0 = re0 + half
            l0 = c * half + cc * col_chunk
            ar = lre_ref[:, l0:l0 + col_chunk]
            ai = lim_ref[:, l0:l0 + col_chunk]

            def step(s, carry, re0=re0, im0=im0, ar=ar, ai=ai):
                hr, hi = carry
                r0 = pl.multiple_of(s * SUBLANES, SUBLANES)
                nr = ar * hr - ai * hi + hbuf[pl.ds(r0, SUBLANES), re0:re0 + col_chunk]
                ni = ar * hi + ai * hr + hbuf[pl.ds(r0, SUBLANES), im0:im0 + col_chunk]
                hbuf[pl.ds(r0, SUBLANES), re0:re0 + col_chunk] = nr
                hbuf[pl.ds(r0, SUBLANES), im0:im0 + col_chunk] = ni
                return nr, ni

            hr, hi = lax.fori_loop(
                0, ts, step,
                (hstate[:, re0:re0 + col_chunk], hstate[:, im0:im0 + col_chunk]), unroll=8)
            hstate[:, re0:re0 + col_chunk] = hr
            hstate[:, im0:im0 + col_chunk] = hi

    hout_ref[...] = hstate[...]

    y = jnp.concatenate(
        [_dot(hbuf[:, c * 2 * half:(c + 1) * 2 * half].astype(BF16), cbd_ref[c]) for c in range(2)], axis=-1)
    y = y + d_ref[...] * u.astype(F32)
    z = jax.nn.gelu(y)
    gate = _dot(z.astype(BF16), wglu_ref[...]) + bglu_ref[...]
    y_ref[...] = (z * jax.nn.sigmoid(gate)).astype(y_ref.dtype)


def _s5_call(u_tb, h0, w, *, ts):
    rows, s5_width = u_tb.shape
    n_state = w['lam_re'].shape[1]
    half = n_state // 2
    width = 2 * n_state
    n_steps = rows // (ts * SUBLANES)
    full = lambda shape: pl.BlockSpec(shape, lambda i: (0,) * len(shape))
    kern = functools.partial(_s5_kernel, ts=ts, n_half=half, col_chunk=512)
    return pl.pallas_call(
        kern,
        grid=(n_steps,),
        in_specs=[
            pl.BlockSpec((ts * SUBLANES, s5_width), lambda i: (i, 0)),
            full((SUBLANES, width)), full((SUBLANES, n_state)), full((SUBLANES, n_state)),
            full((2, s5_width // 2, 2 * half)), full((2, 2 * half, s5_width // 2)),
            full((1, s5_width)), full((s5_width, s5_width)), full((1, s5_width)),
        ],
        out_specs=[
            pl.BlockSpec((ts * SUBLANES, s5_width), lambda i: (i, 0)),
            full((SUBLANES, width)),
        ],
        out_shape=[
            jax.ShapeDtypeStruct((rows, s5_width), BF16),
            jax.ShapeDtypeStruct((SUBLANES, width), F32),
        ],
        scratch_shapes=[
            pltpu.VMEM((ts * SUBLANES, width), F32),
            pltpu.VMEM((SUBLANES, width), F32),
        ],
        compiler_params=pltpu.CompilerParams(
            dimension_semantics=("arbitrary",), vmem_limit_bytes=VMEM_LIMIT),
    )(u_tb, h0, w['lam_re'], w['lam_im'], w['b_bd'], w['c_bd'], w['s5_d'], w['w_glu'], w['b_glu'])


def _out_kernel(x_ref, s5_ref, at_ref, gs5_ref, gat_ref, wout_ref, gffn_ref, wr_ref, br_ref,
                h_ref, route_ref):
    s5n = _rms(s5_ref[...].astype(F32), gs5_ref[...])
    atn = _rms(at_ref[0].astype(F32), gat_ref[...])
    merged = jnp.concatenate([s5n, atn], axis=-1).astype(BF16)
    h = x_ref[0] + _dot(merged, wout_ref[...])
    tm = h.shape[0]
    for j in range(h.shape[1] // LANES):
        h_ref[pl.ds(j, tm, stride=SUBLANES), :] = h[:, j * LANES:(j + 1) * LANES]
    hn = _rms(h, gffn_ref[...]).astype(BF16)

    logits = _dot(hn, wr_ref[...]) + br_ref[...]
    lane_i = _lane_iota(logits.shape)
    lane = lane_i.astype(F32)
    lane_group = (lane_i // EXPERTS_PER_GROUP).astype(F32)
    big = float(LANES)
    is_g = (lane_i >= N_EXPERTS) & (lane_i < N_EXPERTS + N_EXPERT_GROUPS)
    gl = jnp.where(is_g, logits, NEG_INF)
    gmax = jnp.max(gl, axis=-1, keepdims=True)
    g_idx = jnp.min(jnp.where(gl == gmax, lane, big), axis=-1, keepdims=True) - N_EXPERTS
    g_top = 1.0 / jnp.sum(jnp.where(is_g, jnp.exp(gl - gmax), 0.0), axis=-1, keepdims=True)
    in_group = (lane_i < N_EXPERTS) & (lane_group == g_idx)
    el = jnp.where(in_group, logits, NEG_INF)
    v1 = jnp.max(el, axis=-1, keepdims=True)
    i1 = jnp.min(jnp.where(el == v1, lane, big), axis=-1, keepdims=True)
    el2 = jnp.where(lane == i1, NEG_INF, el)
    v2 = jnp.max(el2, axis=-1, keepdims=True)
    i2 = jnp.min(jnp.where(el2 == v2, lane, big), axis=-1, keepdims=True)
    e21 = jnp.exp(v2 - v1)
    w1 = g_top / (1.0 + e21)
    w2 = g_top * e21 / (1.0 + e21)
    route_ref[0] = (jnp.where(lane_i == 0, i1, 0.0) + jnp.where(lane_i == 1, i2, 0.0)
                    + jnp.where(lane_i == 2, w1, 0.0) + jnp.where(lane_i == 3, w2, 0.0))


def _out_call(x, s5_tb, attn, w, *, tm):
    bsz, seq, dm = x.shape
    s5_width = s5_tb.shape[1] // bsz
    aw = attn.shape[2]
    full = lambda shape: pl.BlockSpec(shape, lambda b, i: (0,) * len(shape))
    tok = lambda width: pl.BlockSpec((1, tm, width), lambda b, i: (b, i, 0))
    return pl.pallas_call(
        _out_kernel,
        grid=(bsz, seq // tm),
        in_specs=[
            tok(dm),
            pl.BlockSpec((tm, s5_width), lambda b, i: (i, b)),
            tok(aw),
            full((1, s5_width)), full((1, aw)), full((s5_width + aw, dm)), full((1, dm)),
            full((dm, LANES)), full((1, LANES)),
        ],
        out_specs=[
            pl.BlockSpec((tm * SUBLANES, LANES), lambda b, i: (b * (seq // tm) + i, 0)),
            tok(LANES),
        ],
        out_shape=[
            jax.ShapeDtypeStruct((bsz * seq * SUBLANES, LANES), F32),
            jax.ShapeDtypeStruct((bsz, seq, LANES), F32),
        ],
        compiler_params=pltpu.CompilerParams(
            dimension_semantics=("parallel", "parallel"), vmem_limit_bytes=VMEM_LIMIT),
    )(x, s5_tb, attn, w['g_s5'], w['g_attn'], w['w_out'], w['g_ffn'], w['w_router'], w['b_router'])


def _moe_kernel(offs_ref, toks_ref, gws_ref, h_hbm, gffn_ref, wgu_ref, wd_ref, o_ref, acc, hn_ref, gbuf, ybuf, sem,
                *, d_expert, tb, mc, norm_rows):
    b = pl.program_id(0)
    e = pl.program_id(1)
    tok_rows = SUBLANES
    dm = tok_rows * LANES

    @pl.when(e == 0)
    def _():
        cp = pltpu.make_async_copy(h_hbm.at[pl.ds(b * tb * tok_rows, tb * tok_rows)], acc, sem)
        cp.start()
        gbuf[...] = jnp.zeros_like(gbuf)
        cp.wait()

        def norm(i, carry):
            r0 = pl.multiple_of(i * norm_rows * tok_rows, norm_rows * tok_rows)
            x3 = acc[pl.ds(r0, norm_rows * tok_rows), :].reshape(norm_rows, tok_rows, LANES)
            ms = jnp.sum(jnp.sum(x3 * x3, axis=2, keepdims=True), axis=1, keepdims=True) * (1.0 / dm)
            hn3 = x3 * lax.rsqrt(ms + EPS) * gffn_ref[...][None]
            hn_ref[pl.ds(r0, norm_rows * tok_rows), :] = hn3.reshape(norm_rows * tok_rows, LANES)
            return carry

        lax.fori_loop(0, tb // norm_rows, norm, 0)

    n_exp = wgu_ref.shape[0]
    offs = [offs_ref[0, 0, e * n_exp + k] for k in range(n_exp + 1)]
    cnts = [offs[k + 1] - offs[k] for k in range(n_exp)]

    def chunk(c, carry):
        bases = [offs[k] + c * mc for k in range(n_exp)]
        ns = [jnp.clip(cnts[k] - c * mc, 0, mc) for k in range(n_exp)]

        for k in range(n_exp):
            def gather8(g, carry, k=k):
                for u in range(SUBLANES):
                    i = g * SUBLANES + u
                    row = pl.multiple_of(toks_ref[0, 0, bases[k] + i], tok_rows)
                    gbuf[k, pl.ds(pl.multiple_of(i * tok_rows, tok_rows), tok_rows), :] = (
                        hn_ref[pl.ds(row, tok_rows), :])
                return carry

            lax.fori_loop(0, (ns[k] + SUBLANES - 1) // SUBLANES, gather8, 0)

        for k in range(n_exp):
            x = jnp.concatenate([gbuf[k, pl.ds(j, mc, stride=tok_rows), :].astype(BF16) for j in range(tok_rows)],
                                axis=-1)
            gu = _dot(x, wgu_ref[k])
            hdn = jax.nn.silu(gu[:, :d_expert]) * gu[:, d_expert:]
            y = _dot(hdn.astype(BF16), wd_ref[k])
            for j in range(tok_rows):
                ybuf[k, pl.ds(j, mc, stride=tok_rows), :] = y[:, j * LANES:(j + 1) * LANES]

        for k in range(n_exp):
            def add_rows(first, count, k=k):
                idx = [first + u for u in range(count)]
                dst = [pl.multiple_of(toks_ref[0, 0, bases[k] + i], tok_rows) for i in idx]
                val = [acc[pl.ds(d, tok_rows), :] + gws_ref[0, 0, bases[k] + i]
                       * ybuf[k, pl.ds(pl.multiple_of(i * tok_rows, tok_rows), tok_rows), :]
                       for i, d in zip(idx, dst)]
                for d, v in zip(dst, val):
                    acc[pl.ds(d, tok_rows), :] = v

            n_full = ns[k] // SUBLANES

            def scatter8(g, carry, add_rows=add_rows):
                add_rows(g * SUBLANES, SUBLANES)
                return carry

            def scatter1(i, carry, add_rows=add_rows):
                add_rows(i, 1)
                return carry

            lax.fori_loop(0, n_full, scatter8, 0)
            lax.fori_loop(n_full * SUBLANES, ns[k], scatter1, 0)
        return carry

    n_chunks = functools.reduce(jnp.maximum, [(cnt + mc - 1) // mc for cnt in cnts])
    lax.fori_loop(0, n_chunks, chunk, 0)

    @pl.when(e == pl.num_programs(1) - 1)
    def _():
        for j in range(tok_rows):
            o_ref[:, j * LANES:(j + 1) * LANES] = acc[pl.ds(j, tb, stride=tok_rows), :]


def _moe_call(offs, toks, gws, h_tiles, w, *, tb, mc):
    tok_rows = SUBLANES
    dm = tok_rows * LANES
    t = h_tiles.shape[0] // tok_rows
    assert w['w_gu'].shape[1] == dm, "a token row must be exactly one (8,128) f32 tile"
    nb = t // tb
    d_expert = w['w_d'].shape[1]
    kern = functools.partial(_moe_kernel, d_expert=d_expert, tb=tb, mc=mc, norm_rows=min(tb, 256))
    smem = lambda n: pl.BlockSpec((1, 1, n), lambda b, e: (b, 0, 0), memory_space=pltpu.SMEM)
    n_exp = MOE_EXPERTS_PER_STEP
    return pl.pallas_call(
        kern,
        grid=(nb, N_EXPERTS // n_exp),
        in_specs=[
            smem(offs.shape[2]), smem(toks.shape[2]), smem(gws.shape[2]),
            pl.BlockSpec(memory_space=pl.ANY),
            pl.BlockSpec((tok_rows, LANES), lambda b, e: (0, 0)),
            pl.BlockSpec((n_exp, dm, 2 * d_expert), lambda b, e: (e, 0, 0)),
            pl.BlockSpec((n_exp, d_expert, dm), lambda b, e: (e, 0, 0)),
        ],
        out_specs=pl.BlockSpec((tb, dm), lambda b, e: (b, 0)),
        out_shape=jax.ShapeDtypeStruct((t, dm), F32),
        scratch_shapes=[
            pltpu.VMEM((tb * tok_rows, LANES), F32),
            pltpu.VMEM((tb * tok_rows, LANES), F32),
            pltpu.VMEM((n_exp, mc * tok_rows, LANES), F32),
            pltpu.VMEM((n_exp, mc * tok_rows, LANES), F32),
            pltpu.SemaphoreType.DMA(()),
        ],
        compiler_params=pltpu.CompilerParams(
            dimension_semantics=("parallel", "arbitrary"), vmem_limit_bytes=VMEM_LIMIT),
    )(offs, toks, gws, h_tiles, w['g_ffn'].reshape(tok_rows, LANES), w['w_gu'], w['w_d'])


def _route_tables(route, tb):
    t = route.shape[0]
    nb = t // tb
    ids = route[:, :2].astype(jnp.int32).reshape(nb, 2 * tb)
    wts = route[:, 2:4].reshape(nb, 2 * tb)
    order = jnp.argsort(ids, axis=1, stable=True)
    toks = (order // 2 * SUBLANES).astype(jnp.int32)
    toks = jnp.pad(toks, ((0, 0), (0, SUBLANES)))
    gws = jnp.take_along_axis(wts, order, axis=1)
    counts = jnp.sum(ids[:, :, None] == jnp.arange(N_EXPERTS, dtype=jnp.int32), axis=1, dtype=jnp.int32)
    offs = jnp.concatenate([jnp.zeros((nb, 1), jnp.int32), jnp.cumsum(counts, axis=1)], axis=1)
    offs = jnp.pad(offs, ((0, 0), (0, LANES - offs.shape[1])))
    return offs[:, None, :], toks[:, None, :], gws[:, None, :]


def _prepare_weights(p):
    dm, in_cols = p['w_in'].shape
    s5_width = p['s5_d'].shape[0]
    q_lora = p['q_lora_norm_g'].shape[0]
    kv_lora = p['kv_lora_norm_g'].shape[0]
    half_r = QK_ROPE // 2
    w = {'s5_width': s5_width, 'q_lora': q_lora, 'kv_lora': kv_lora}

    def partner(a):
        return jnp.concatenate([-a[..., half_r:], a[..., :half_r]], axis=-1)

    def swap(a):
        return jnp.concatenate([a[..., half_r:], a[..., :half_r]], axis=-1)

    w_kr = p['w_in'][:, in_cols - QK_ROPE:]
    w['w_in'] = jnp.concatenate(
        [p['w_in'], partner(w_kr), jnp.zeros((dm, LANES - 2 * QK_ROPE), F32)], axis=-1).astype(BF16)
    w['g_mix'] = p['norm_mix_g'][None]
    w['g_q'] = p['q_lora_norm_g'][None]
    w['g_kv'] = p['kv_lora_norm_g'][None]
    gkr = p['k_rope_norm_g']
    w['g_kr'] = jnp.concatenate([gkr, swap(gkr), jnp.zeros((LANES - 2 * QK_ROPE,), F32)])[None]

    wq = p['w_uq']
    wq_r = wq[..., QK_NOPE:]
    w['w_uq'] = jnp.concatenate([wq, partner(wq_r)], axis=-1).reshape(q_lora, N_HEADS * HEAD_PAD).astype(BF16)
    gr = p['q_rope_norm_g']
    w['q_gain'] = jnp.tile(jnp.concatenate([p['q_nope_norm_g'], gr, swap(gr)]), N_HEADS)[None]
    idx = jnp.arange(HEAD_PAD)
    row_nope = (idx < QK_NOPE)[:, None]
    row_rope = ((idx >= QK_NOPE) & (idx < QK_DIM))[:, None]
    col_nope = (idx < QK_NOPE)[None, :]
    m_q = jnp.where(row_nope & col_nope, 1.0 / QK_NOPE, 0.0) + jnp.where(row_rope & ~col_nope, 1.0 / QK_ROPE, 0.0)
    m_k = jnp.where(row_nope, 1.0 / QK_NOPE, 0.0) * jnp.ones((1, HEAD_PAD), F32)
    eye2 = jnp.eye(2, dtype=F32)
    w['bd_q'] = jnp.kron(eye2, m_q).astype(BF16)
    w['bd_k'] = jnp.kron(eye2, m_k).astype(BF16)

    pad_h = lambda a: jnp.pad(a, ((0, 0), (0, 0), (0, HEAD_PAD - a.shape[-1])))
    w['w_uk'] = pad_h(p['w_uk']).reshape(kv_lora, N_HEADS * HEAD_PAD).astype(BF16)
    w['w_uv'] = pad_h(p['w_uv']).reshape(kv_lora, N_HEADS * HEAD_PAD).astype(BF16)
    w['k_gain'] = jnp.tile(jnp.pad(p['k_nope_norm_g'], (0, HEAD_PAD - QK_NOPE)), N_HEADS)[None]

    dt = jnp.exp(p['s5_log_dt'])[:, None]
    ar, ai = p['s5_a_re'], p['s5_a_im']
    mag = jnp.exp(dt * ar)
    abar_re = mag * jnp.cos(dt * ai)
    abar_im = mag * jnp.sin(dt * ai)
    den = ar * ar + ai * ai
    num_re = abar_re - 1.0
    coef_re = (num_re * ar + abar_im * ai) / den
    coef_im = (abar_im * ar - num_re * ai) / den
    br, bi = p['s5_b_re'], p['s5_b_im']
    bbar_re = coef_re[..., None] * br - coef_im[..., None] * bi
    bbar_im = coef_re[..., None] * bi + coef_im[..., None] * br
    g = ar.shape[0]
    n_state = g * S5_STATE
    half = n_state // 2
    eye_g = jnp.eye(g, dtype=F32)
    b_re = jnp.einsum('gni,gh->gihn', bbar_re, eye_g).reshape(s5_width, n_state)
    b_im = jnp.einsum('gni,gh->gihn', bbar_im, eye_g).reshape(s5_width, n_state)
    c_re = jnp.einsum('gon,gh->gnho', p['s5_c_re'], eye_g).reshape(n_state, s5_width)
    c_im = jnp.einsum('gon,gh->gnho', p['s5_c_im'], eye_g).reshape(n_state, s5_width)
    uw = s5_width // 2
    w['b_bd'] = jnp.stack([
        jnp.concatenate([b_re[c * uw:(c + 1) * uw, c * half:(c + 1) * half],
                         b_im[c * uw:(c + 1) * uw, c * half:(c + 1) * half]], axis=1) for c in range(2)]).astype(BF16)
    w['c_bd'] = jnp.stack([
        jnp.concatenate([c_re[c * half:(c + 1) * half, c * uw:(c + 1) * uw],
                         -c_im[c * half:(c + 1) * half, c * uw:(c + 1) * uw]], axis=0) for c in range(2)]).astype(BF16)
    w['lam_re'] = jnp.broadcast_to(abar_re.reshape(1, n_state), (SUBLANES, n_state))
    w['lam_im'] = jnp.broadcast_to(abar_im.reshape(1, n_state), (SUBLANES, n_state))
    w['s5_d'] = p['s5_d'][None]
    w['w_glu'] = p['s5_w_glu'].astype(BF16)
    w['b_glu'] = p['s5_b_glu'][None]

    w['g_s5'] = p['out_norm_s5_g'][None]
    w['g_attn'] = p['out_norm_attn_g'][None]
    w['w_out'] = p['w_out'].astype(BF16)
    w['g_ffn'] = p['norm_ffn_g'][None]
    n_r = N_EXPERTS + N_EXPERT_GROUPS
    w['w_router'] = jnp.pad(jnp.concatenate([p['w_router_expert'], p['w_router_group']], axis=1),
                            ((0, 0), (0, LANES - n_r))).astype(BF16)
    w['b_router'] = jnp.pad(jnp.concatenate([p['b_router_expert'], p['b_router_group']]), (0, LANES - n_r))[None]
    w['w_gu'] = jnp.concatenate([p['w_e_gate'], p['w_e_up']], axis=-1).astype(BF16)
    w['w_d'] = p['w_e_down'].astype(BF16)
    return w


def _rope_table(pos):
    half = QK_ROPE // 2
    inv = ROPE_THETA ** (-jnp.arange(half, dtype=F32) / half)
    ang = pos.astype(F32)[:, None] * inv[None, :]
    cc = jnp.tile(jnp.cos(ang), (1, 2))
    ss = jnp.tile(jnp.sin(ang), (1, 2))
    n = pos.shape[0]
    scale = QK_DIM ** -0.5 * math.log2(math.e)
    tab_q = scale * jnp.concatenate([jnp.ones((n, QK_NOPE), F32), cc, ss], axis=-1)
    tab_k = jnp.concatenate([cc, ss, jnp.zeros((n, LANES - 2 * QK_ROPE), F32)], axis=-1)
    return jnp.concatenate([tab_q, tab_k], axis=-1)


def _s5_state_in(re, im):
    b = re.shape[0]
    re = re.reshape(b, -1)
    im = im.reshape(b, -1)
    half = re.shape[1] // 2
    return jnp.concatenate([re[:, :half], im[:, :half], re[:, half:], im[:, half:]], axis=1)


def _s5_state_out(h, g):
    b = h.shape[0]
    half = h.shape[1] // 4
    re = jnp.concatenate([h[:, :half], h[:, 2 * half:3 * half]], axis=1).reshape(b, g, S5_STATE)
    im = jnp.concatenate([h[:, half:2 * half], h[:, 3 * half:]], axis=1).reshape(b, g, S5_STATE)
    return re, im


def _trunk_layer(x, past_ckv, past_krope, h0_re, h0_im, w):
    bsz, seq, dm = x.shape
    assert bsz == SUBLANES, "the S5 scan keeps the batch on the sublanes"
    past_len = 0 if past_ckv is None else past_ckv.shape[1]
    g = w['lam_re'].shape[1] // S5_STATE
    tm = _tile(seq, 512)

    tab = _rope_table(past_len + jnp.arange(seq, dtype=jnp.int32))
    u_tb, q, new_ckv, new_krope = _proj_call(x, tab, w, tm=tm)

    if h0_re is None:
        h0 = jnp.zeros((bsz, 2 * g * S5_STATE), F32)
    else:
        h0 = _s5_state_in(h0_re.astype(F32), h0_im.astype(F32))
    s5_tb, h_last = _s5_call(u_tb.reshape(seq * bsz, -1), h0, w, ts=min(seq, 64))
    h_re, h_im = _s5_state_out(h_last, g)

    if past_ckv is None:
        all_ckv, all_krope = new_ckv, new_krope
    else:
        all_ckv = jnp.concatenate([past_ckv.astype(F32), new_ckv], axis=1)
        all_krope = jnp.concatenate([past_krope.astype(F32), new_krope], axis=1)
    sk = all_ckv.shape[1]
    sk_pad = -(-sk // LANES) * LANES
    all_ckv = jnp.pad(all_ckv, ((0, 0), (0, sk_pad - sk), (0, 0)))
    krp = jnp.pad(all_krope, ((0, 0), (0, sk_pad - sk), (QK_NOPE, LANES - QK_DIM)))
    whole = sk_pad % ATTN_TK != 0
    k, vt = _kv_call(all_ckv, krp, w, tm=sk_pad if whole else KV_TILE)
    sq_pad = -(-seq // LANES) * LANES
    q = jnp.pad(q, ((0, 0), (0, 0), (0, sq_pad - seq), (0, 0)))
    tq = ATTN_TQ if sq_pad % ATTN_TQ == 0 else LANES
    attn = _attn_call(q, k, vt, tq=tq, tk=sk_pad if whole else ATTN_TK, q_pos0=past_len, sk_valid=sk)[:, :seq]

    h_tiles, route = _out_call(x, s5_tb.reshape(seq, -1), attn, w, tm=tm)
    t = bsz * seq
    tb = _tile(t, MOE_BLOCK)
    offs, toks, gws = _route_tables(route.reshape(t, LANES), tb)
    y = _moe_call(offs, toks, gws, h_tiles, w, tb=tb, mc=MOE_CHUNK)
    return y.reshape(bsz, seq, dm), new_ckv, new_krope, h_re, h_im


def kernel(x_prompt, x_sample, cache_ckv, cache_krope, state_s5_re, state_s5_im, norm_mix_g, w_in, s5_a_re, s5_a_im, s5_log_dt, s5_b_re, s5_b_im, s5_c_re, s5_c_im, s5_d, s5_w_glu, s5_b_glu, q_lora_norm_g, w_uq, kv_lora_norm_g, w_uk, w_uv, q_nope_norm_g, q_rope_norm_g, k_nope_norm_g, k_rope_norm_g, out_norm_s5_g, out_norm_attn_g, w_out, norm_ffn_g, w_router_group, b_router_group, w_router_expert, b_router_expert, w_e_gate, w_e_up, w_e_down):
    params = dict(
        norm_mix_g=norm_mix_g, w_in=w_in, s5_a_re=s5_a_re, s5_a_im=s5_a_im, s5_log_dt=s5_log_dt,
        s5_b_re=s5_b_re, s5_b_im=s5_b_im, s5_c_re=s5_c_re, s5_c_im=s5_c_im, s5_d=s5_d,
        s5_w_glu=s5_w_glu, s5_b_glu=s5_b_glu, q_lora_norm_g=q_lora_norm_g, w_uq=w_uq,
        kv_lora_norm_g=kv_lora_norm_g, w_uk=w_uk, w_uv=w_uv, q_nope_norm_g=q_nope_norm_g,
        q_rope_norm_g=q_rope_norm_g, k_nope_norm_g=k_nope_norm_g, k_rope_norm_g=k_rope_norm_g,
        out_norm_s5_g=out_norm_s5_g, out_norm_attn_g=out_norm_attn_g, w_out=w_out, norm_ffn_g=norm_ffn_g,
        w_router_group=w_router_group, b_router_group=b_router_group,
        w_router_expert=w_router_expert, b_router_expert=b_router_expert,
        w_e_gate=w_e_gate, w_e_up=w_e_up, w_e_down=w_e_down)
    depth = w_in.shape[0]
    y_p, y_s = x_prompt, x_sample
    outs = [[] for _ in range(8)]
    for l in range(depth):
        w = _prepare_weights({k: a[l] for k, a in params.items()})
        y_p, c1, k1, r1, i1 = _trunk_layer(y_p, None, None, None, None, w)
        y_s, c2, k2, r2, i2 = _trunk_layer(y_s, cache_ckv[l], cache_krope[l], state_s5_re[l], state_s5_im[l], w)
        for lst, a in zip(outs, (c1, k1, r1, i1, c2, k2, r2, i2)):
            lst.append(a)
    return (y_p, y_s) + tuple(jnp.stack(lst) for lst in outs)
```

```python
import functools
import math

import jax
import jax.numpy as jnp
from jax import lax
from jax.experimental import pallas as pl
from jax.experimental.pallas import tpu as pltpu

F32 = jnp.float32
BF16 = jnp.bfloat16

CHUNK = 64
S5_GROUP = 16
S5_STATE = 64
N_HEADS = 8
QK_NOPE = 64
QK_ROPE = 32
QK_DIM = QK_NOPE + QK_ROPE
V_DIM = 64
ROPE_THETA = 10000.0
N_EXPERT_GROUPS = 4
EXPERTS_PER_GROUP = 8
N_EXPERTS = N_EXPERT_GROUPS * EXPERTS_PER_GROUP
EPS = 1e-6
NEG_INF = -1e30

LANES = 128
SUBLANES = 8
HEAD_PAD = LANES
ATTN_TQ = 512
ATTN_TK = 512
KV_TILE = 512
MOE_BLOCK = 2048
MOE_CHUNK = 160
MOE_EXPERTS_PER_STEP = 2
REF_MARGIN = 1.02
REF_LIMIT = 40.0
ATTN_SPAN = 2
ATTN_COLS = 256
VT_ROWS = 80
VMEM_LIMIT = 48 * 1024 * 1024


def _tile(n, cap):
    for t in range(min(cap, n), 15, -1):
        if n % t == 0 and t % 16 == 0:
            return t
    return n


def _rms(x, g):
    return x * lax.rsqrt(jnp.mean(x * x, axis=-1, keepdims=True) + EPS) * g


def _dot(a, b):
    return jnp.dot(a, b, preferred_element_type=F32)


def _lane_iota(shape):
    return lax.broadcasted_iota(jnp.int32, shape, len(shape) - 1)


def _proj_kernel(x_ref, gmix_ref, win_ref, gq_ref, wuq_ref, bdq_ref, qgain_ref, gkv_ref, gkr_ref, tab_ref,
                 u_ref, q_ref, ckv_ref, kr_ref, *, s5_width, q_lora, kv_lora):
    x = x_ref[0]
    xn = _rms(x, gmix_ref[...])
    proj = _dot(xn.astype(BF16), win_ref[...])
    u_ref[...] = proj[:, :s5_width].astype(BF16)

    o = s5_width
    cqn = _rms(proj[:, o:o + q_lora], gq_ref[...])
    o += q_lora
    ckv_ref[0] = _rms(proj[:, o:o + kv_lora], gkv_ref[...])
    o += kv_lora

    tab = tab_ref[...]
    lane = _lane_iota((x.shape[0], LANES))

    kr = proj[:, o:o + LANES]
    ms = jnp.sum(jnp.where(lane < QK_ROPE, kr * kr, 0.0), axis=-1, keepdims=True) * (1.0 / QK_ROPE)
    t = kr * lax.rsqrt(ms + EPS) * gkr_ref[...] * tab[:, LANES:]
    kro = t + pltpu.roll(t, LANES - QK_ROPE, axis=1)
    kr_ref[0] = kro[:, :QK_ROPE]

    qa = _dot(cqn.astype(BF16), wuq_ref[...])
    qsq = (qa * qa).astype(BF16)
    pair = 2 * HEAD_PAD
    ms = jnp.concatenate([_dot(qsq[:, c * pair:(c + 1) * pair], bdq_ref[...])
                          for c in range(N_HEADS // 2)], axis=-1)
    qn = qa * lax.rsqrt(ms + EPS) * qgain_ref[...]
    is_rope = (lane >= QK_NOPE) & (lane < QK_DIM)
    for h in range(N_HEADS):
        th = qn[:, h * HEAD_PAD:(h + 1) * HEAD_PAD] * tab[:, :LANES]
        rolled = pltpu.roll(th, LANES - QK_ROPE, axis=1)
        oh = jnp.where(lane < QK_DIM, th + jnp.where(is_rope, rolled, 0.0), 0.0)
        q_ref[0, h] = oh.astype(BF16)


def _proj_call(x, tab, w, *, tm):
    bsz, seq, dm = x.shape
    s5_width, q_lora, kv_lora = w['s5_width'], w['q_lora'], w['kv_lora']
    in_cols = w['w_in'].shape[1]
    hp = N_HEADS * HEAD_PAD
    full = lambda shape: pl.BlockSpec(shape, lambda b, i: (0,) * len(shape))
    kern = functools.partial(_proj_kernel, s5_width=s5_width, q_lora=q_lora, kv_lora=kv_lora)
    return pl.pallas_call(
        kern,
        grid=(bsz, seq // tm),
        in_specs=[
            pl.BlockSpec((1, tm, dm), lambda b, i: (b, i, 0)),
            full((1, dm)), full((dm, in_cols)), full((1, q_lora)), full((q_lora, hp)),
            full((2 * HEAD_PAD, 2 * HEAD_PAD)), full((1, hp)), full((1, kv_lora)), full((1, LANES)),
            pl.BlockSpec((tm, 2 * LANES), lambda b, i: (i, 0)),
        ],
        out_specs=[
            pl.BlockSpec((tm, s5_width), lambda b, i: (i, b)),
            pl.BlockSpec((1, N_HEADS, tm, HEAD_PAD), lambda b, i: (b, 0, i, 0)),
            pl.BlockSpec((1, tm, kv_lora), lambda b, i: (b, i, 0)),
            pl.BlockSpec((1, tm, QK_ROPE), lambda b, i: (b, i, 0)),
        ],
        out_shape=[
            jax.ShapeDtypeStruct((seq, bsz * s5_width), BF16),
            jax.ShapeDtypeStruct((bsz, N_HEADS, seq, HEAD_PAD), BF16),
            jax.ShapeDtypeStruct((bsz, seq, kv_lora), F32),
            jax.ShapeDtypeStruct((bsz, seq, QK_ROPE), F32),
        ],
        compiler_params=pltpu.CompilerParams(
            dimension_semantics=("parallel", "parallel"), vmem_limit_bytes=VMEM_LIMIT),
    )(x, w['g_mix'], w['w_in'], w['g_q'], w['w_uq'], w['bd_q'], w['q_gain'], w['g_kv'], w['g_kr'], tab)


def _kv_kernel(ckv_ref, krp_ref, wuk_ref, wuv_ref, bdk_ref, kg_ref, k_ref, v_ref):
    c = ckv_ref[0].astype(BF16)
    ka = _dot(c, wuk_ref[...])
    va = _dot(c, wuv_ref[...])
    ksq = (ka * ka).astype(BF16)
    pair = 2 * HEAD_PAD
    ms = jnp.concatenate([_dot(ksq[:, c2 * pair:(c2 + 1) * pair], bdk_ref[...])
                          for c2 in range(N_HEADS // 2)], axis=-1)
    kn = ka * lax.rsqrt(ms + EPS) * kg_ref[...]
    krp = krp_ref[0]
    lane = _lane_iota(krp.shape)
    ones_col = jnp.where(lane == V_DIM, 1.0, 0.0)
    for h in range(N_HEADS):
        k_ref[0, h] = (kn[:, h * HEAD_PAD:(h + 1) * HEAD_PAD] + krp).astype(BF16)
        vt = (va[:, h * HEAD_PAD:(h + 1) * HEAD_PAD] + ones_col).T
        v_ref[0, h] = vt[:VT_ROWS].astype(BF16)


def _kv_call(ckv, krp, w, *, tm):
    bsz, sk, kv_lora = ckv.shape
    hp = N_HEADS * HEAD_PAD
    full = lambda shape: pl.BlockSpec(shape, lambda b, i: (0,) * len(shape))
    return pl.pallas_call(
        _kv_kernel,
        grid=(bsz, sk // tm),
        in_specs=[
            pl.BlockSpec((1, tm, kv_lora), lambda b, i: (b, i, 0)),
            pl.BlockSpec((1, tm, LANES), lambda b, i: (b, i, 0)),
            full((kv_lora, hp)), full((kv_lora, hp)), full((2 * HEAD_PAD, 2 * HEAD_PAD)), full((1, hp)),
        ],
        out_specs=[
            pl.BlockSpec((1, N_HEADS, tm, HEAD_PAD), lambda b, i: (b, 0, i, 0)),
            pl.BlockSpec((1, N_HEADS, VT_ROWS, tm), lambda b, i: (b, 0, 0, i)),
        ],
        out_shape=[
            jax.ShapeDtypeStruct((bsz, N_HEADS, sk, HEAD_PAD), BF16),
            jax.ShapeDtypeStruct((bsz, N_HEADS, VT_ROWS, sk), BF16),
        ],
        compiler_params=pltpu.CompilerParams(
            dimension_semantics=("parallel", "parallel"), vmem_limit_bytes=VMEM_LIMIT),
    )(ckv, krp, w['w_uk'], w['w_uv'], w['bd_k'], w['k_gain'])


def _attn_kernel(q_ref, k_ref, vt_ref, o_ref, *, tq, tk, q_pos0, sk_valid):
    sq = q_ref.shape[2]
    nq = sq // tq

    n_heads = q_ref.shape[1]

    cw = min(tq, ATTN_COLS)
    streams = [(hh, c) for hh in range(n_heads) for c in range(tq // cw)]
    sk = k_ref.shape[2]
    nt = (((1,), (1,)), ((), ()))
    ones_rows = jnp.ones((SUBLANES, HEAD_PAD), BF16)

    def sq_norms(x):
        xf = x.astype(F32)
        return lax.dot_general(ones_rows, (xf * xf).astype(BF16), nt, preferred_element_type=F32)[:1]

    def kmax_body(j, carry):
        k0 = pl.multiple_of(j * tk, tk)
        return tuple(jnp.maximum(carry[hh], sq_norms(k_ref[0, hh, pl.ds(k0, tk), :])) for hh in range(n_heads))

    kmax2 = lax.fori_loop(0, sk // tk, kmax_body, tuple(jnp.zeros((1, tk), F32) for _ in range(n_heads)))
    kmax2 = [jnp.max(v, axis=1, keepdims=True) for v in kmax2]

    def scores(qs, p0, j, width, masked):
        k0 = pl.multiple_of(j * width, width)
        kbs = [k_ref[0, hh, pl.ds(k0, width), :] for hh in range(n_heads)]
        ss = [lax.dot_general(kbs[hh], qs[i], nt, preferred_element_type=F32)
              for i, (hh, c) in enumerate(streams)]
        if masked:
            k_pos = k0 + lax.broadcasted_iota(jnp.int32, (width, 1), 0)
            for c in range(tq // cw):
                q_pos = p0 + c * cw + lax.broadcasted_iota(jnp.int32, (1, cw), 1)
                visible = k_pos < jnp.minimum((q_pos // CHUNK + 1) * CHUNK, sk_valid)
                for i, (hh, c2) in enumerate(streams):
                    if c2 == c:
                        ss[i] = jnp.where(visible, ss[i], NEG_INF)
        return ss

    def values(j, width):
        k0 = pl.multiple_of(j * width, width)
        return [vt_ref[0, hh, :, pl.ds(k0, width)] for hh in range(n_heads)]

    def step_running_max(qs, p0, j, carry, masked):
        ss, vts = scores(qs, p0, j, tk, masked), values(j, tk)
        new = []
        for i, (hh, c) in enumerate(streams):
            m, acc = carry[i]
            m_new = jnp.maximum(m, jnp.max(ss[i], axis=0, keepdims=True))
            acc = jnp.exp2(m - m_new) * acc + _dot(vts[hh], jnp.exp2(ss[i] - m_new).astype(BF16))
            new.append((m_new, acc))
        return tuple(new)

    def step_fixed_ref(qs, refs, p0, j, width, carry, masked):
        ss, vts = scores(qs, p0, j, width, masked), values(j, width)
        return tuple(carry[i] + _dot(vts[hh], jnp.exp2(ss[i] - refs[i]).astype(BF16))
                     for i, (hh, c) in enumerate(streams))

    def q_body(qi, carry):
        q0 = pl.multiple_of(qi * tq, tq)
        qs = [q_ref[0, hh, pl.ds(q0 + c * cw, cw), :] for hh, c in streams]
        p0 = q_pos0 + q0
        n_open = jnp.minimum((p0 // CHUNK + 1) * CHUNK, sk_valid) // tk
        v1 = jnp.minimum(((p0 + tq - 1) // CHUNK + 1) * CHUNK, sk_valid)
        n_all = (v1 + tk - 1) // tk
        refs = [jnp.sqrt(sq_norms(qs[i]) * kmax2[hh]) * REF_MARGIN for i, (hh, c) in enumerate(streams)]
        ref_max = functools.reduce(jnp.maximum, [jnp.max(r) for r in refs])

        def fixed_ref():
            span = ATTN_SPAN if ATTN_SPAN * tk <= sk else 1
            n_wide = n_open // span
            st = tuple(jnp.zeros((VT_ROWS, cw), F32) for _ in streams)
            if span > 1:
                st = lax.fori_loop(0, n_wide,
                                   lambda j, c: step_fixed_ref(qs, refs, p0, j, span * tk, c, False), st)
            st = lax.fori_loop(n_wide * span if span > 1 else 0, n_open,
                               lambda j, c: step_fixed_ref(qs, refs, p0, j, tk, c, False), st)
            return lax.fori_loop(n_open, n_all, lambda j, c: step_fixed_ref(qs, refs, p0, j, tk, c, True), st)

        def running_max():
            st = tuple((jnp.full((1, cw), NEG_INF, F32), jnp.zeros((VT_ROWS, cw), F32)) for _ in streams)
            st = lax.fori_loop(0, n_open, lambda j, c: step_running_max(qs, p0, j, c, False), st)
            st = lax.fori_loop(n_open, n_all, lambda j, c: step_running_max(qs, p0, j, c, True), st)
            return tuple(acc for _, acc in st)

        accs = lax.cond(ref_max < REF_LIMIT, fixed_ref, running_max)
        for c in range(tq // cw):
            ot = jnp.concatenate([accs[i][:V_DIM] / accs[i][V_DIM:V_DIM + 1]
                                  for i, (hh, c2) in enumerate(streams) if c2 == c], axis=0)
            o_ref[0, pl.ds(q0 + c * cw, cw), :] = ot.T.astype(o_ref.dtype)
        return carry

    lax.fori_loop(0, nq, q_body, 0)


def _attn_call(q, k, vt, *, tq, tk, q_pos0, sk_valid):
    bsz, nh, sq, _ = q.shape
    sk = k.shape[2]
    assert sq % tq == 0 and sk % tk == 0 and tq % LANES == 0 and tk % LANES == 0
    kern = functools.partial(_attn_kernel, tq=tq, tk=tk, q_pos0=q_pos0, sk_valid=sk_valid)
    return pl.pallas_call(
        kern,
        grid=(bsz, nh // 2),
        in_specs=[
            pl.BlockSpec((1, 2, sq, HEAD_PAD), lambda b, h: (b, h, 0, 0)),
            pl.BlockSpec((1, 2, sk, HEAD_PAD), lambda b, h: (b, h, 0, 0)),
            pl.BlockSpec((1, 2, VT_ROWS, sk), lambda b, h: (b, h, 0, 0)),
        ],
        out_specs=pl.BlockSpec((1, sq, 2 * V_DIM), lambda b, h: (b, 0, h)),
        out_shape=jax.ShapeDtypeStruct((bsz, sq, nh * V_DIM), BF16),
        compiler_params=pltpu.CompilerParams(
            dimension_semantics=("parallel", "parallel"), vmem_limit_bytes=VMEM_LIMIT),
    )(q, k, vt)


def _s5_kernel(u_ref, h0_ref, lre_ref, lim_ref, bbd_ref, cbd_ref, d_ref, wglu_ref, bglu_ref,
               y_ref, hout_ref, hbuf, hstate, *, ts, n_half, col_chunk):
    i = pl.program_id(0)
    half = n_half
    width = hbuf.shape[1]
    uw = u_ref.shape[1] // 2

    @pl.when(i == 0)
    def _():
        hstate[...] = h0_ref[...]

    u = u_ref[...]
    for c in range(2):
        hbuf[:, c * 2 * half:(c + 1) * 2 * half] = _dot(u[:, c * uw:(c + 1) * uw], bbd_ref[c])

    for c in range(2):
        for cc in range(half // col_chunk):
            re0 = c * 2 * half + cc * col_chunk
            im0 = re0 + half
            l0 = c * half + cc * col_chunk
            ar = lre_ref[:, l0:l0 + col_chunk]
            ai = lim_ref[:, l0:l0 + col_chunk]

            def step(s, carry, re0=re0, im0=im0, ar=ar, ai=ai):
                hr, hi = carry
                r0 = pl.multiple_of(s * SUBLANES, SUBLANES)
                nr = ar * hr - ai * hi + hbuf[pl.ds(r0, SUBLANES), re0:re0 + col_chunk]
                ni = ar * hi + ai * hr + hbuf[pl.ds(r0, SUBLANES), im0:im0 + col_chunk]
                hbuf[pl.ds(r0, SUBLANES), re0:re0 + col_chunk] = nr
                hbuf[pl.ds(r0, SUBLANES), im0:im0 + col_chunk] = ni
                return nr, ni

            hr, hi = lax.fori_loop(
                0, ts, step,
                (hstate[:, re0:re0 + col_chunk], hstate[:, im0:im0 + col_chunk]), unroll=8)
            hstate[:, re0:re0 + col_chunk] = hr
            hstate[:, im0:im0 + col_chunk] = hi

    hout_ref[...] = hstate[...]

    y = jnp.concatenate(
        [_dot(hbuf[:, c * 2 * half:(c + 1) * 2 * half].astype(BF16), cbd_ref[c]) for c in range(2)], axis=-1)
    y = y + d_ref[...] * u.astype(F32)
    z = jax.nn.gelu(y)
    gate = _dot(z.astype(BF16), wglu_ref[...]) + bglu_ref[...]
    y_ref[...] = (z * jax.nn.sigmoid(gate)).astype(y_ref.dtype)


def _s5_call(u_tb, h0, w, *, ts):
    rows, s5_width = u_tb.shape
    n_state = w['lam_re'].shape[1]
    half = n_state // 2
    width = 2 * n_state
    n_steps = rows // (ts * SUBLANES)
    full = lambda shape: pl.BlockSpec(shape, lambda i: (0,) * len(shape))
    kern = functools.partial(_s5_kernel, ts=ts, n_half=half, col_chunk=512)
    return pl.pallas_call(
        kern,
        grid=(n_steps,),
        in_specs=[
            pl.BlockSpec((ts * SUBLANES, s5_width), lambda i: (i, 0)),
            full((SUBLANES, width)), full((SUBLANES, n_state)), full((SUBLANES, n_state)),
            full((2, s5_width // 2, 2 * half)), full((2, 2 * half, s5_width // 2)),
            full((1, s5_width)), full((s5_width, s5_width)), full((1, s5_width)),
        ],
        out_specs=[
            pl.BlockSpec((ts * SUBLANES, s5_width), lambda i: (i, 0)),
            full((SUBLANES, width)),
        ],
        out_shape=[
            jax.ShapeDtypeStruct((rows, s5_width), BF16),
            jax.ShapeDtypeStruct((SUBLANES, width), F32),
        ],
        scratch_shapes=[
            pltpu.VMEM((ts * SUBLANES, width), F32),
            pltpu.VMEM((SUBLANES, width), F32),
        ],
        compiler_params=pltpu.CompilerParams(
            dimension_semantics=("arbitrary",), vmem_limit_bytes=VMEM_LIMIT),
    )(u_tb, h0, w['lam_re'], w['lam_im'], w['b_bd'], w['c_bd'], w['s5_d'], w['w_glu'], w['b_glu'])


def _out_kernel(x_ref, s5_ref, at_ref, gs5_ref, gat_ref, wout_ref, gffn_ref, wr_ref, br_ref,
                h_ref, route_ref):
    s5n = _rms(s5_ref[...].astype(F32), gs5_ref[...])
    atn = _rms(at_ref[0].astype(F32), gat_ref[...])
    merged = jnp.concatenate([s5n, atn], axis=-1).astype(BF16)
    h = x_ref[0] + _dot(merged, wout_ref[...])
    tm = h.shape[0]
    for j in range(h.shape[1] // LANES):
        h_ref[pl.ds(j, tm, stride=SUBLANES), :] = h[:, j * LANES:(j + 1) * LANES]
    hn = _rms(h, gffn_ref[...]).astype(BF16)

    logits = _dot(hn, wr_ref[...]) + br_ref[...]
    lane_i = _lane_iota(logits.shape)
    lane = lane_i.astype(F32)
    lane_group = (lane_i // EXPERTS_PER_GROUP).astype(F32)
    big = float(LANES)
    is_g = (lane_i >= N_EXPERTS) & (lane_i < N_EXPERTS + N_EXPERT_GROUPS)
    gl = jnp.where(is_g, logits, NEG_INF)
    gmax = jnp.max(gl, axis=-1, keepdims=True)
    g_idx = jnp.min(jnp.where(gl == gmax, lane, big), axis=-1, keepdims=True) - N_EXPERTS
    g_top = 1.0 / jnp.sum(jnp.where(is_g, jnp.exp(gl - gmax), 0.0), axis=-1, keepdims=True)
    in_group = (lane_i < N_EXPERTS) & (lane_group == g_idx)
    el = jnp.where(in_group, logits, NEG_INF)
    v1 = jnp.max(el, axis=-1, keepdims=True)
    i1 = jnp.min(jnp.where(el == v1, lane, big), axis=-1, keepdims=True)
    el2 = jnp.where(lane == i1, NEG_INF, el)
    v2 = jnp.max(el2, axis=-1, keepdims=True)
    i2 = jnp.min(jnp.where(el2 == v2, lane, big), axis=-1, keepdims=True)
    e21 = jnp.exp(v2 - v1)
    w1 = g_top / (1.0 + e21)
    w2 = g_top * e21 / (1.0 + e21)
    route_ref[0] = (jnp.where(lane_i == 0, i1, 0.0) + jnp.where(lane_i == 1, i2, 0.0)
                    + jnp.where(lane_i == 2, w1, 0.0) + jnp.where(lane_i == 3, w2, 0.0))


def _out_call(x, s5_tb, attn, w, *, tm):
    bsz, seq, dm = x.shape
    s5_width = s5_tb.shape[1] // bsz
    aw = attn.shape[2]
    full = lambda shape: pl.BlockSpec(shape, lambda b, i: (0,) * len(shape))
    tok = lambda width: pl.BlockSpec((1, tm, width), lambda b, i: (b, i, 0))
    return pl.pallas_call(
        _out_kernel,
        grid=(bsz, seq // tm),
        in_specs=[
            tok(dm),
            pl.BlockSpec((tm, s5_width), lambda b, i: (i, b)),
            tok(aw),
            full((1, s5_width)), full((1, aw)), full((s5_width + aw, dm)), full((1, dm)),
            full((dm, LANES)), full((1, LANES)),
        ],
        out_specs=[
            pl.BlockSpec((tm * SUBLANES, LANES), lambda b, i: (b * (seq // tm) + i, 0)),
            tok(LANES),
        ],
        out_shape=[
            jax.ShapeDtypeStruct((bsz * seq * SUBLANES, LANES), F32),
            jax.ShapeDtypeStruct((bsz, seq, LANES), F32),
        ],
        compiler_params=pltpu.CompilerParams(
            dimension_semantics=("parallel", "parallel"), vmem_limit_bytes=VMEM_LIMIT),
    )(x, s5_tb, attn, w['g_s5'], w['g_attn'], w['w_out'], w['g_ffn'], w['w_router'], w['b_router'])


def _moe_kernel(offs_ref, toks_ref, gws_ref, h_hbm, gffn_ref, wgu_ref, wd_ref, o_ref, acc, hn_ref, gbuf, ybuf, sem,
                *, d_expert, tb, mc, norm_rows):
    b = pl.program_id(0)
    e = pl.program_id(1)
    tok_rows = SUBLANES
    dm = tok_rows * LANES

    @pl.when(e == 0)
    def _():
        cp = pltpu.make_async_copy(h_hbm.at[pl.ds(b * tb * tok_rows, tb * tok_rows)], acc, sem)
        cp.start()
        gbuf[...] = jnp.zeros_like(gbuf)
        cp.wait()

        def norm(i, carry):
            r0 = pl.multiple_of(i * norm_rows * tok_rows, norm_rows * tok_rows)
            x3 = acc[pl.ds(r0, norm_rows * tok_rows), :].reshape(norm_rows, tok_rows, LANES)
            ms = jnp.sum(jnp.sum(x3 * x3, axis=2, keepdims=True), axis=1, keepdims=True) * (1.0 / dm)
            hn3 = x3 * lax.rsqrt(ms + EPS) * gffn_ref[...][None]
            hn_ref[pl.ds(r0, norm_rows * tok_rows), :] = hn3.reshape(norm_rows * tok_rows, LANES)
            return carry

        lax.fori_loop(0, tb // norm_rows, norm, 0)

    n_exp = wgu_ref.shape[0]
    offs = [offs_ref[0, 0, e * n_exp + k] for k in range(n_exp + 1)]
    cnts = [offs[k + 1] - offs[k] for k in range(n_exp)]

    def chunk(c, carry):
        bases = [offs[k] + c * mc for k in range(n_exp)]
        ns = [jnp.clip(cnts[k] - c * mc, 0, mc) for k in range(n_exp)]

        for k in range(n_exp):
            def gather8(g, carry, k=k):
                for u in range(SUBLANES):
                    i = g * SUBLANES + u
                    row = pl.multiple_of(toks_ref[0, 0, bases[k] + i], tok_rows)
                    gbuf[k, pl.ds(pl.multiple_of(i * tok_rows, tok_rows), tok_rows), :] = (
                        hn_ref[pl.ds(row, tok_rows), :])
                return carry

            lax.fori_loop(0, (ns[k] + SUBLANES - 1) // SUBLANES, gather8, 0)

        for k in range(n_exp):
            x = jnp.concatenate([gbuf[k, pl.ds(j, mc, stride=tok_rows), :].astype(BF16) for j in range(tok_rows)],
                                axis=-1)
            gu = _dot(x, wgu_ref[k])
            hdn = jax.nn.silu(gu[:, :d_expert]) * gu[:, d_expert:]
            y = _dot(hdn.astype(BF16), wd_ref[k])
            for j in range(tok_rows):
                ybuf[k, pl.ds(j, mc, stride=tok_rows), :] = y[:, j * LANES:(j + 1) * LANES]

        for k in range(n_exp):
            def add_rows(first, count, k=k):
                idx = [first + u for u in range(count)]
                dst = [pl.multiple_of(toks_ref[0, 0, bases[k] + i], tok_rows) for i in idx]
                val = [acc[pl.ds(d, tok_rows), :] + gws_ref[0, 0, bases[k] + i]
                       * ybuf[k, pl.ds(pl.multiple_of(i * tok_rows, tok_rows), tok_rows), :]
                       for i, d in zip(idx, dst)]
                for d, v in zip(dst, val):
                    acc[pl.ds(d, tok_rows), :] = v

            n_full = ns[k] // SUBLANES

            def scatter8(g, carry, add_rows=add_rows):
                add_rows(g * SUBLANES, SUBLANES)
                return carry

            def scatter1(i, carry, add_rows=add_rows):
                add_rows(i, 1)
                return carry

            lax.fori_loop(0, n_full, scatter8, 0)
            lax.fori_loop(n_full * SUBLANES, ns[k], scatter1, 0)
        return carry

    n_chunks = functools.reduce(jnp.maximum, [(cnt + mc - 1) // mc for cnt in cnts])
    lax.fori_loop(0, n_chunks, chunk, 0)

    @pl.when(e == pl.num_programs(1) - 1)
    def _():
        for j in range(tok_rows):
            o_ref[:, j * LANES:(j + 1) * LANES] = acc[pl.ds(j, tb, stride=tok_rows), :]


def _moe_call(offs, toks, gws, h_tiles, w, *, tb, mc):
    tok_rows = SUBLANES
    dm = tok_rows * LANES
    t = h_tiles.shape[0] // tok_rows
    assert w['w_gu'].shape[1] == dm, "a token row must be exactly one (8,128) f32 tile"
    nb = t // tb
    d_expert = w['w_d'].shape[1]
    kern = functools.partial(_moe_kernel, d_expert=d_expert, tb=tb, mc=mc, norm_rows=min(tb, 256))
    smem = lambda n: pl.BlockSpec((1, 1, n), lambda b, e: (b, 0, 0), memory_space=pltpu.SMEM)
    n_exp = MOE_EXPERTS_PER_STEP
    return pl.pallas_call(
        kern,
        grid=(nb, N_EXPERTS // n_exp),
        in_specs=[
            smem(offs.shape[2]), smem(toks.shape[2]), smem(gws.shape[2]),
            pl.BlockSpec(memory_space=pl.ANY),
            pl.BlockSpec((tok_rows, LANES), lambda b, e: (0, 0)),
            pl.BlockSpec((n_exp, dm, 2 * d_expert), lambda b, e: (e, 0, 0)),
            pl.BlockSpec((n_exp, d_expert, dm), lambda b, e: (e, 0, 0)),
        ],
        out_specs=pl.BlockSpec((tb, dm), lambda b, e: (b, 0)),
        out_shape=jax.ShapeDtypeStruct((t, dm), F32),
        scratch_shapes=[
            pltpu.VMEM((tb * tok_rows, LANES), F32),
            pltpu.VMEM((tb * tok_rows, LANES), F32),
            pltpu.VMEM((n_exp, mc * tok_rows, LANES), F32),
            pltpu.VMEM((n_exp, mc * tok_rows, LANES), F32),
            pltpu.SemaphoreType.DMA(()),
        ],
        compiler_params=pltpu.CompilerParams(
            dimension_semantics=("parallel", "arbitrary"), vmem_limit_bytes=VMEM_LIMIT),
    )(offs, toks, gws, h_tiles, w['g_ffn'].reshape(tok_rows, LANES), w['w_gu'], w['w_d'])


def _route_tables(route, tb):
    t = route.shape[0]
    nb = t // tb
    ids = route[:, :2].astype(jnp.int32).reshape(nb, 2 * tb)
    wts = route[:, 2:4].reshape(nb, 2 * tb)
    order = jnp.argsort(ids, axis=1, stable=True)
    toks = (order // 2 * SUBLANES).astype(jnp.int32)
    toks = jnp.pad(toks, ((0, 0), (0, SUBLANES)))
    gws = jnp.take_along_axis(wts, order, axis=1)
    counts = jnp.sum(ids[:, :, None] == jnp.arange(N_EXPERTS, dtype=jnp.int32), axis=1, dtype=jnp.int32)
    offs = jnp.concatenate([jnp.zeros((nb, 1), jnp.int32), jnp.cumsum(counts, axis=1)], axis=1)
    offs = jnp.pad(offs, ((0, 0), (0, LANES - offs.shape[1])))
    return offs[:, None, :], toks[:, None, :], gws[:, None, :]


def _prepare_weights(p):
    dm, in_cols = p['w_in'].shape
    s5_width = p['s5_d'].shape[0]
    q_lora = p['q_lora_norm_g'].shape[0]
    kv_lora = p['kv_lora_norm_g'].shape[0]
    half_r = QK_ROPE // 2
    w = {'s5_width': s5_width, 'q_lora': q_lora, 'kv_lora': kv_lora}

    def partner(a):
        return jnp.concatenate([-a[..., half_r:], a[..., :half_r]], axis=-1)

    def swap(a):
        return jnp.concatenate([a[..., half_r:], a[..., :half_r]], axis=-1)

    w_kr = p['w_in'][:, in_cols - QK_ROPE:]
    w['w_in'] = jnp.concatenate(
        [p['w_in'], partner(w_kr), jnp.zeros((dm, LANES - 2 * QK_ROPE), F32)], axis=-1).astype(BF16)
    w['g_mix'] = p['norm_mix_g'][None]
    w['g_q'] = p['q_lora_norm_g'][None]
    w['g_kv'] = p['kv_lora_norm_g'][None]
    gkr = p['k_rope_norm_g']
    w['g_kr'] = jnp.concatenate([gkr, swap(gkr), jnp.zeros((LANES - 2 * QK_ROPE,), F32)])[None]

    wq = p['w_uq']
    wq_r = wq[..., QK_NOPE:]
    w['w_uq'] = jnp.concatenate([wq, partner(wq_r)], axis=-1).reshape(q_lora, N_HEADS * HEAD_PAD).astype(BF16)
    gr = p['q_rope_norm_g']
    w['q_gain'] = jnp.tile(jnp.concatenate([p['q_nope_norm_g'], gr, swap(gr)]), N_HEADS)[None]
    idx = jnp.arange(HEAD_PAD)
    row_nope = (idx < QK_NOPE)[:, None]
    row_rope = ((idx >= QK_NOPE) & (idx < QK_DIM))[:, None]
    col_nope = (idx < QK_NOPE)[None, :]
    m_q = jnp.where(row_nope & col_nope, 1.0 / QK_NOPE, 0.0) + jnp.where(row_rope & ~col_nope, 1.0 / QK_ROPE, 0.0)
    m_k = jnp.where(row_nope, 1.0 / QK_NOPE, 0.0) * jnp.ones((1, HEAD_PAD), F32)
    eye2 = jnp.eye(2, dtype=F32)
    w['bd_q'] = jnp.kron(eye2, m_q).astype(BF16)
    w['bd_k'] = jnp.kron(eye2, m_k).astype(BF16)

    pad_h = lambda a: jnp.pad(a, ((0, 0), (0, 0), (0, HEAD_PAD - a.shape[-1])))
    w['w_uk'] = pad_h(p['w_uk']).reshape(kv_lora, N_HEADS * HEAD_PAD).astype(BF16)
    w['w_uv'] = pad_h(p['w_uv']).reshape(kv_lora, N_HEADS * HEAD_PAD).astype(BF16)
    w['k_gain'] = jnp.tile(jnp.pad(p['k_nope_norm_g'], (0, HEAD_PAD - QK_NOPE)), N_HEADS)[None]

    dt = jnp.exp(p['s5_log_dt'])[:, None]
    ar, ai = p['s5_a_re'], p['s5_a_im']
    mag = jnp.exp(dt * ar)
    abar_re = mag * jnp.cos(dt * ai)
    abar_im = mag * jnp.sin(dt * ai)
    den = ar * ar + ai * ai
    num_re = abar_re - 1.0
    coef_re = (num_re * ar + abar_im * ai) / den
    coef_im = (abar_im * ar - num_re * ai) / den
    br, bi = p['s5_b_re'], p['s5_b_im']
    bbar_re = coef_re[..., None] * br - coef_im[..., None] * bi
    bbar_im = coef_re[..., None] * bi + coef_im[..., None] * br
    g = ar.shape[0]
    n_state = g * S5_STATE
    half = n_state // 2
    eye_g = jnp.eye(g, dtype=F32)
    b_re = jnp.einsum('gni,gh->gihn', bbar_re, eye_g).reshape(s5_width, n_state)
    b_im = jnp.einsum('gni,gh->gihn', bbar_im, eye_g).reshape(s5_width, n_state)
    c_re = jnp.einsum('gon,gh->gnho', p['s5_c_re'], eye_g).reshape(n_state, s5_width)
    c_im = jnp.einsum('gon,gh->gnho', p['s5_c_im'], eye_g).reshape(n_state, s5_width)
    uw = s5_width // 2
    w['b_bd'] = jnp.stack([
        jnp.concatenate([b_re[c * uw:(c + 1) * uw, c * half:(c + 1) * half],
                         b_im[c * uw:(c + 1) * uw, c * half:(c + 1) * half]], axis=1) for c in range(2)]).astype(BF16)
    w['c_bd'] = jnp.stack([
        jnp.concatenate([c_re[c * half:(c + 1) * half, c * uw:(c + 1) * uw],
                         -c_im[c * half:(c + 1) * half, c * uw:(c + 1) * uw]], axis=0) for c in range(2)]).astype(BF16)
    w['lam_re'] = jnp.broadcast_to(abar_re.reshape(1, n_state), (SUBLANES, n_state))
    w['lam_im'] = jnp.broadcast_to(abar_im.reshape(1, n_state), (SUBLANES, n_state))
    w['s5_d'] = p['s5_d'][None]
    w['w_glu'] = p['s5_w_glu'].astype(BF16)
    w['b_glu'] = p['s5_b_glu'][None]

    w['g_s5'] = p['out_norm_s5_g'][None]
    w['g_attn'] = p['out_norm_attn_g'][None]
    w['w_out'] = p['w_out'].astype(BF16)
    w['g_ffn'] = p['norm_ffn_g'][None]
    n_r = N_EXPERTS + N_EXPERT_GROUPS
    w['w_router'] = jnp.pad(jnp.concatenate([p['w_router_expert'], p['w_router_group']], axis=1),
                            ((0, 0), (0, LANES - n_r))).astype(BF16)
    w['b_router'] = jnp.pad(jnp.concatenate([p['b_router_expert'], p['b_router_group']]), (0, LANES - n_r))[None]
    w['w_gu'] = jnp.concatenate([p['w_e_gate'], p['w_e_up']], axis=-1).astype(BF16)
    w['w_d'] = p['w_e_down'].astype(BF16)
    return w


def _rope_table(pos):
    half = QK_ROPE // 2
    inv = ROPE_THETA ** (-jnp.arange(half, dtype=F32) / half)
    ang = pos.astype(F32)[:, None] * inv[None, :]
    cc = jnp.tile(jnp.cos(ang), (1, 2))
    ss = jnp.tile(jnp.sin(ang), (1, 2))
    n = pos.shape[0]
    scale = QK_DIM ** -0.5 * math.log2(math.e)
    tab_q = scale * jnp.concatenate([jnp.ones((n, QK_NOPE), F32), cc, ss], axis=-1)
    tab_k = jnp.concatenate([cc, ss, jnp.zeros((n, LANES - 2 * QK_ROPE), F32)], axis=-1)
    return jnp.concatenate([tab_q, tab_k], axis=-1)


def _s5_state_in(re, im):
    b = re.shape[0]
    re = re.reshape(b, -1)
    im = im.reshape(b, -1)
    half = re.shape[1] // 2
    return jnp.concatenate([re[:, :half], im[:, :half], re[:, half:], im[:, half:]], axis=1)


def _s5_state_out(h, g):
    b = h.shape[0]
    half = h.shape[1] // 4
    re = jnp.concatenate([h[:, :half], h[:, 2 * half:3 * half]], axis=1).reshape(b, g, S5_STATE)
    im = jnp.concatenate([h[:, half:2 * half], h[:, 3 * half:]], axis=1).reshape(b, g, S5_STATE)
    return re, im


def _trunk_layer(x, past_ckv, past_krope, h0_re, h0_im, w):
    bsz, seq, dm = x.shape
    assert bsz == SUBLANES, "the S5 scan keeps the batch on the sublanes"
    past_len = 0 if past_ckv is None else past_ckv.shape[1]
    g = w['lam_re'].shape[1] // S5_STATE
    tm = _tile(seq, 512)

    tab = _rope_table(past_len + jnp.arange(seq, dtype=jnp.int32))
    u_tb, q, new_ckv, new_krope = _proj_call(x, tab, w, tm=tm)

    if h0_re is None:
        h0 = jnp.zeros((bsz, 2 * g * S5_STATE), F32)
    else:
        h0 = _s5_state_in(h0_re.astype(F32), h0_im.astype(F32))
    s5_tb, h_last = _s5_call(u_tb.reshape(seq * bsz, -1), h0, w, ts=min(seq, 64))
    h_re, h_im = _s5_state_out(h_last, g)

    if past_ckv is None:
        all_ckv, all_krope = new_ckv, new_krope
    else:
        all_ckv = jnp.concatenate([past_ckv.astype(F32), new_ckv], axis=1)
        all_krope = jnp.concatenate([past_krope.astype(F32), new_krope], axis=1)
    sk = all_ckv.shape[1]
    sk_pad = -(-sk // LANES) * LANES
    all_ckv = jnp.pad(all_ckv, ((0, 0), (0, sk_pad - sk), (0, 0)))
    krp = jnp.pad(all_krope, ((0, 0), (0, sk_pad - sk), (QK_NOPE, LANES - QK_DIM)))
    whole = sk_pad % ATTN_TK != 0
    k, vt = _kv_call(all_ckv, krp, w, tm=sk_pad if whole else KV_TILE)
    sq_pad = -(-seq // LANES) * LANES
    q = jnp.pad(q, ((0, 0), (0, 0), (0, sq_pad - seq), (0, 0)))
    tq = ATTN_TQ if sq_pad % ATTN_TQ == 0 else LANES
    attn = _attn_call(q, k, vt, tq=tq, tk=sk_pad if whole else ATTN_TK, q_pos0=past_len, sk_valid=sk)[:, :seq]

    h_tiles, route = _out_call(x, s5_tb.reshape(seq, -1), attn, w, tm=tm)
    t = bsz * seq
    tb = _tile(t, MOE_BLOCK)
    offs, toks, gws = _route_tables(route.reshape(t, LANES), tb)
    y = _moe_call(offs, toks, gws, h_tiles, w, tb=tb, mc=MOE_CHUNK)
    return y.reshape(bsz, seq, dm), new_ckv, new_krope, h_re, h_im


def kernel(x_prompt, x_sample, cache_ckv, cache_krope, state_s5_re, state_s5_im, norm_mix_g, w_in, s5_a_re, s5_a_im, s5_log_dt, s5_b_re, s5_b_im, s5_c_re, s5_c_im, s5_d, s5_w_glu, s5_b_glu, q_lora_norm_g, w_uq, kv_lora_norm_g, w_uk, w_uv, q_nope_norm_g, q_rope_norm_g, k_nope_norm_g, k_rope_norm_g, out_norm_s5_g, out_norm_attn_g, w_out, norm_ffn_g, w_router_group, b_router_group, w_router_expert, b_router_expert, w_e_gate, w_e_up, w_e_down):
    params = dict(
        norm_mix_g=norm_mix_g, w_in=w_in, s5_a_re=s5_a_re, s5_a_im=s5_a_im, s5_log_dt=s5_log_dt,
        s5_b_re=s5_b_re, s5_b_im=s5_b_im, s5_c_re=s5_c_re, s5_c_im=s5_c_im, s5_d=s5_d,
        s5_w_glu=s5_w_glu, s5_b_glu=s5_b_glu, q_lora_norm_g=q_lora_norm_g, w_uq=w_uq,
        kv_lora_norm_g=kv_lora_norm_g, w_uk=w_uk, w_uv=w_uv, q_nope_norm_g=q_nope_norm_g,
        q_rope_norm_g=q_rope_norm_g, k_nope_norm_g=k_nope_norm_g, k_rope_norm_g=k_rope_norm_g,
        out_norm_s5_g=out_norm_s5_g, out_norm_attn_g=out_norm_attn_g, w_out=w_out, norm_ffn_g=norm_ffn_g,
        w_router_group=w_router_group, b_router_group=b_router_group,
        w_router_expert=w_router_expert, b_router_expert=b_router_expert,
        w_e_gate=w_e_gate, w_e_up=w_e_up, w_e_down=w_e_down)
    depth = w_in.shape[0]
    y_p, y_s = x_prompt, x_sample
    outs = [[] for _ in range(8)]
    for l in range(depth):
        w = _prepare_weights({k: a[l] for k, a in params.items()})
        y_p, c1, k1, r1, i1 = _trunk_layer(y_p, None, None, None, None, w)
        y_s, c2, k2, r2, i2 = _trunk_layer(y_s, cache_ckv[l], cache_krope[l], state_s5_re[l], state_s5_im[l], w)
        for lst, a in zip(outs, (c1, k1, r1, i1, c2, k2, r2, i2)):
            lst.append(a)
    return (y_p, y_s) + tuple(jnp.stack(lst) for lst in outs)
```

```python
import functools
import math

import jax
import jax.numpy as jnp
from jax import lax
from jax.experimental import pallas as pl
from jax.experimental.pallas import tpu as pltpu

F32 = jnp.float32
BF16 = jnp.bfloat16

CHUNK = 64
S5_GROUP = 16
S5_STATE = 64
N_HEADS = 8
QK_NOPE = 64
QK_ROPE = 32
QK_DIM = QK_NOPE + QK_ROPE
V_DIM = 64
ROPE_THETA = 10000.0
N_EXPERT_GROUPS = 4
EXPERTS_PER_GROUP = 8
N_EXPERTS = N_EXPERT_GROUPS * EXPERTS_PER_GROUP
EPS = 1e-6
NEG_INF = -1e30

LANES = 128
SUBLANES = 8
HEAD_PAD = LANES
ATTN_TQ = 512
ATTN_TK = 512
KV_TILE = 512
MOE_BLOCK = 2048
MOE_CHUNK = 160
MOE_EXPERTS_PER_STEP = 2
REF_MARGIN = 1.02
REF_LIMIT = 40.0
ATTN_SPAN = 2
ATTN_COLS = 256
VT_ROWS = 80
VMEM_LIMIT = 48 * 1024 * 1024


def _tile(n, cap):
    for t in range(min(cap, n), 15, -1):
        if n % t == 0 and t % 16 == 0:
            return t
    return n


def _rms(x, g):
    return x * lax.rsqrt(jnp.mean(x * x, axis=-1, keepdims=True) + EPS) * g


def _dot(a, b):
    return jnp.dot(a, b, preferred_element_type=F32)


def _lane_iota(shape):
    return lax.broadcasted_iota(jnp.int32, shape, len(shape) - 1)


def _proj_kernel(x_ref, gmix_ref, win_ref, gq_ref, wuq_ref, bdq_ref, qgain_ref, gkv_ref, gkr_ref, tab_ref,
                 u_ref, q_ref, ckv_ref, kr_ref, *, s5_width, q_lora, kv_lora):
    x = x_ref[0]
    xn = _rms(x, gmix_ref[...])
    proj = _dot(xn.astype(BF16), win_ref[...])
    u_ref[0] = proj[:, :s5_width].astype(BF16)

    o = s5_width
    cqn = _rms(proj[:, o:o + q_lora], gq_ref[...])
    o += q_lora
    ckv_ref[0] = _rms(proj[:, o:o + kv_lora], gkv_ref[...])
    o += kv_lora

    tab = tab_ref[...]
    lane = _lane_iota((x.shape[0], LANES))

    kr = proj[:, o:o + LANES]
    ms = jnp.sum(jnp.where(lane < QK_ROPE, kr * kr, 0.0), axis=-1, keepdims=True) * (1.0 / QK_ROPE)
    t = kr * lax.rsqrt(ms + EPS) * gkr_ref[...] * tab[:, LANES:]
    kro = t + pltpu.roll(t, LANES - QK_ROPE, axis=1)
    kr_ref[0] = kro[:, :QK_ROPE]

    qa = _dot(cqn.astype(BF16), wuq_ref[...])
    qsq = (qa * qa).astype(BF16)
    pair = 2 * HEAD_PAD
    ms = jnp.concatenate([_dot(qsq[:, c * pair:(c + 1) * pair], bdq_ref[...])
                          for c in range(N_HEADS // 2)], axis=-1)
    qn = qa * lax.rsqrt(ms + EPS) * qgain_ref[...]
    is_rope = (lane >= QK_NOPE) & (lane < QK_DIM)
    for h in range(N_HEADS):
        th = qn[:, h * HEAD_PAD:(h + 1) * HEAD_PAD] * tab[:, :LANES]
        rolled = pltpu.roll(th, LANES - QK_ROPE, axis=1)
        oh = jnp.where(lane < QK_DIM, th + jnp.where(is_rope, rolled, 0.0), 0.0)
        q_ref[0, h] = oh.astype(BF16)


def _proj_call(x, tab, w, *, tm):
    bsz, seq, dm = x.shape
    s5_width, q_lora, kv_lora = w['s5_width'], w['q_lora'], w['kv_lora']
    in_cols = w['w_in'].shape[1]
    hp = N_HEADS * HEAD_PAD
    full = lambda shape: pl.BlockSpec(shape, lambda b, i: (0,) * len(shape))
    kern = functools.partial(_proj_kernel, s5_width=s5_width, q_lora=q_lora, kv_lora=kv_lora)
    return pl.pallas_call(
        kern,
        grid=(bsz, seq // tm),
        in_specs=[
            pl.BlockSpec((1, tm, dm), lambda b, i: (b, i, 0)),
            full((1, dm)), full((dm, in_cols)), full((1, q_lora)), full((q_lora, hp)),
            full((2 * HEAD_PAD, 2 * HEAD_PAD)), full((1, hp)), full((1, kv_lora)), full((1, LANES)),
            pl.BlockSpec((tm, 2 * LANES), lambda b, i: (i, 0)),
        ],
        out_specs=[
            pl.BlockSpec((1, tm, s5_width), lambda b, i: (b, i, 0)),
            pl.BlockSpec((1, N_HEADS, tm, HEAD_PAD), lambda b, i: (b, 0, i, 0)),
            pl.BlockSpec((1, tm, kv_lora), lambda b, i: (b, i, 0)),
            pl.BlockSpec((1, tm, QK_ROPE), lambda b, i: (b, i, 0)),
        ],
        out_shape=[
            jax.ShapeDtypeStruct((bsz, seq, s5_width), BF16),
            jax.ShapeDtypeStruct((bsz, N_HEADS, seq, HEAD_PAD), BF16),
            jax.ShapeDtypeStruct((bsz, seq, kv_lora), F32),
            jax.ShapeDtypeStruct((bsz, seq, QK_ROPE), F32),
        ],
        compiler_params=pltpu.CompilerParams(
            dimension_semantics=("parallel", "parallel"), vmem_limit_bytes=VMEM_LIMIT),
    )(x, w['g_mix'], w['w_in'], w['g_q'], w['w_uq'], w['bd_q'], w['q_gain'], w['g_kv'], w['g_kr'], tab)


def _kv_kernel(ckv_ref, krp_ref, wuk_ref, wuv_ref, bdk_ref, kg_ref, k_ref, v_ref):
    c = ckv_ref[0].astype(BF16)
    ka = _dot(c, wuk_ref[...])
    va = _dot(c, wuv_ref[...])
    ksq = (ka * ka).astype(BF16)
    pair = 2 * HEAD_PAD
    ms = jnp.concatenate([_dot(ksq[:, c2 * pair:(c2 + 1) * pair], bdk_ref[...])
                          for c2 in range(N_HEADS // 2)], axis=-1)
    kn = ka * lax.rsqrt(ms + EPS) * kg_ref[...]
    krp = krp_ref[0]
    lane = _lane_iota(krp.shape)
    ones_col = jnp.where(lane == V_DIM, 1.0, 0.0)
    for h in range(N_HEADS):
        k_ref[0, h] = (kn[:, h * HEAD_PAD:(h + 1) * HEAD_PAD] + krp).astype(BF16)
        vt = (va[:, h * HEAD_PAD:(h + 1) * HEAD_PAD] + ones_col).T
        v_ref[0, h] = vt[:VT_ROWS].astype(BF16)


def _kv_call(ckv, krp, w, *, tm):
    bsz, sk, kv_lora = ckv.shape
    hp = N_HEADS * HEAD_PAD
    full = lambda shape: pl.BlockSpec(shape, lambda b, i: (0,) * len(shape))
    return pl.pallas_call(
        _kv_kernel,
        grid=(bsz, sk // tm),
        in_specs=[
            pl.BlockSpec((1, tm, kv_lora), lambda b, i: (b, i, 0)),
            pl.BlockSpec((1, tm, LANES), lambda b, i: (b, i, 0)),
            full((kv_lora, hp)), full((kv_lora, hp)), full((2 * HEAD_PAD, 2 * HEAD_PAD)), full((1, hp)),
        ],
        out_specs=[
            pl.BlockSpec((1, N_HEADS, tm, HEAD_PAD), lambda b, i: (b, 0, i, 0)),
            pl.BlockSpec((1, N_HEADS, VT_ROWS, tm), lambda b, i: (b, 0, 0, i)),
        ],
        out_shape=[
            jax.ShapeDtypeStruct((bsz, N_HEADS, sk, HEAD_PAD), BF16),
            jax.ShapeDtypeStruct((bsz, N_HEADS, VT_ROWS, sk), BF16),
        ],
        compiler_params=pltpu.CompilerParams(
            dimension_semantics=("parallel", "parallel"), vmem_limit_bytes=VMEM_LIMIT),
    )(ckv, krp, w['w_uk'], w['w_uv'], w['bd_k'], w['k_gain'])


def _attn_kernel(q_ref, k_ref, vt_ref, o_ref, *, tq, tk, q_pos0, sk_valid):
    sq = q_ref.shape[2]
    nq = sq // tq

    n_heads = q_ref.shape[1]

    cw = min(tq, ATTN_COLS)
    streams = [(hh, c) for hh in range(n_heads) for c in range(tq // cw)]
    sk = k_ref.shape[2]
    nt = (((1,), (1,)), ((), ()))
    ones_rows = jnp.ones((SUBLANES, HEAD_PAD), BF16)

    def sq_norms(x):
        xf = x.astype(F32)
        return lax.dot_general(ones_rows, (xf * xf).astype(BF16), nt, preferred_element_type=F32)[:1]

    def kmax_body(j, carry):
        k0 = pl.multiple_of(j * tk, tk)
        return tuple(jnp.maximum(carry[hh], sq_norms(k_ref[0, hh, pl.ds(k0, tk), :])) for hh in range(n_heads))

    kmax2 = lax.fori_loop(0, sk // tk, kmax_body, tuple(jnp.zeros((1, tk), F32) for _ in range(n_heads)))
    kmax2 = [jnp.max(v, axis=1, keepdims=True) for v in kmax2]

    def scores(qs, p0, j, width, masked):
        k0 = pl.multiple_of(j * width, width)
        kbs = [k_ref[0, hh, pl.ds(k0, width), :] for hh in range(n_heads)]
        ss = [lax.dot_general(kbs[hh], qs[i], nt, preferred_element_type=F32)
              for i, (hh, c) in enumerate(streams)]
        if masked:
            k_pos = k0 + lax.broadcasted_iota(jnp.int32, (width, 1), 0)
            for c in range(tq // cw):
                q_pos = p0 + c * cw + lax.broadcasted_iota(jnp.int32, (1, cw), 1)
                visible = k_pos < jnp.minimum((q_pos // CHUNK + 1) * CHUNK, sk_valid)
                for i, (hh, c2) in enumerate(streams):
                    if c2 == c:
                        ss[i] = jnp.where(visible, ss[i], NEG_INF)
        return ss

    def values(j, width):
        k0 = pl.multiple_of(j * width, width)
        return [vt_ref[0, hh, :, pl.ds(k0, width)] for hh in range(n_heads)]

    def step_running_max(qs, p0, j, carry, masked):
        ss, vts = scores(qs, p0, j, tk, masked), values(j, tk)
        new = []
        for i, (hh, c) in enumerate(streams):
            m, acc = carry[i]
            m_new = jnp.maximum(m, jnp.max(ss[i], axis=0, keepdims=True))
            acc = jnp.exp2(m - m_new) * acc + _dot(vts[hh], jnp.exp2(ss[i] - m_new).astype(BF16))
            new.append((m_new, acc))
        return tuple(new)

    def step_fixed_ref(qs, refs, p0, j, width, carry, masked):
        ss, vts = scores(qs, p0, j, width, masked), values(j, width)
        return tuple(carry[i] + _dot(vts[hh], jnp.exp2(ss[i] - refs[i]).astype(BF16))
                     for i, (hh, c) in enumerate(streams))

    def q_body(qi, carry):
        q0 = pl.multiple_of(qi * tq, tq)
        qs = [q_ref[0, hh, pl.ds(q0 + c * cw, cw), :] for hh, c in streams]
        p0 = q_pos0 + q0
        n_open = jnp.minimum((p0 // CHUNK + 1) * CHUNK, sk_valid) // tk
        v1 = jnp.minimum(((p0 + tq - 1) // CHUNK + 1) * CHUNK, sk_valid)
        n_all = (v1 + tk - 1) // tk
        refs = [jnp.sqrt(sq_norms(qs[i]) * kmax2[hh]) * REF_MARGIN for i, (hh, c) in enumerate(streams)]
        ref_max = functools.reduce(jnp.maximum, [jnp.max(r) for r in refs])

        def fixed_ref():
            span = ATTN_SPAN if ATTN_SPAN * tk <= sk else 1
            n_wide = n_open // span
            st = tuple(jnp.zeros((VT_ROWS, cw), F32) for _ in streams)
            if span > 1:
                st = lax.fori_loop(0, n_wide,
                                   lambda j, c: step_fixed_ref(qs, refs, p0, j, span * tk, c, False), st)
            st = lax.fori_loop(n_wide * span if span > 1 else 0, n_open,
                               lambda j, c: step_fixed_ref(qs, refs, p0, j, tk, c, False), st)
            return lax.fori_loop(n_open, n_all, lambda j, c: step_fixed_ref(qs, refs, p0, j, tk, c, True), st)

        def running_max():
            st = tuple((jnp.full((1, cw), NEG_INF, F32), jnp.zeros((VT_ROWS, cw), F32)) for _ in streams)
            st = lax.fori_loop(0, n_open, lambda j, c: step_running_max(qs, p0, j, c, False), st)
            st = lax.fori_loop(n_open, n_all, lambda j, c: step_running_max(qs, p0, j, c, True), st)
            return tuple(acc for _, acc in st)

        accs = lax.cond(ref_max < REF_LIMIT, fixed_ref, running_max)
        for c in range(tq // cw):
            ot = jnp.concatenate([accs[i][:V_DIM] / accs[i][V_DIM:V_DIM + 1]
                                  for i, (hh, c2) in enumerate(streams) if c2 == c], axis=0)
            o_ref[0, pl.ds(q0 + c * cw, cw), :] = ot.T.astype(o_ref.dtype)
        return carry

    lax.fori_loop(0, nq, q_body, 0)


def _attn_call(q, k, vt, *, tq, tk, q_pos0, sk_valid):
    bsz, nh, sq, _ = q.shape
    sk = k.shape[2]
    assert sq % tq == 0 and sk % tk == 0 and tq % LANES == 0 and tk % LANES == 0
    kern = functools.partial(_attn_kernel, tq=tq, tk=tk, q_pos0=q_pos0, sk_valid=sk_valid)
    return pl.pallas_call(
        kern,
        grid=(bsz, nh // 2),
        in_specs=[
            pl.BlockSpec((1, 2, sq, HEAD_PAD), lambda b, h: (b, h, 0, 0)),
            pl.BlockSpec((1, 2, sk, HEAD_PAD), lambda b, h: (b, h, 0, 0)),
            pl.BlockSpec((1, 2, VT_ROWS, sk), lambda b, h: (b, h, 0, 0)),
        ],
        out_specs=pl.BlockSpec((1, sq, 2 * V_DIM), lambda b, h: (b, 0, h)),
        out_shape=jax.ShapeDtypeStruct((bsz, sq, nh * V_DIM), BF16),
        compiler_params=pltpu.CompilerParams(
            dimension_semantics=("parallel", "parallel"), vmem_limit_bytes=VMEM_LIMIT),
    )(q, k, vt)


def _s5_kernel(u_ref, h0_ref, lre_ref, lim_ref, bbd_ref, cbd_ref, d_ref, wglu_ref, bglu_ref,
               y_ref, hout_ref, hbuf, hstate, u_tb, y_tb, *, ts, n_half, col_chunk):
    i = pl.program_id(0)
    half = n_half
    width = hbuf.shape[1]
    bsz, _, s5_width = u_ref.shape
    uw = s5_width // 2
    n_slab = s5_width // LANES

    @pl.when(i == 0)
    def _():
        hstate[...] = h0_ref[...]

    for b in range(bsz):
        ub = u_ref[b].astype(F32)
        for c in range(n_slab):
            u_tb[c, pl.ds(b, ts, stride=SUBLANES), :] = ub[:, c * LANES:(c + 1) * LANES]
    u32 = jnp.concatenate([u_tb[c] for c in range(n_slab)], axis=-1)
    u = u32.astype(BF16)
    for c in range(2):
        hbuf[:, c * 2 * half:(c + 1) * 2 * half] = _dot(u[:, c * uw:(c + 1) * uw], bbd_ref[c])

    for c in range(2):
        for cc in range(half // col_chunk):
            re0 = c * 2 * half + cc * col_chunk
            im0 = re0 + half
            l0 = c * half + cc * col_chunk
            ar = lre_ref[:, l0:l0 + col_chunk]
            ai = lim_ref[:, l0:l0 + col_chunk]

            def step(s, carry, re0=re0, im0=im0, ar=ar, ai=ai):
                hr, hi = carry
                r0 = pl.multiple_of(s * SUBLANES, SUBLANES)
                nr = ar * hr - ai * hi + hbuf[pl.ds(r0, SUBLANES), re0:re0 + col_chunk]
                ni = ar * hi + ai * hr + hbuf[pl.ds(r0, SUBLANES), im0:im0 + col_chunk]
                hbuf[pl.ds(r0, SUBLANES), re0:re0 + col_chunk] = nr
                hbuf[pl.ds(r0, SUBLANES), im0:im0 + col_chunk] = ni
                return nr, ni

            hr, hi = lax.fori_loop(
                0, ts, step,
                (hstate[:, re0:re0 + col_chunk], hstate[:, im0:im0 + col_chunk]), unroll=8)
            hstate[:, re0:re0 + col_chunk] = hr
            hstate[:, im0:im0 + col_chunk] = hi

    hout_ref[...] = hstate[...]

    y = jnp.concatenate(
        [_dot(hbuf[:, c * 2 * half:(c + 1) * 2 * half].astype(BF16), cbd_ref[c]) for c in range(2)], axis=-1)
    y = y + d_ref[...] * u32
    z = jax.nn.gelu(y)
    gate = _dot(z.astype(BF16), wglu_ref[...]) + bglu_ref[...]
    out = z * jax.nn.sigmoid(gate)
    for c in range(n_slab):
        y_tb[c] = out[:, c * LANES:(c + 1) * LANES]
    for b in range(bsz):
        y_ref[b] = jnp.concatenate([y_tb[c, pl.ds(b, ts, stride=SUBLANES), :] for c in range(n_slab)],
                                   axis=-1).astype(y_ref.dtype)


def _s5_call(u, h0, w, *, ts):
    bsz, seq, s5_width = u.shape
    assert bsz == SUBLANES, "the S5 scan keeps the batch on the sublanes"
    n_state = w['lam_re'].shape[1]
    half = n_state // 2
    width = 2 * n_state
    n_steps = seq // ts
    full = lambda shape: pl.BlockSpec(shape, lambda i: (0,) * len(shape))
    kern = functools.partial(_s5_kernel, ts=ts, n_half=half, col_chunk=512)
    return pl.pallas_call(
        kern,
        grid=(n_steps,),
        in_specs=[
            pl.BlockSpec((bsz, ts, s5_width), lambda i: (0, i, 0)),
            full((SUBLANES, width)), full((SUBLANES, n_state)), full((SUBLANES, n_state)),
            full((2, s5_width // 2, 2 * half)), full((2, 2 * half, s5_width // 2)),
            full((1, s5_width)), full((s5_width, s5_width)), full((1, s5_width)),
        ],
        out_specs=[
            pl.BlockSpec((bsz, ts, s5_width), lambda i: (0, i, 0)),
            full((SUBLANES, width)),
        ],
        out_shape=[
            jax.ShapeDtypeStruct((bsz, seq, s5_width), BF16),
            jax.ShapeDtypeStruct((SUBLANES, width), F32),
        ],
        scratch_shapes=[
            pltpu.VMEM((ts * SUBLANES, width), F32),
            pltpu.VMEM((SUBLANES, width), F32),
            pltpu.VMEM((s5_width // LANES, ts * SUBLANES, LANES), F32),
            pltpu.VMEM((s5_width // LANES, ts * SUBLANES, LANES), F32),
        ],
        compiler_params=pltpu.CompilerParams(
            dimension_semantics=("arbitrary",), vmem_limit_bytes=VMEM_LIMIT),
    )(u, h0, w['lam_re'], w['lam_im'], w['b_bd'], w['c_bd'], w['s5_d'], w['w_glu'], w['b_glu'])


def _out_kernel(x_ref, s5_ref, at_ref, gs5_ref, gat_ref, wout_ref, gffn_ref, wr_ref, br_ref,
                h_ref, route_ref):
    s5n = _rms(s5_ref[0].astype(F32), gs5_ref[...])
    atn = _rms(at_ref[0].astype(F32), gat_ref[...])
    merged = jnp.concatenate([s5n, atn], axis=-1).astype(BF16)
    h = x_ref[0] + _dot(merged, wout_ref[...])
    tm = h.shape[0]
    for j in range(h.shape[1] // LANES):
        h_ref[pl.ds(j, tm, stride=SUBLANES), :] = h[:, j * LANES:(j + 1) * LANES]
    hn = _rms(h, gffn_ref[...]).astype(BF16)

    logits = _dot(hn, wr_ref[...]) + br_ref[...]
    lane_i = _lane_iota(logits.shape)
    lane = lane_i.astype(F32)
    lane_group = (lane_i // EXPERTS_PER_GROUP).astype(F32)
    big = float(LANES)
    is_g = (lane_i >= N_EXPERTS) & (lane_i < N_EXPERTS + N_EXPERT_GROUPS)
    gl = jnp.where(is_g, logits, NEG_INF)
    gmax = jnp.max(gl, axis=-1, keepdims=True)
    g_idx = jnp.min(jnp.where(gl == gmax, lane, big), axis=-1, keepdims=True) - N_EXPERTS
    g_top = 1.0 / jnp.sum(jnp.where(is_g, jnp.exp(gl - gmax), 0.0), axis=-1, keepdims=True)
    in_group = (lane_i < N_EXPERTS) & (lane_group == g_idx)
    el = jnp.where(in_group, logits, NEG_INF)
    v1 = jnp.max(el, axis=-1, keepdims=True)
    i1 = jnp.min(jnp.where(el == v1, lane, big), axis=-1, keepdims=True)
    el2 = jnp.where(lane == i1, NEG_INF, el)
    v2 = jnp.max(el2, axis=-1, keepdims=True)
    i2 = jnp.min(jnp.where(el2 == v2, lane, big), axis=-1, keepdims=True)
    e21 = jnp.exp(v2 - v1)
    w1 = g_top / (1.0 + e21)
    w2 = g_top * e21 / (1.0 + e21)
    route_ref[0] = (jnp.where(lane_i == 0, i1, 0.0) + jnp.where(lane_i == 1, i2, 0.0)
                    + jnp.where(lane_i == 2, w1, 0.0) + jnp.where(lane_i == 3, w2, 0.0))


def _out_call(x, s5_out, attn, w, *, tm):
    bsz, seq, dm = x.shape
    s5_width = s5_out.shape[2]
    aw = attn.shape[2]
    full = lambda shape: pl.BlockSpec(shape, lambda b, i: (0,) * len(shape))
    tok = lambda width: pl.BlockSpec((1, tm, width), lambda b, i: (b, i, 0))
    return pl.pallas_call(
        _out_kernel,
        grid=(bsz, seq // tm),
        in_specs=[
            tok(dm),
            tok(s5_width),
            tok(aw),
            full((1, s5_width)), full((1, aw)), full((s5_width + aw, dm)), full((1, dm)),
            full((dm, LANES)), full((1, LANES)),
        ],
        out_specs=[
            pl.BlockSpec((tm * SUBLANES, LANES), lambda b, i: (b * (seq // tm) + i, 0)),
            tok(LANES),
        ],
        out_shape=[
            jax.ShapeDtypeStruct((bsz * seq * SUBLANES, LANES), F32),
            jax.ShapeDtypeStruct((bsz, seq, LANES), F32),
        ],
        compiler_params=pltpu.CompilerParams(
            dimension_semantics=("parallel", "parallel"), vmem_limit_bytes=VMEM_LIMIT),
    )(x, s5_out, attn, w['g_s5'], w['g_attn'], w['w_out'], w['g_ffn'], w['w_router'], w['b_router'])


def _moe_kernel(offs_ref, toks_ref, gws_ref, h_hbm, gffn_ref, wgu_ref, wd_ref, o_ref, acc, hn_ref, gbuf, ybuf, sem,
                *, d_expert, tb, mc, norm_rows):
    b = pl.program_id(0)
    e = pl.program_id(1)
    tok_rows = SUBLANES
    dm = tok_rows * LANES

    @pl.when(e == 0)
    def _():
        cp = pltpu.make_async_copy(h_hbm.at[pl.ds(b * tb * tok_rows, tb * tok_rows)], acc, sem)
        cp.start()
        gbuf[...] = jnp.zeros_like(gbuf)
        cp.wait()

        def norm(i, carry):
            r0 = pl.multiple_of(i * norm_rows * tok_rows, norm_rows * tok_rows)
            x3 = acc[pl.ds(r0, norm_rows * tok_rows), :].reshape(norm_rows, tok_rows, LANES)
            ms = jnp.sum(jnp.sum(x3 * x3, axis=2, keepdims=True), axis=1, keepdims=True) * (1.0 / dm)
            hn3 = x3 * lax.rsqrt(ms + EPS) * gffn_ref[...][None]
            hn_ref[pl.ds(r0, norm_rows * tok_rows), :] = hn3.reshape(norm_rows * tok_rows, LANES)
            return carry

        lax.fori_loop(0, tb // norm_rows, norm, 0)

    n_exp = wgu_ref.shape[0]
    offs = [offs_ref[0, 0, e * n_exp + k] for k in range(n_exp + 1)]
    cnts = [offs[k + 1] - offs[k] for k in range(n_exp)]

    def chunk(c, carry):
        bases = [offs[k] + c * mc for k in range(n_exp)]
        ns = [jnp.clip(cnts[k] - c * mc, 0, mc) for k in range(n_exp)]

        for k in range(n_exp):
            def gather8(g, carry, k=k):
                for u in range(SUBLANES):
                    i = g * SUBLANES + u
                    row = pl.multiple_of(toks_ref[0, 0, bases[k] + i], tok_rows)
                    gbuf[k, pl.ds(pl.multiple_of(i * tok_rows, tok_rows), tok_rows), :] = (
                        hn_ref[pl.ds(row, tok_rows), :])
                return carry

            lax.fori_loop(0, (ns[k] + SUBLANES - 1) // SUBLANES, gather8, 0)

        for k in range(n_exp):
            x = jnp.concatenate([gbuf[k, pl.ds(j, mc, stride=tok_rows), :].astype(BF16) for j in range(tok_rows)],
                                axis=-1)
            gu = _dot(x, wgu_ref[k])
            hdn = jax.nn.silu(gu[:, :d_expert]) * gu[:, d_expert:]
            y = _dot(hdn.astype(BF16), wd_ref[k])
            for j in range(tok_rows):
                ybuf[k, pl.ds(j, mc, stride=tok_rows), :] = y[:, j * LANES:(j + 1) * LANES]

        for k in range(n_exp):
            def add_rows(first, count, k=k):
                idx = [first + u for u in range(count)]
                dst = [pl.multiple_of(toks_ref[0, 0, bases[k] + i], tok_rows) for i in idx]
                val = [acc[pl.ds(d, tok_rows), :] + gws_ref[0, 0, bases[k] + i]
                       * ybuf[k, pl.ds(pl.multiple_of(i * tok_rows, tok_rows), tok_rows), :]
                       for i, d in zip(idx, dst)]
                for d, v in zip(dst, val):
                    acc[pl.ds(d, tok_rows), :] = v

            n_full = ns[k] // SUBLANES

            def scatter8(g, carry, add_rows=add_rows):
                add_rows(g * SUBLANES, SUBLANES)
                return carry

            def scatter1(i, carry, add_rows=add_rows):
                add_rows(i, 1)
                return carry

            lax.fori_loop(0, n_full, scatter8, 0)
            lax.fori_loop(n_full * SUBLANES, ns[k], scatter1, 0)
        return carry

    n_chunks = functools.reduce(jnp.maximum, [(cnt + mc - 1) // mc for cnt in cnts])
    lax.fori_loop(0, n_chunks, chunk, 0)

    @pl.when(e == pl.num_programs(1) - 1)
    def _():
        for j in range(tok_rows):
            o_ref[:, j * LANES:(j + 1) * LANES] = acc[pl.ds(j, tb, stride=tok_rows), :]


def _moe_call(offs, toks, gws, h_tiles, w, *, tb, mc):
    tok_rows = SUBLANES
    dm = tok_rows * LANES
    t = h_tiles.shape[0] // tok_rows
    assert w['w_gu'].shape[1] == dm, "a token row must be exactly one (8,128) f32 tile"
    nb = t // tb
    d_expert = w['w_d'].shape[1]
    kern = functools.partial(_moe_kernel, d_expert=d_expert, tb=tb, mc=mc, norm_rows=min(tb, 256))
    smem = lambda n: pl.BlockSpec((1, 1, n), lambda b, e: (b, 0, 0), memory_space=pltpu.SMEM)
    n_exp = MOE_EXPERTS_PER_STEP
    return pl.pallas_call(
        kern,
        grid=(nb, N_EXPERTS // n_exp),
        in_specs=[
            smem(offs.shape[2]), smem(toks.shape[2]), smem(gws.shape[2]),
            pl.BlockSpec(memory_space=pl.ANY),
            pl.BlockSpec((tok_rows, LANES), lambda b, e: (0, 0)),
            pl.BlockSpec((n_exp, dm, 2 * d_expert), lambda b, e: (e, 0, 0)),
            pl.BlockSpec((n_exp, d_expert, dm), lambda b, e: (e, 0, 0)),
        ],
        out_specs=pl.BlockSpec((tb, dm), lambda b, e: (b, 0)),
        out_shape=jax.ShapeDtypeStruct((t, dm), F32),
        scratch_shapes=[
            pltpu.VMEM((tb * tok_rows, LANES), F32),
            pltpu.VMEM((tb * tok_rows, LANES), F32),
            pltpu.VMEM((n_exp, mc * tok_rows, LANES), F32),
            pltpu.VMEM((n_exp, mc * tok_rows, LANES), F32),
            pltpu.SemaphoreType.DMA(()),
        ],
        compiler_params=pltpu.CompilerParams(
            dimension_semantics=("parallel", "arbitrary"), vmem_limit_bytes=VMEM_LIMIT),
    )(offs, toks, gws, h_tiles, w['g_ffn'].reshape(tok_rows, LANES), w['w_gu'], w['w_d'])


def _route_tables(route, tb):
    t = route.shape[0]
    nb = t // tb
    ids = route[:, :2].astype(jnp.int32).reshape(nb, 2 * tb)
    wts = route[:, 2:4].reshape(nb, 2 * tb)
    order = jnp.argsort(ids, axis=1, stable=True)
    toks = (order // 2 * SUBLANES).astype(jnp.int32)
    toks = jnp.pad(toks, ((0, 0), (0, SUBLANES)))
    gws = jnp.take_along_axis(wts, order, axis=1)
    counts = jnp.sum(ids[:, :, None] == jnp.arange(N_EXPERTS, dtype=jnp.int32), axis=1, dtype=jnp.int32)
    offs = jnp.concatenate([jnp.zeros((nb, 1), jnp.int32), jnp.cumsum(counts, axis=1)], axis=1)
    offs = jnp.pad(offs, ((0, 0), (0, LANES - offs.shape[1])))
    return offs[:, None, :], toks[:, None, :], gws[:, None, :]


def _prepare_weights(p):
    dm, in_cols = p['w_in'].shape
    s5_width = p['s5_d'].shape[0]
    q_lora = p['q_lora_norm_g'].shape[0]
    kv_lora = p['kv_lora_norm_g'].shape[0]
    half_r = QK_ROPE // 2
    w = {'s5_width': s5_width, 'q_lora': q_lora, 'kv_lora': kv_lora}

    def partner(a):
        return jnp.concatenate([-a[..., half_r:], a[..., :half_r]], axis=-1)

    def swap(a):
        return jnp.concatenate([a[..., half_r:], a[..., :half_r]], axis=-1)

    w_kr = p['w_in'][:, in_cols - QK_ROPE:]
    w['w_in'] = jnp.concatenate(
        [p['w_in'], partner(w_kr), jnp.zeros((dm, LANES - 2 * QK_ROPE), F32)], axis=-1).astype(BF16)
    w['g_mix'] = p['norm_mix_g'][None]
    w['g_q'] = p['q_lora_norm_g'][None]
    w['g_kv'] = p['kv_lora_norm_g'][None]
    gkr = p['k_rope_norm_g']
    w['g_kr'] = jnp.concatenate([gkr, swap(gkr), jnp.zeros((LANES - 2 * QK_ROPE,), F32)])[None]

    wq = p['w_uq']
    wq_r = wq[..., QK_NOPE:]
    w['w_uq'] = jnp.concatenate([wq, partner(wq_r)], axis=-1).reshape(q_lora, N_HEADS * HEAD_PAD).astype(BF16)
    gr = p['q_rope_norm_g']
    w['q_gain'] = jnp.tile(jnp.concatenate([p['q_nope_norm_g'], gr, swap(gr)]), N_HEADS)[None]
    idx = jnp.arange(HEAD_PAD)
    row_nope = (idx < QK_NOPE)[:, None]
    row_rope = ((idx >= QK_NOPE) & (idx < QK_DIM))[:, None]
    col_nope = (idx < QK_NOPE)[None, :]
    m_q = jnp.where(row_nope & col_nope, 1.0 / QK_NOPE, 0.0) + jnp.where(row_rope & ~col_nope, 1.0 / QK_ROPE, 0.0)
    m_k = jnp.where(row_nope, 1.0 / QK_NOPE, 0.0) * jnp.ones((1, HEAD_PAD), F32)
    eye2 = jnp.eye(2, dtype=F32)
    w['bd_q'] = jnp.kron(eye2, m_q).astype(BF16)
    w['bd_k'] = jnp.kron(eye2, m_k).astype(BF16)

    pad_h = lambda a: jnp.pad(a, ((0, 0), (0, 0), (0, HEAD_PAD - a.shape[-1])))
    w['w_uk'] = pad_h(p['w_uk']).reshape(kv_lora, N_HEADS * HEAD_PAD).astype(BF16)
    w['w_uv'] = pad_h(p['w_uv']).reshape(kv_lora, N_HEADS * HEAD_PAD).astype(BF16)
    w['k_gain'] = jnp.tile(jnp.pad(p['k_nope_norm_g'], (0, HEAD_PAD - QK_NOPE)), N_HEADS)[None]

    dt = jnp.exp(p['s5_log_dt'])[:, None]
    ar, ai = p['s5_a_re'], p['s5_a_im']
    mag = jnp.exp(dt * ar)
    abar_re = mag * jnp.cos(dt * ai)
    abar_im = mag * jnp.sin(dt * ai)
    den = ar * ar + ai * ai
    num_re = abar_re - 1.0
    coef_re = (num_re * ar + abar_im * ai) / den
    coef_im = (abar_im * ar - num_re * ai) / den
    br, bi = p['s5_b_re'], p['s5_b_im']
    bbar_re = coef_re[..., None] * br - coef_im[..., None] * bi
    bbar_im = coef_re[..., None] * bi + coef_im[..., None] * br
    g = ar.shape[0]
    n_state = g * S5_STATE
    half = n_state // 2
    eye_g = jnp.eye(g, dtype=F32)
    b_re = jnp.einsum('gni,gh->gihn', bbar_re, eye_g).reshape(s5_width, n_state)
    b_im = jnp.einsum('gni,gh->gihn', bbar_im, eye_g).reshape(s5_width, n_state)
    c_re = jnp.einsum('gon,gh->gnho', p['s5_c_re'], eye_g).reshape(n_state, s5_width)
    c_im = jnp.einsum('gon,gh->gnho', p['s5_c_im'], eye_g).reshape(n_state, s5_width)
    uw = s5_width // 2
    w['b_bd'] = jnp.stack([
        jnp.concatenate([b_re[c * uw:(c + 1) * uw, c * half:(c + 1) * half],
                         b_im[c * uw:(c + 1) * uw, c * half:(c + 1) * half]], axis=1) for c in range(2)]).astype(BF16)
    w['c_bd'] = jnp.stack([
        jnp.concatenate([c_re[c * half:(c + 1) * half, c * uw:(c + 1) * uw],
                         -c_im[c * half:(c + 1) * half, c * uw:(c + 1) * uw]], axis=0) for c in range(2)]).astype(BF16)
    w['lam_re'] = jnp.broadcast_to(abar_re.reshape(1, n_state), (SUBLANES, n_state))
    w['lam_im'] = jnp.broadcast_to(abar_im.reshape(1, n_state), (SUBLANES, n_state))
    w['s5_d'] = p['s5_d'][None]
    w['w_glu'] = p['s5_w_glu'].astype(BF16)
    w['b_glu'] = p['s5_b_glu'][None]

    w['g_s5'] = p['out_norm_s5_g'][None]
    w['g_attn'] = p['out_norm_attn_g'][None]
    w['w_out'] = p['w_out'].astype(BF16)
    w['g_ffn'] = p['norm_ffn_g'][None]
    n_r = N_EXPERTS + N_EXPERT_GROUPS
    w['w_router'] = jnp.pad(jnp.concatenate([p['w_router_expert'], p['w_router_group']], axis=1),
                            ((0, 0), (0, LANES - n_r))).astype(BF16)
    w['b_router'] = jnp.pad(jnp.concatenate([p['b_router_expert'], p['b_router_group']]), (0, LANES - n_r))[None]
    w['w_gu'] = jnp.concatenate([p['w_e_gate'], p['w_e_up']], axis=-1).astype(BF16)
    w['w_d'] = p['w_e_down'].astype(BF16)
    return w


def _rope_table(pos):
    half = QK_ROPE // 2
    inv = ROPE_THETA ** (-jnp.arange(half, dtype=F32) / half)
    ang = pos.astype(F32)[:, None] * inv[None, :]
    cc = jnp.tile(jnp.cos(ang), (1, 2))
    ss = jnp.tile(jnp.sin(ang), (1, 2))
    n = pos.shape[0]
    scale = QK_DIM ** -0.5 * math.log2(math.e)
    tab_q = scale * jnp.concatenate([jnp.ones((n, QK_NOPE), F32), cc, ss], axis=-1)
    tab_k = jnp.concatenate([cc, ss, jnp.zeros((n, LANES - 2 * QK_ROPE), F32)], axis=-1)
    return jnp.concatenate([tab_q, tab_k], axis=-1)


def _s5_state_in(re, im):
    b = re.shape[0]
    re = re.reshape(b, -1)
    im = im.reshape(b, -1)
    half = re.shape[1] // 2
    return jnp.concatenate([re[:, :half], im[:, :half], re[:, half:], im[:, half:]], axis=1)


def _s5_state_out(h, g):
    b = h.shape[0]
    half = h.shape[1] // 4
    re = jnp.concatenate([h[:, :half], h[:, 2 * half:3 * half]], axis=1).reshape(b, g, S5_STATE)
    im = jnp.concatenate([h[:, half:2 * half], h[:, 3 * half:]], axis=1).reshape(b, g, S5_STATE)
    return re, im


def _trunk_layer(x, past_ckv, past_krope, h0_re, h0_im, w):
    bsz, seq, dm = x.shape
    assert bsz == SUBLANES, "the S5 scan keeps the batch on the sublanes"
    past_len = 0 if past_ckv is None else past_ckv.shape[1]
    g = w['lam_re'].shape[1] // S5_STATE
    tm = _tile(seq, 512)

    tab = _rope_table(past_len + jnp.arange(seq, dtype=jnp.int32))
    u, q, new_ckv, new_krope = _proj_call(x, tab, w, tm=tm)

    if h0_re is None:
        h0 = jnp.zeros((bsz, 2 * g * S5_STATE), F32)
    else:
        h0 = _s5_state_in(h0_re.astype(F32), h0_im.astype(F32))
    s5_out, h_last = _s5_call(u, h0, w, ts=min(seq, 64))
    h_re, h_im = _s5_state_out(h_last, g)

    if past_ckv is None:
        all_ckv, all_krope = new_ckv, new_krope
    else:
        all_ckv = jnp.concatenate([past_ckv.astype(F32), new_ckv], axis=1)
        all_krope = jnp.concatenate([past_krope.astype(F32), new_krope], axis=1)
    sk = all_ckv.shape[1]
    sk_pad = -(-sk // LANES) * LANES
    all_ckv = jnp.pad(all_ckv, ((0, 0), (0, sk_pad - sk), (0, 0)))
    krp = jnp.pad(all_krope, ((0, 0), (0, sk_pad - sk), (QK_NOPE, LANES - QK_DIM)))
    whole = sk_pad % ATTN_TK != 0
    k, vt = _kv_call(all_ckv, krp, w, tm=sk_pad if whole else KV_TILE)
    sq_pad = -(-seq // LANES) * LANES
    q = jnp.pad(q, ((0, 0), (0, 0), (0, sq_pad - seq), (0, 0)))
    tq = ATTN_TQ if sq_pad % ATTN_TQ == 0 else LANES
    attn = _attn_call(q, k, vt, tq=tq, tk=sk_pad if whole else ATTN_TK, q_pos0=past_len, sk_valid=sk)[:, :seq]

    h_tiles, route = _out_call(x, s5_out, attn, w, tm=tm)
    t = bsz * seq
    tb = _tile(t, MOE_BLOCK)
    offs, toks, gws = _route_tables(route.reshape(t, LANES), tb)
    y = _moe_call(offs, toks, gws, h_tiles, w, tb=tb, mc=MOE_CHUNK)
    return y.reshape(bsz, seq, dm), new_ckv, new_krope, h_re, h_im


def kernel(x_prompt, x_sample, cache_ckv, cache_krope, state_s5_re, state_s5_im, norm_mix_g, w_in, s5_a_re, s5_a_im, s5_log_dt, s5_b_re, s5_b_im, s5_c_re, s5_c_im, s5_d, s5_w_glu, s5_b_glu, q_lora_norm_g, w_uq, kv_lora_norm_g, w_uk, w_uv, q_nope_norm_g, q_rope_norm_g, k_nope_norm_g, k_rope_norm_g, out_norm_s5_g, out_norm_attn_g, w_out, norm_ffn_g, w_router_group, b_router_group, w_router_expert, b_router_expert, w_e_gate, w_e_up, w_e_down):
    params = dict(
        norm_mix_g=norm_mix_g, w_in=w_in, s5_a_re=s5_a_re, s5_a_im=s5_a_im, s5_log_dt=s5_log_dt,
        s5_b_re=s5_b_re, s5_b_im=s5_b_im, s5_c_re=s5_c_re, s5_c_im=s5_c_im, s5_d=s5_d,
        s5_w_glu=s5_w_glu, s5_b_glu=s5_b_glu, q_lora_norm_g=q_lora_norm_g, w_uq=w_uq,
        kv_lora_norm_g=kv_lora_norm_g, w_uk=w_uk, w_uv=w_uv, q_nope_norm_g=q_nope_norm_g,
        q_rope_norm_g=q_rope_norm_g, k_nope_norm_g=k_nope_norm_g, k_rope_norm_g=k_rope_norm_g,
        out_norm_s5_g=out_norm_s5_g, out_norm_attn_g=out_norm_attn_g, w_out=w_out, norm_ffn_g=norm_ffn_g,
        w_router_group=w_router_group, b_router_group=b_router_group,
        w_router_expert=w_router_expert, b_router_expert=b_router_expert,
        w_e_gate=w_e_gate, w_e_up=w_e_up, w_e_down=w_e_down)
    depth = w_in.shape[0]
    y_p, y_s = x_prompt, x_sample
    outs = [[] for _ in range(8)]
    for l in range(depth):
        w = _prepare_weights({k: a[l] for k, a in params.items()})
        y_p, c1, k1, r1, i1 = _trunk_layer(y_p, None, None, None, None, w)
        y_s, c2, k2, r2, i2 = _trunk_layer(y_s, cache_ckv[l], cache_krope[l], state_s5_re[l], state_s5_im[l], w)
        for lst, a in zip(outs, (c1, k1, r1, i1, c2, k2, r2, i2)):
            lst.append(a)
    return (y_p, y_s) + tuple(jnp.stack(lst) for lst in outs)
```

```python
import functools
import math

import jax
import jax.numpy as jnp
from jax import lax
from jax.experimental import pallas as pl
from jax.experimental.pallas import tpu as pltpu

F32 = jnp.float32
BF16 = jnp.bfloat16

CHUNK = 64
S5_GROUP = 16
S5_STATE = 64
N_HEADS = 8
QK_NOPE = 64
QK_ROPE = 32
QK_DIM = QK_NOPE + QK_ROPE
V_DIM = 64
ROPE_THETA = 10000.0
N_EXPERT_GROUPS = 4
EXPERTS_PER_GROUP = 8
N_EXPERTS = N_EXPERT_GROUPS * EXPERTS_PER_GROUP
EPS = 1e-6
NEG_INF = -1e30

LANES = 128
SUBLANES = 8
HEAD_PAD = LANES
ATTN_TQ = 512
ATTN_TK = 512
KV_TILE = 512
ROW_TILE = 1024
ROW_SUB = 512
MOE_BLOCK = 2048
MOE_CHUNK = 160
MOE_EXPERTS_PER_STEP = 2
REF_MARGIN = 1.02
REF_LIMIT = 40.0
ATTN_SPAN = 2
ATTN_COLS = 256
VT_ROWS = 80
VMEM_LIMIT = 48 * 1024 * 1024


def _tile(n, cap):
    for t in range(min(cap, n), 15, -1):
        if n % t == 0 and t % 16 == 0:
            return t
    return n


def _rms(x, g):
    return x * lax.rsqrt(jnp.mean(x * x, axis=-1, keepdims=True) + EPS) * g


def _dot(a, b):
    return jnp.dot(a, b, preferred_element_type=F32)


def _lane_iota(shape):
    return lax.broadcasted_iota(jnp.int32, shape, len(shape) - 1)


def _proj_kernel(x_ref, gmix_ref, win_ref, gq_ref, wuq_ref, bdq_ref, qgain_ref, gkv_ref, gkr_ref, tab_ref,
                 u_ref, q_ref, ckv_ref, kr_ref, *, s5_width, q_lora, kv_lora, sub):
    for r in range(x_ref.shape[1] // sub):
        rows = pl.ds(r * sub, sub)
        x = x_ref[0, rows, :]
        xn = _rms(x, gmix_ref[...])
        proj = _dot(xn.astype(BF16), win_ref[...])
        u_ref[0, rows, :] = proj[:, :s5_width].astype(BF16)

        o = s5_width
        cqn = _rms(proj[:, o:o + q_lora], gq_ref[...])
        o += q_lora
        ckv_ref[0, rows, :] = _rms(proj[:, o:o + kv_lora], gkv_ref[...])
        o += kv_lora

        tab = tab_ref[rows, :]
        lane = _lane_iota((sub, LANES))

        kr = proj[:, o:o + LANES]
        ms = jnp.sum(jnp.where(lane < QK_ROPE, kr * kr, 0.0), axis=-1, keepdims=True) * (1.0 / QK_ROPE)
        t = kr * lax.rsqrt(ms + EPS) * gkr_ref[...] * tab[:, LANES:]
        kro = t + pltpu.roll(t, LANES - QK_ROPE, axis=1)
        kr_ref[0, rows, :] = kro[:, :QK_ROPE]

        qa = _dot(cqn.astype(BF16), wuq_ref[...])
        qsq = (qa * qa).astype(BF16)
        pair = 2 * HEAD_PAD
        ms = jnp.concatenate([_dot(qsq[:, c * pair:(c + 1) * pair], bdq_ref[...])
                              for c in range(N_HEADS // 2)], axis=-1)
        qn = qa * lax.rsqrt(ms + EPS) * qgain_ref[...]
        is_rope = (lane >= QK_NOPE) & (lane < QK_DIM)
        for h in range(N_HEADS):
            th = qn[:, h * HEAD_PAD:(h + 1) * HEAD_PAD] * tab[:, :LANES]
            rolled = pltpu.roll(th, LANES - QK_ROPE, axis=1)
            oh = jnp.where(lane < QK_DIM, th + jnp.where(is_rope, rolled, 0.0), 0.0)
            q_ref[0, h, rows, :] = oh.astype(BF16)


def _proj_call(x, tab, w, *, tm):
    bsz, seq, dm = x.shape
    s5_width, q_lora, kv_lora = w['s5_width'], w['q_lora'], w['kv_lora']
    in_cols = w['w_in'].shape[1]
    hp = N_HEADS * HEAD_PAD
    full = lambda shape: pl.BlockSpec(shape, lambda b, i: (0,) * len(shape))
    kern = functools.partial(_proj_kernel, s5_width=s5_width, q_lora=q_lora, kv_lora=kv_lora, sub=min(tm, ROW_SUB))
    return pl.pallas_call(
        kern,
        grid=(bsz, seq // tm),
        in_specs=[
            pl.BlockSpec((1, tm, dm), lambda b, i: (b, i, 0)),
            full((1, dm)), full((dm, in_cols)), full((1, q_lora)), full((q_lora, hp)),
            full((2 * HEAD_PAD, 2 * HEAD_PAD)), full((1, hp)), full((1, kv_lora)), full((1, LANES)),
            pl.BlockSpec((tm, 2 * LANES), lambda b, i: (i, 0)),
        ],
        out_specs=[
            pl.BlockSpec((1, tm, s5_width), lambda b, i: (b, i, 0)),
            pl.BlockSpec((1, N_HEADS, tm, HEAD_PAD), lambda b, i: (b, 0, i, 0)),
            pl.BlockSpec((1, tm, kv_lora), lambda b, i: (b, i, 0)),
            pl.BlockSpec((1, tm, QK_ROPE), lambda b, i: (b, i, 0)),
        ],
        out_shape=[
            jax.ShapeDtypeStruct((bsz, seq, s5_width), BF16),
            jax.ShapeDtypeStruct((bsz, N_HEADS, seq, HEAD_PAD), BF16),
            jax.ShapeDtypeStruct((bsz, seq, kv_lora), F32),
            jax.ShapeDtypeStruct((bsz, seq, QK_ROPE), F32),
        ],
        compiler_params=pltpu.CompilerParams(
            dimension_semantics=("parallel", "parallel"), vmem_limit_bytes=VMEM_LIMIT),
    )(x, w['g_mix'], w['w_in'], w['g_q'], w['w_uq'], w['bd_q'], w['q_gain'], w['g_kv'], w['g_kr'], tab)


def _kv_kernel(ckv_ref, krp_ref, wuk_ref, wuv_ref, bdk_ref, kg_ref, k_ref, v_ref):
    c = ckv_ref[0].astype(BF16)
    ka = _dot(c, wuk_ref[...])
    va = _dot(c, wuv_ref[...])
    ksq = (ka * ka).astype(BF16)
    pair = 2 * HEAD_PAD
    ms = jnp.concatenate([_dot(ksq[:, c2 * pair:(c2 + 1) * pair], bdk_ref[...])
                          for c2 in range(N_HEADS // 2)], axis=-1)
    kn = ka * lax.rsqrt(ms + EPS) * kg_ref[...]
    krp = krp_ref[0]
    lane = _lane_iota(krp.shape)
    ones_col = jnp.where(lane == V_DIM, 1.0, 0.0)
    for h in range(N_HEADS):
        k_ref[0, h] = (kn[:, h * HEAD_PAD:(h + 1) * HEAD_PAD] + krp).astype(BF16)
        vt = (va[:, h * HEAD_PAD:(h + 1) * HEAD_PAD] + ones_col).T
        v_ref[0, h] = vt[:VT_ROWS].astype(BF16)


def _kv_call(ckv, krp, w, *, tm):
    bsz, sk, kv_lora = ckv.shape
    hp = N_HEADS * HEAD_PAD
    full = lambda shape: pl.BlockSpec(shape, lambda b, i: (0,) * len(shape))
    return pl.pallas_call(
        _kv_kernel,
        grid=(bsz, sk // tm),
        in_specs=[
            pl.BlockSpec((1, tm, kv_lora), lambda b, i: (b, i, 0)),
            pl.BlockSpec((1, tm, LANES), lambda b, i: (b, i, 0)),
            full((kv_lora, hp)), full((kv_lora, hp)), full((2 * HEAD_PAD, 2 * HEAD_PAD)), full((1, hp)),
        ],
        out_specs=[
            pl.BlockSpec((1, N_HEADS, tm, HEAD_PAD), lambda b, i: (b, 0, i, 0)),
            pl.BlockSpec((1, N_HEADS, VT_ROWS, tm), lambda b, i: (b, 0, 0, i)),
        ],
        out_shape=[
            jax.ShapeDtypeStruct((bsz, N_HEADS, sk, HEAD_PAD), BF16),
            jax.ShapeDtypeStruct((bsz, N_HEADS, VT_ROWS, sk), BF16),
        ],
        compiler_params=pltpu.CompilerParams(
            dimension_semantics=("parallel", "parallel"), vmem_limit_bytes=VMEM_LIMIT),
    )(ckv, krp, w['w_uk'], w['w_uv'], w['bd_k'], w['k_gain'])


def _attn_kernel(q_ref, k_ref, vt_ref, o_ref, *, tq, tk, q_pos0, sk_valid):
    sq = q_ref.shape[2]
    nq = sq // tq

    n_heads = q_ref.shape[1]

    cw = min(tq, ATTN_COLS)
    streams = [(hh, c) for hh in range(n_heads) for c in range(tq // cw)]
    sk = k_ref.shape[2]
    nt = (((1,), (1,)), ((), ()))
    ones_rows = jnp.ones((SUBLANES, HEAD_PAD), BF16)

    def sq_norms(x):
        xf = x.astype(F32)
        return lax.dot_general(ones_rows, (xf * xf).astype(BF16), nt, preferred_element_type=F32)[:1]

    def kmax_body(j, carry):
        k0 = pl.multiple_of(j * tk, tk)
        return tuple(jnp.maximum(carry[hh], sq_norms(k_ref[0, hh, pl.ds(k0, tk), :])) for hh in range(n_heads))

    kmax2 = lax.fori_loop(0, sk // tk, kmax_body, tuple(jnp.zeros((1, tk), F32) for _ in range(n_heads)))
    kmax2 = [jnp.max(v, axis=1, keepdims=True) for v in kmax2]

    def scores(qs, p0, j, width, masked):
        k0 = pl.multiple_of(j * width, width)
        kbs = [k_ref[0, hh, pl.ds(k0, width), :] for hh in range(n_heads)]
        ss = [lax.dot_general(kbs[hh], qs[i], nt, preferred_element_type=F32)
              for i, (hh, c) in enumerate(streams)]
        if masked:
            k_pos = k0 + lax.broadcasted_iota(jnp.int32, (width, 1), 0)
            for c in range(tq // cw):
                q_pos = p0 + c * cw + lax.broadcasted_iota(jnp.int32, (1, cw), 1)
                visible = k_pos < jnp.minimum((q_pos // CHUNK + 1) * CHUNK, sk_valid)
                for i, (hh, c2) in enumerate(streams):
                    if c2 == c:
                        ss[i] = jnp.where(visible, ss[i], NEG_INF)
        return ss

    def values(j, width):
        k0 = pl.multiple_of(j * width, width)
        return [vt_ref[0, hh, :, pl.ds(k0, width)] for hh in range(n_heads)]

    def step_running_max(qs, p0, j, carry, masked):
        ss, vts = scores(qs, p0, j, tk, masked), values(j, tk)
        new = []
        for i, (hh, c) in enumerate(streams):
            m, acc = carry[i]
            m_new = jnp.maximum(m, jnp.max(ss[i], axis=0, keepdims=True))
            acc = jnp.exp2(m - m_new) * acc + _dot(vts[hh], jnp.exp2(ss[i] - m_new).astype(BF16))
            new.append((m_new, acc))
        return tuple(new)

    def step_fixed_ref(qs, refs, p0, j, width, carry, masked):
        ss, vts = scores(qs, p0, j, width, masked), values(j, width)
        return tuple(carry[i] + _dot(vts[hh], jnp.exp2(ss[i] - refs[i]).astype(BF16))
                     for i, (hh, c) in enumerate(streams))

    def q_body(qi, carry):
        q0 = pl.multiple_of(qi * tq, tq)
        qs = [q_ref[0, hh, pl.ds(q0 + c * cw, cw), :] for hh, c in streams]
        p0 = q_pos0 + q0
        n_open = jnp.minimum((p0 // CHUNK + 1) * CHUNK, sk_valid) // tk
        v1 = jnp.minimum(((p0 + tq - 1) // CHUNK + 1) * CHUNK, sk_valid)
        n_all = (v1 + tk - 1) // tk
        refs = [jnp.sqrt(sq_norms(qs[i]) * kmax2[hh]) * REF_MARGIN for i, (hh, c) in enumerate(streams)]
        ref_max = functools.reduce(jnp.maximum, [jnp.max(r) for r in refs])

        def fixed_ref():
            span = ATTN_SPAN if ATTN_SPAN * tk <= sk else 1
            n_wide = n_open // span
            st = tuple(jnp.zeros((VT_ROWS, cw), F32) for _ in streams)
            if span > 1:
                st = lax.fori_loop(0, n_wide,
                                   lambda j, c: step_fixed_ref(qs, refs, p0, j, span * tk, c, False), st)
            st = lax.fori_loop(n_wide * span if span > 1 else 0, n_open,
                               lambda j, c: step_fixed_ref(qs, refs, p0, j, tk, c, False), st)
            return lax.fori_loop(n_open, n_all, lambda j, c: step_fixed_ref(qs, refs, p0, j, tk, c, True), st)

        def running_max():
            st = tuple((jnp.full((1, cw), NEG_INF, F32), jnp.zeros((VT_ROWS, cw), F32)) for _ in streams)
            st = lax.fori_loop(0, n_open, lambda j, c: step_running_max(qs, p0, j, c, False), st)
            st = lax.fori_loop(n_open, n_all, lambda j, c: step_running_max(qs, p0, j, c, True), st)
            return tuple(acc for _, acc in st)

        accs = lax.cond(ref_max < REF_LIMIT, fixed_ref, running_max)
        for c in range(tq // cw):
            ot = jnp.concatenate([accs[i][:V_DIM] / accs[i][V_DIM:V_DIM + 1]
                                  for i, (hh, c2) in enumerate(streams) if c2 == c], axis=0)
            o_ref[0, pl.ds(q0 + c * cw, cw), :] = ot.T.astype(o_ref.dtype)
        return carry

    lax.fori_loop(0, nq, q_body, 0)


def _attn_call(q, k, vt, *, tq, tk, q_pos0, sk_valid):
    bsz, nh, sq, _ = q.shape
    sk = k.shape[2]
    assert sq % tq == 0 and sk % tk == 0 and tq % LANES == 0 and tk % LANES == 0
    kern = functools.partial(_attn_kernel, tq=tq, tk=tk, q_pos0=q_pos0, sk_valid=sk_valid)
    return pl.pallas_call(
        kern,
        grid=(bsz, nh // 2),
        in_specs=[
            pl.BlockSpec((1, 2, sq, HEAD_PAD), lambda b, h: (b, h, 0, 0)),
            pl.BlockSpec((1, 2, sk, HEAD_PAD), lambda b, h: (b, h, 0, 0)),
            pl.BlockSpec((1, 2, VT_ROWS, sk), lambda b, h: (b, h, 0, 0)),
        ],
        out_specs=pl.BlockSpec((1, sq, 2 * V_DIM), lambda b, h: (b, 0, h)),
        out_shape=jax.ShapeDtypeStruct((bsz, sq, nh * V_DIM), BF16),
        compiler_params=pltpu.CompilerParams(
            dimension_semantics=("parallel", "parallel"), vmem_limit_bytes=VMEM_LIMIT),
    )(q, k, vt)


def _s5_kernel(u_ref, h0_ref, lre_ref, lim_ref, bbd_ref, cbd_ref, d_ref, wglu_ref, bglu_ref,
               y_ref, hout_ref, hbuf, hstate, u_tb, y_tb, *, ts, n_half, col_chunk):
    i = pl.program_id(0)
    half = n_half
    width = hbuf.shape[1]
    bsz, _, s5_width = u_ref.shape
    uw = s5_width // 2
    n_slab = s5_width // LANES

    @pl.when(i == 0)
    def _():
        hstate[...] = h0_ref[...]

    for b in range(bsz):
        ub = u_ref[b].astype(F32)
        for c in range(n_slab):
            u_tb[c, pl.ds(b, ts, stride=SUBLANES), :] = ub[:, c * LANES:(c + 1) * LANES]
    u32 = jnp.concatenate([u_tb[c] for c in range(n_slab)], axis=-1)
    u = u32.astype(BF16)
    for c in range(2):
        hbuf[:, c * 2 * half:(c + 1) * 2 * half] = _dot(u[:, c * uw:(c + 1) * uw], bbd_ref[c])

    for c in range(2):
        for cc in range(half // col_chunk):
            re0 = c * 2 * half + cc * col_chunk
            im0 = re0 + half
            l0 = c * half + cc * col_chunk
            ar = lre_ref[:, l0:l0 + col_chunk]
            ai = lim_ref[:, l0:l0 + col_chunk]

            def step(s, carry, re0=re0, im0=im0, ar=ar, ai=ai):
                hr, hi = carry
                r0 = pl.multiple_of(s * SUBLANES, SUBLANES)
                nr = ar * hr - ai * hi + hbuf[pl.ds(r0, SUBLANES), re0:re0 + col_chunk]
                ni = ar * hi + ai * hr + hbuf[pl.ds(r0, SUBLANES), im0:im0 + col_chunk]
                hbuf[pl.ds(r0, SUBLANES), re0:re0 + col_chunk] = nr
                hbuf[pl.ds(r0, SUBLANES), im0:im0 + col_chunk] = ni
                return nr, ni

            hr, hi = lax.fori_loop(
                0, ts, step,
                (hstate[:, re0:re0 + col_chunk], hstate[:, im0:im0 + col_chunk]), unroll=8)
            hstate[:, re0:re0 + col_chunk] = hr
            hstate[:, im0:im0 + col_chunk] = hi

    hout_ref[...] = hstate[...]

    y = jnp.concatenate(
        [_dot(hbuf[:, c * 2 * half:(c + 1) * 2 * half].astype(BF16), cbd_ref[c]) for c in range(2)], axis=-1)
    y = y + d_ref[...] * u32
    z = jax.nn.gelu(y)
    gate = _dot(z.astype(BF16), wglu_ref[...]) + bglu_ref[...]
    out = z * jax.nn.sigmoid(gate)
    for c in range(n_slab):
        y_tb[c] = out[:, c * LANES:(c + 1) * LANES]
    for b in range(bsz):
        y_ref[b] = jnp.concatenate([y_tb[c, pl.ds(b, ts, stride=SUBLANES), :] for c in range(n_slab)],
                                   axis=-1).astype(y_ref.dtype)


def _s5_call(u, h0, w, *, ts):
    bsz, seq, s5_width = u.shape
    assert bsz == SUBLANES, "the S5 scan keeps the batch on the sublanes"
    n_state = w['lam_re'].shape[1]
    half = n_state // 2
    width = 2 * n_state
    n_steps = seq // ts
    full = lambda shape: pl.BlockSpec(shape, lambda i: (0,) * len(shape))
    kern = functools.partial(_s5_kernel, ts=ts, n_half=half, col_chunk=512)
    return pl.pallas_call(
        kern,
        grid=(n_steps,),
        in_specs=[
            pl.BlockSpec((bsz, ts, s5_width), lambda i: (0, i, 0)),
            full((SUBLANES, width)), full((SUBLANES, n_state)), full((SUBLANES, n_state)),
            full((2, s5_width // 2, 2 * half)), full((2, 2 * half, s5_width // 2)),
            full((1, s5_width)), full((s5_width, s5_width)), full((1, s5_width)),
        ],
        out_specs=[
            pl.BlockSpec((bsz, ts, s5_width), lambda i: (0, i, 0)),
            full((SUBLANES, width)),
        ],
        out_shape=[
            jax.ShapeDtypeStruct((bsz, seq, s5_width), BF16),
            jax.ShapeDtypeStruct((SUBLANES, width), F32),
        ],
        scratch_shapes=[
            pltpu.VMEM((ts * SUBLANES, width), F32),
            pltpu.VMEM((SUBLANES, width), F32),
            pltpu.VMEM((s5_width // LANES, ts * SUBLANES, LANES), F32),
            pltpu.VMEM((s5_width // LANES, ts * SUBLANES, LANES), F32),
        ],
        compiler_params=pltpu.CompilerParams(
            dimension_semantics=("arbitrary",), vmem_limit_bytes=VMEM_LIMIT),
    )(u, h0, w['lam_re'], w['lam_im'], w['b_bd'], w['c_bd'], w['s5_d'], w['w_glu'], w['b_glu'])


def _out_kernel(x_ref, s5_ref, at_ref, gs5_ref, gat_ref, wout_ref, gffn_ref, wr_ref, br_ref,
                h_ref, route_ref, *, sub):
    for r in range(x_ref.shape[1] // sub):
        rows = pl.ds(r * sub, sub)
        s5n = _rms(s5_ref[0, rows, :].astype(F32), gs5_ref[...])
        atn = _rms(at_ref[0, rows, :].astype(F32), gat_ref[...])
        merged = jnp.concatenate([s5n, atn], axis=-1).astype(BF16)
        h = x_ref[0, rows, :] + _dot(merged, wout_ref[...])
        for j in range(h.shape[1] // LANES):
            h_ref[pl.ds(r * sub * SUBLANES + j, sub, stride=SUBLANES), :] = h[:, j * LANES:(j + 1) * LANES]
        hn = _rms(h, gffn_ref[...]).astype(BF16)

        logits = _dot(hn, wr_ref[...]) + br_ref[...]
        lane_i = _lane_iota(logits.shape)
        lane = lane_i.astype(F32)
        lane_group = (lane_i // EXPERTS_PER_GROUP).astype(F32)
        big = float(LANES)
        is_g = (lane_i >= N_EXPERTS) & (lane_i < N_EXPERTS + N_EXPERT_GROUPS)
        gl = jnp.where(is_g, logits, NEG_INF)
        gmax = jnp.max(gl, axis=-1, keepdims=True)
        g_idx = jnp.min(jnp.where(gl == gmax, lane, big), axis=-1, keepdims=True) - N_EXPERTS
        g_top = 1.0 / jnp.sum(jnp.where(is_g, jnp.exp(gl - gmax), 0.0), axis=-1, keepdims=True)
        in_group = (lane_i < N_EXPERTS) & (lane_group == g_idx)
        el = jnp.where(in_group, logits, NEG_INF)
        v1 = jnp.max(el, axis=-1, keepdims=True)
        i1 = jnp.min(jnp.where(el == v1, lane, big), axis=-1, keepdims=True)
        el2 = jnp.where(lane == i1, NEG_INF, el)
        v2 = jnp.max(el2, axis=-1, keepdims=True)
        i2 = jnp.min(jnp.where(el2 == v2, lane, big), axis=-1, keepdims=True)
        e21 = jnp.exp(v2 - v1)
        w1 = g_top / (1.0 + e21)
        w2 = g_top * e21 / (1.0 + e21)
        route_ref[0, rows, :] = (jnp.where(lane_i == 0, i1, 0.0) + jnp.where(lane_i == 1, i2, 0.0)
                                 + jnp.where(lane_i == 2, w1, 0.0) + jnp.where(lane_i == 3, w2, 0.0))


def _out_call(x, s5_out, attn, w, *, tm):
    bsz, seq, dm = x.shape
    s5_width = s5_out.shape[2]
    aw = attn.shape[2]
    full = lambda shape: pl.BlockSpec(shape, lambda b, i: (0,) * len(shape))
    tok = lambda width: pl.BlockSpec((1, tm, width), lambda b, i: (b, i, 0))
    return pl.pallas_call(
        functools.partial(_out_kernel, sub=min(tm, ROW_SUB)),
        grid=(bsz, seq // tm),
        in_specs=[
            tok(dm),
            tok(s5_width),
            tok(aw),
            full((1, s5_width)), full((1, aw)), full((s5_width + aw, dm)), full((1, dm)),
            full((dm, LANES)), full((1, LANES)),
        ],
        out_specs=[
            pl.BlockSpec((tm * SUBLANES, LANES), lambda b, i: (b * (seq // tm) + i, 0)),
            tok(LANES),
        ],
        out_shape=[
            jax.ShapeDtypeStruct((bsz * seq * SUBLANES, LANES), F32),
            jax.ShapeDtypeStruct((bsz, seq, LANES), F32),
        ],
        compiler_params=pltpu.CompilerParams(
            dimension_semantics=("parallel", "parallel"), vmem_limit_bytes=VMEM_LIMIT),
    )(x, s5_out, attn, w['g_s5'], w['g_attn'], w['w_out'], w['g_ffn'], w['w_router'], w['b_router'])


def _moe_kernel(offs_ref, toks_ref, gws_ref, h_hbm, gffn_ref, wgu_ref, wd_ref, o_ref, acc, hn_ref, gbuf, ybuf, sem,
                *, d_expert, tb, mc, norm_rows):
    b = pl.program_id(0)
    e = pl.program_id(1)
    tok_rows = SUBLANES
    dm = tok_rows * LANES

    @pl.when(e == 0)
    def _():
        cp = pltpu.make_async_copy(h_hbm.at[pl.ds(b * tb * tok_rows, tb * tok_rows)], acc, sem)
        cp.start()
        gbuf[...] = jnp.zeros_like(gbuf)
        cp.wait()

        def norm(i, carry):
            r0 = pl.multiple_of(i * norm_rows * tok_rows, norm_rows * tok_rows)
            x3 = acc[pl.ds(r0, norm_rows * tok_rows), :].reshape(norm_rows, tok_rows, LANES)
            ms = jnp.sum(jnp.sum(x3 * x3, axis=2, keepdims=True), axis=1, keepdims=True) * (1.0 / dm)
            hn3 = x3 * lax.rsqrt(ms + EPS) * gffn_ref[...][None]
            hn_ref[pl.ds(r0, norm_rows * tok_rows), :] = hn3.reshape(norm_rows * tok_rows, LANES)
            return carry

        lax.fori_loop(0, tb // norm_rows, norm, 0)

    n_exp = wgu_ref.shape[0]
    offs = [offs_ref[0, 0, e * n_exp + k] for k in range(n_exp + 1)]
    cnts = [offs[k + 1] - offs[k] for k in range(n_exp)]

    def chunk(c, carry):
        bases = [offs[k] + c * mc for k in range(n_exp)]
        ns = [jnp.clip(cnts[k] - c * mc, 0, mc) for k in range(n_exp)]

        for k in range(n_exp):
            def gather8(g, carry, k=k):
                for u in range(SUBLANES):
                    i = g * SUBLANES + u
                    row = pl.multiple_of(toks_ref[0, 0, bases[k] + i], tok_rows)
                    gbuf[k, pl.ds(pl.multiple_of(i * tok_rows, tok_rows), tok_rows), :] = (
                        hn_ref[pl.ds(row, tok_rows), :])
                return carry

            lax.fori_loop(0, (ns[k] + SUBLANES - 1) // SUBLANES, gather8, 0)

        for k in range(n_exp):
            x = jnp.concatenate([gbuf[k, pl.ds(j, mc, stride=tok_rows), :].astype(BF16) for j in range(tok_rows)],
                                axis=-1)
            gu = _dot(x, wgu_ref[k])
            hdn = jax.nn.silu(gu[:, :d_expert]) * gu[:, d_expert:]
            y = _dot(hdn.astype(BF16), wd_ref[k])
            for j in range(tok_rows):
                ybuf[k, pl.ds(j, mc, stride=tok_rows), :] = y[:, j * LANES:(j + 1) * LANES]

        for k in range(n_exp):
            def add_rows(first, count, k=k):
                idx = [first + u for u in range(count)]
                dst = [pl.multiple_of(toks_ref[0, 0, bases[k] + i], tok_rows) for i in idx]
                val = [acc[pl.ds(d, tok_rows), :] + gws_ref[0, 0, bases[k] + i]
                       * ybuf[k, pl.ds(pl.multiple_of(i * tok_rows, tok_rows), tok_rows), :]
                       for i, d in zip(idx, dst)]
                for d, v in zip(dst, val):
                    acc[pl.ds(d, tok_rows), :] = v

            n_full = ns[k] // SUBLANES

            def scatter8(g, carry, add_rows=add_rows):
                add_rows(g * SUBLANES, SUBLANES)
                return carry

            def scatter1(i, carry, add_rows=add_rows):
                add_rows(i, 1)
                return carry

            lax.fori_loop(0, n_full, scatter8, 0)
            lax.fori_loop(n_full * SUBLANES, ns[k], scatter1, 0)
        return carry

    n_chunks = functools.reduce(jnp.maximum, [(cnt + mc - 1) // mc for cnt in cnts])
    lax.fori_loop(0, n_chunks, chunk, 0)

    @pl.when(e == pl.num_programs(1) - 1)
    def _():
        for j in range(tok_rows):
            o_ref[:, j * LANES:(j + 1) * LANES] = acc[pl.ds(j, tb, stride=tok_rows), :]


def _moe_call(offs, toks, gws, h_tiles, w, *, tb, mc):
    tok_rows = SUBLANES
    dm = tok_rows * LANES
    t = h_tiles.shape[0] // tok_rows
    assert w['w_gu'].shape[1] == dm, "a token row must be exactly one (8,128) f32 tile"
    nb = t // tb
    d_expert = w['w_d'].shape[1]
    kern = functools.partial(_moe_kernel, d_expert=d_expert, tb=tb, mc=mc, norm_rows=min(tb, 256))
    smem = lambda n: pl.BlockSpec((1, 1, n), lambda b, e: (b, 0, 0), memory_space=pltpu.SMEM)
    n_exp = MOE_EXPERTS_PER_STEP
    return pl.pallas_call(
        kern,
        grid=(nb, N_EXPERTS // n_exp),
        in_specs=[
            smem(offs.shape[2]), smem(toks.shape[2]), smem(gws.shape[2]),
            pl.BlockSpec(memory_space=pl.ANY),
            pl.BlockSpec((tok_rows, LANES), lambda b, e: (0, 0)),
            pl.BlockSpec((n_exp, dm, 2 * d_expert), lambda b, e: (e, 0, 0)),
            pl.BlockSpec((n_exp, d_expert, dm), lambda b, e: (e, 0, 0)),
        ],
        out_specs=pl.BlockSpec((tb, dm), lambda b, e: (b, 0)),
        out_shape=jax.ShapeDtypeStruct((t, dm), F32),
        scratch_shapes=[
            pltpu.VMEM((tb * tok_rows, LANES), F32),
            pltpu.VMEM((tb * tok_rows, LANES), F32),
            pltpu.VMEM((n_exp, mc * tok_rows, LANES), F32),
            pltpu.VMEM((n_exp, mc * tok_rows, LANES), F32),
            pltpu.SemaphoreType.DMA(()),
        ],
        compiler_params=pltpu.CompilerParams(
            dimension_semantics=("parallel", "arbitrary"), vmem_limit_bytes=VMEM_LIMIT),
    )(offs, toks, gws, h_tiles, w['g_ffn'].reshape(tok_rows, LANES), w['w_gu'], w['w_d'])


def _route_tables(route, tb):
    t = route.shape[0]
    nb = t // tb
    ids = route[:, :2].astype(jnp.int32).reshape(nb, 2 * tb)
    wts = route[:, 2:4].reshape(nb, 2 * tb)
    n_slot = 2 * tb
    keys = jnp.sort(ids * n_slot + jnp.arange(n_slot, dtype=jnp.int32), axis=1)
    order = keys % n_slot
    toks = order // 2 * SUBLANES
    toks = jnp.pad(toks, ((0, 0), (0, SUBLANES)))
    gws = jnp.take_along_axis(wts, order, axis=1)
    bounds = jnp.arange(N_EXPERTS + 1, dtype=jnp.int32) * n_slot
    offs = jax.vmap(lambda k: jnp.searchsorted(k, bounds, side='left'))(keys).astype(jnp.int32)
    offs = jnp.pad(offs, ((0, 0), (0, LANES - offs.shape[1])))
    return offs[:, None, :], toks[:, None, :], gws[:, None, :]


def _prepare_weights(p):
    dm, in_cols = p['w_in'].shape
    s5_width = p['s5_d'].shape[0]
    q_lora = p['q_lora_norm_g'].shape[0]
    kv_lora = p['kv_lora_norm_g'].shape[0]
    half_r = QK_ROPE // 2
    w = {'s5_width': s5_width, 'q_lora': q_lora, 'kv_lora': kv_lora}

    def partner(a):
        return jnp.concatenate([-a[..., half_r:], a[..., :half_r]], axis=-1)

    def swap(a):
        return jnp.concatenate([a[..., half_r:], a[..., :half_r]], axis=-1)

    w_kr = p['w_in'][:, in_cols - QK_ROPE:]
    w['w_in'] = jnp.concatenate(
        [p['w_in'], partner(w_kr), jnp.zeros((dm, LANES - 2 * QK_ROPE), F32)], axis=-1).astype(BF16)
    w['g_mix'] = p['norm_mix_g'][None]
    w['g_q'] = p['q_lora_norm_g'][None]
    w['g_kv'] = p['kv_lora_norm_g'][None]
    gkr = p['k_rope_norm_g']
    w['g_kr'] = jnp.concatenate([gkr, swap(gkr), jnp.zeros((LANES - 2 * QK_ROPE,), F32)])[None]

    wq = p['w_uq']
    wq_r = wq[..., QK_NOPE:]
    w['w_uq'] = jnp.concatenate([wq, partner(wq_r)], axis=-1).reshape(q_lora, N_HEADS * HEAD_PAD).astype(BF16)
    gr = p['q_rope_norm_g']
    w['q_gain'] = jnp.tile(jnp.concatenate([p['q_nope_norm_g'], gr, swap(gr)]), N_HEADS)[None]
    idx = jnp.arange(HEAD_PAD)
    row_nope = (idx < QK_NOPE)[:, None]
    row_rope = ((idx >= QK_NOPE) & (idx < QK_DIM))[:, None]
    col_nope = (idx < QK_NOPE)[None, :]
    m_q = jnp.where(row_nope & col_nope, 1.0 / QK_NOPE, 0.0) + jnp.where(row_rope & ~col_nope, 1.0 / QK_ROPE, 0.0)
    m_k = jnp.where(row_nope, 1.0 / QK_NOPE, 0.0) * jnp.ones((1, HEAD_PAD), F32)
    eye2 = jnp.eye(2, dtype=F32)
    w['bd_q'] = jnp.kron(eye2, m_q).astype(BF16)
    w['bd_k'] = jnp.kron(eye2, m_k).astype(BF16)

    pad_h = lambda a: jnp.pad(a, ((0, 0), (0, 0), (0, HEAD_PAD - a.shape[-1])))
    w['w_uk'] = pad_h(p['w_uk']).reshape(kv_lora, N_HEADS * HEAD_PAD).astype(BF16)
    w['w_uv'] = pad_h(p['w_uv']).reshape(kv_lora, N_HEADS * HEAD_PAD).astype(BF16)
    w['k_gain'] = jnp.tile(jnp.pad(p['k_nope_norm_g'], (0, HEAD_PAD - QK_NOPE)), N_HEADS)[None]

    dt = jnp.exp(p['s5_log_dt'])[:, None]
    ar, ai = p['s5_a_re'], p['s5_a_im']
    mag = jnp.exp(dt * ar)
    abar_re = mag * jnp.cos(dt * ai)
    abar_im = mag * jnp.sin(dt * ai)
    den = ar * ar + ai * ai
    num_re = abar_re - 1.0
    coef_re = (num_re * ar + abar_im * ai) / den
    coef_im = (abar_im * ar - num_re * ai) / den
    br, bi = p['s5_b_re'], p['s5_b_im']
    bbar_re = coef_re[..., None] * br - coef_im[..., None] * bi
    bbar_im = coef_re[..., None] * bi + coef_im[..., None] * br
    g = ar.shape[0]
    n_state = g * S5_STATE
    half = n_state // 2
    eye_g = jnp.eye(g, dtype=F32)
    b_re = jnp.einsum('gni,gh->gihn', bbar_re, eye_g).reshape(s5_width, n_state)
    b_im = jnp.einsum('gni,gh->gihn', bbar_im, eye_g).reshape(s5_width, n_state)
    c_re = jnp.einsum('gon,gh->gnho', p['s5_c_re'], eye_g).reshape(n_state, s5_width)
    c_im = jnp.einsum('gon,gh->gnho', p['s5_c_im'], eye_g).reshape(n_state, s5_width)
    uw = s5_width // 2
    w['b_bd'] = jnp.stack([
        jnp.concatenate([b_re[c * uw:(c + 1) * uw, c * half:(c + 1) * half],
                         b_im[c * uw:(c + 1) * uw, c * half:(c + 1) * half]], axis=1) for c in range(2)]).astype(BF16)
    w['c_bd'] = jnp.stack([
        jnp.concatenate([c_re[c * half:(c + 1) * half, c * uw:(c + 1) * uw],
                         -c_im[c * half:(c + 1) * half, c * uw:(c + 1) * uw]], axis=0) for c in range(2)]).astype(BF16)
    w['lam_re'] = jnp.broadcast_to(abar_re.reshape(1, n_state), (SUBLANES, n_state))
    w['lam_im'] = jnp.broadcast_to(abar_im.reshape(1, n_state), (SUBLANES, n_state))
    w['s5_d'] = p['s5_d'][None]
    w['w_glu'] = p['s5_w_glu'].astype(BF16)
    w['b_glu'] = p['s5_b_glu'][None]

    w['g_s5'] = p['out_norm_s5_g'][None]
    w['g_attn'] = p['out_norm_attn_g'][None]
    w['w_out'] = p['w_out'].astype(BF16)
    w['g_ffn'] = p['norm_ffn_g'][None]
    n_r = N_EXPERTS + N_EXPERT_GROUPS
    w['w_router'] = jnp.pad(jnp.concatenate([p['w_router_expert'], p['w_router_group']], axis=1),
                            ((0, 0), (0, LANES - n_r))).astype(BF16)
    w['b_router'] = jnp.pad(jnp.concatenate([p['b_router_expert'], p['b_router_group']]), (0, LANES - n_r))[None]
    w['w_gu'] = jnp.concatenate([p['w_e_gate'], p['w_e_up']], axis=-1).astype(BF16)
    w['w_d'] = p['w_e_down'].astype(BF16)
    return w


def _rope_table(pos):
    half = QK_ROPE // 2
    inv = ROPE_THETA ** (-jnp.arange(half, dtype=F32) / half)
    ang = pos.astype(F32)[:, None] * inv[None, :]
    cc = jnp.tile(jnp.cos(ang), (1, 2))
    ss = jnp.tile(jnp.sin(ang), (1, 2))
    n = pos.shape[0]
    scale = QK_DIM ** -0.5 * math.log2(math.e)
    tab_q = scale * jnp.concatenate([jnp.ones((n, QK_NOPE), F32), cc, ss], axis=-1)
    tab_k = jnp.concatenate([cc, ss, jnp.zeros((n, LANES - 2 * QK_ROPE), F32)], axis=-1)
    return jnp.concatenate([tab_q, tab_k], axis=-1)


def _s5_state_in(re, im):
    b = re.shape[0]
    re = re.reshape(b, -1)
    im = im.reshape(b, -1)
    half = re.shape[1] // 2
    return jnp.concatenate([re[:, :half], im[:, :half], re[:, half:], im[:, half:]], axis=1)


def _s5_state_out(h, g):
    b = h.shape[0]
    half = h.shape[1] // 4
    re = jnp.concatenate([h[:, :half], h[:, 2 * half:3 * half]], axis=1).reshape(b, g, S5_STATE)
    im = jnp.concatenate([h[:, half:2 * half], h[:, 3 * half:]], axis=1).reshape(b, g, S5_STATE)
    return re, im


def _trunk_layer(x, past_ckv, past_krope, h0_re, h0_im, w):
    bsz, seq, dm = x.shape
    assert bsz == SUBLANES, "the S5 scan keeps the batch on the sublanes"
    past_len = 0 if past_ckv is None else past_ckv.shape[1]
    g = w['lam_re'].shape[1] // S5_STATE
    tm = _tile(seq, ROW_TILE)

    tab = _rope_table(past_len + jnp.arange(seq, dtype=jnp.int32))
    u, q, new_ckv, new_krope = _proj_call(x, tab, w, tm=tm)

    if h0_re is None:
        h0 = jnp.zeros((bsz, 2 * g * S5_STATE), F32)
    else:
        h0 = _s5_state_in(h0_re.astype(F32), h0_im.astype(F32))
    s5_out, h_last = _s5_call(u, h0, w, ts=min(seq, 64))
    h_re, h_im = _s5_state_out(h_last, g)

    if past_ckv is None:
        all_ckv, all_krope = new_ckv, new_krope
    else:
        all_ckv = jnp.concatenate([past_ckv.astype(F32), new_ckv], axis=1)
        all_krope = jnp.concatenate([past_krope.astype(F32), new_krope], axis=1)
    sk = all_ckv.shape[1]
    sk_pad = -(-sk // LANES) * LANES
    all_ckv = jnp.pad(all_ckv, ((0, 0), (0, sk_pad - sk), (0, 0)))
    krp = jnp.pad(all_krope, ((0, 0), (0, sk_pad - sk), (QK_NOPE, LANES - QK_DIM)))
    whole = sk_pad % ATTN_TK != 0
    k, vt = _kv_call(all_ckv, krp, w, tm=sk_pad if whole else KV_TILE)
    sq_pad = -(-seq // LANES) * LANES
    q = jnp.pad(q, ((0, 0), (0, 0), (0, sq_pad - seq), (0, 0)))
    tq = ATTN_TQ if sq_pad % ATTN_TQ == 0 else LANES
    attn = _attn_call(q, k, vt, tq=tq, tk=sk_pad if whole else ATTN_TK, q_pos0=past_len, sk_valid=sk)[:, :seq]

    h_tiles, route = _out_call(x, s5_out, attn, w, tm=tm)
    t = bsz * seq
    tb = _tile(t, MOE_BLOCK)
    offs, toks, gws = _route_tables(route.reshape(t, LANES), tb)
    y = _moe_call(offs, toks, gws, h_tiles, w, tb=tb, mc=MOE_CHUNK)
    return y.reshape(bsz, seq, dm), new_ckv, new_krope, h_re, h_im


def kernel(x_prompt, x_sample, cache_ckv, cache_krope, state_s5_re, state_s5_im, norm_mix_g, w_in, s5_a_re, s5_a_im, s5_log_dt, s5_b_re, s5_b_im, s5_c_re, s5_c_im, s5_d, s5_w_glu, s5_b_glu, q_lora_norm_g, w_uq, kv_lora_norm_g, w_uk, w_uv, q_nope_norm_g, q_rope_norm_g, k_nope_norm_g, k_rope_norm_g, out_norm_s5_g, out_norm_attn_g, w_out, norm_ffn_g, w_router_group, b_router_group, w_router_expert, b_router_expert, w_e_gate, w_e_up, w_e_down):
    params = dict(
        norm_mix_g=norm_mix_g, w_in=w_in, s5_a_re=s5_a_re, s5_a_im=s5_a_im, s5_log_dt=s5_log_dt,
        s5_b_re=s5_b_re, s5_b_im=s5_b_im, s5_c_re=s5_c_re, s5_c_im=s5_c_im, s5_d=s5_d,
        s5_w_glu=s5_w_glu, s5_b_glu=s5_b_glu, q_lora_norm_g=q_lora_norm_g, w_uq=w_uq,
        kv_lora_norm_g=kv_lora_norm_g, w_uk=w_uk, w_uv=w_uv, q_nope_norm_g=q_nope_norm_g,
        q_rope_norm_g=q_rope_norm_g, k_nope_norm_g=k_nope_norm_g, k_rope_norm_g=k_rope_norm_g,
        out_norm_s5_g=out_norm_s5_g, out_norm_attn_g=out_norm_attn_g, w_out=w_out, norm_ffn_g=norm_ffn_g,
        w_router_group=w_router_group, b_router_group=b_router_group,
        w_router_expert=w_router_expert, b_router_expert=b_router_expert,
        w_e_gate=w_e_gate, w_e_up=w_e_up, w_e_down=w_e_down)
    depth = w_in.shape[0]
    y_p, y_s = x_prompt, x_sample
    outs = [[] for _ in range(8)]
    for l in range(depth):
        w = _prepare_weights({k: a[l] for k, a in params.items()})
        y_p, c1, k1, r1, i1 = _trunk_layer(y_p, None, None, None, None, w)
        y_s, c2, k2, r2, i2 = _trunk_layer(y_s, cache_ckv[l], cache_krope[l], state_s5_re[l], state_s5_im[l], w)
        for lst, a in zip(outs, (c1, k1, r1, i1, c2, k2, r2, i2)):
            lst.append(a)
    return (y_p, y_s) + tuple(jnp.stack(lst) for lst in outs)
```

```python
import functools
import math

import jax
import jax.numpy as jnp
from jax import lax
from jax.experimental import pallas as pl
from jax.experimental.pallas import tpu as pltpu

F32 = jnp.float32
BF16 = jnp.bfloat16

CHUNK = 64
S5_GROUP = 16
S5_STATE = 64
N_HEADS = 8
QK_NOPE = 64
QK_ROPE = 32
QK_DIM = QK_NOPE + QK_ROPE
V_DIM = 64
ROPE_THETA = 10000.0
N_EXPERT_GROUPS = 4
EXPERTS_PER_GROUP = 8
N_EXPERTS = N_EXPERT_GROUPS * EXPERTS_PER_GROUP
EPS = 1e-6
NEG_INF = -1e30

LANES = 128
SUBLANES = 8
HEAD_PAD = LANES
ATTN_TQ = 512
ATTN_TK = 512
KV_TILE = 1024
ROW_TILE = 1024
ROW_SUB = 512
MOE_BLOCK = 2048
MOE_CHUNK = 160
MOE_EXPERTS_PER_STEP = 4
REF_MARGIN = 1.02
REF_LIMIT = 40.0
ATTN_SPANS = (4, 2, 1)
ATTN_COLS = 256
VT_ROWS = 80
VMEM_LIMIT = 52 * 1024 * 1024


def _tile(n, cap):
    for t in range(min(cap, n), 15, -1):
        if n % t == 0 and t % 16 == 0:
            return t
    return n


def _rms(x, g):
    return x * lax.rsqrt(jnp.mean(x * x, axis=-1, keepdims=True) + EPS) * g


def _dot(a, b):
    return jnp.dot(a, b, preferred_element_type=F32)


def _lane_iota(shape):
    return lax.broadcasted_iota(jnp.int32, shape, len(shape) - 1)


def _proj_kernel(x_ref, gmix_ref, win_ref, gq_ref, wuq_ref, bdq_ref, qgain_ref, gkv_ref, gkr_ref, tab_ref,
                 u_ref, q_ref, ckv_ref, kr_ref, *, s5_width, q_lora, kv_lora, sub):
    for r in range(x_ref.shape[1] // sub):
        rows = pl.ds(r * sub, sub)
        x = x_ref[0, rows, :]
        xn = _rms(x, gmix_ref[...])
        proj = _dot(xn.astype(BF16), win_ref[...])
        u_ref[0, rows, :] = proj[:, :s5_width].astype(BF16)

        o = s5_width
        cqn = _rms(proj[:, o:o + q_lora], gq_ref[...])
        o += q_lora
        ckv_ref[0, rows, :] = _rms(proj[:, o:o + kv_lora], gkv_ref[...])
        o += kv_lora

        tab = tab_ref[rows, :]
        lane = _lane_iota((sub, LANES))

        kr = proj[:, o:o + LANES]
        ms = jnp.sum(jnp.where(lane < QK_ROPE, kr * kr, 0.0), axis=-1, keepdims=True) * (1.0 / QK_ROPE)
        t = kr * lax.rsqrt(ms + EPS) * gkr_ref[...] * tab[:, LANES:]
        kro = t + pltpu.roll(t, LANES - QK_ROPE, axis=1)
        kr_ref[0, rows, :] = kro[:, :QK_ROPE]

        qa = _dot(cqn.astype(BF16), wuq_ref[...])
        qsq = (qa * qa).astype(BF16)
        pair = 2 * HEAD_PAD
        ms = jnp.concatenate([_dot(qsq[:, c * pair:(c + 1) * pair], bdq_ref[...])
                              for c in range(N_HEADS // 2)], axis=-1)
        qn = qa * lax.rsqrt(ms + EPS) * qgain_ref[...]
        is_rope = (lane >= QK_NOPE) & (lane < QK_DIM)
        for h in range(N_HEADS):
            th = qn[:, h * HEAD_PAD:(h + 1) * HEAD_PAD] * tab[:, :LANES]
            rolled = pltpu.roll(th, LANES - QK_ROPE, axis=1)
            oh = jnp.where(lane < QK_DIM, th + jnp.where(is_rope, rolled, 0.0), 0.0)
            q_ref[0, h, rows, :] = oh.astype(BF16)


def _proj_call(x, tab, w, *, tm):
    bsz, seq, dm = x.shape
    s5_width, q_lora, kv_lora = w['s5_width'], w['q_lora'], w['kv_lora']
    in_cols = w['w_in'].shape[1]
    hp = N_HEADS * HEAD_PAD
    full = lambda shape: pl.BlockSpec(shape, lambda b, i: (0,) * len(shape))
    kern = functools.partial(_proj_kernel, s5_width=s5_width, q_lora=q_lora, kv_lora=kv_lora, sub=min(tm, ROW_SUB))
    return pl.pallas_call(
        kern,
        grid=(bsz, seq // tm),
        in_specs=[
            pl.BlockSpec((1, tm, dm), lambda b, i: (b, i, 0)),
            full((1, dm)), full((dm, in_cols)), full((1, q_lora)), full((q_lora, hp)),
            full((2 * HEAD_PAD, 2 * HEAD_PAD)), full((1, hp)), full((1, kv_lora)), full((1, LANES)),
            pl.BlockSpec((tm, 2 * LANES), lambda b, i: (i, 0)),
        ],
        out_specs=[
            pl.BlockSpec((1, tm, s5_width), lambda b, i: (b, i, 0)),
            pl.BlockSpec((1, N_HEADS, tm, HEAD_PAD), lambda b, i: (b, 0, i, 0)),
            pl.BlockSpec((1, tm, kv_lora), lambda b, i: (b, i, 0)),
            pl.BlockSpec((1, tm, QK_ROPE), lambda b, i: (b, i, 0)),
        ],
        out_shape=[
            jax.ShapeDtypeStruct((bsz, seq, s5_width), BF16),
            jax.ShapeDtypeStruct((bsz, N_HEADS, seq, HEAD_PAD), BF16),
            jax.ShapeDtypeStruct((bsz, seq, kv_lora), F32),
            jax.ShapeDtypeStruct((bsz, seq, QK_ROPE), F32),
        ],
        compiler_params=pltpu.CompilerParams(
            dimension_semantics=("parallel", "parallel"), vmem_limit_bytes=VMEM_LIMIT),
    )(x, w['g_mix'], w['w_in'], w['g_q'], w['w_uq'], w['bd_q'], w['q_gain'], w['g_kv'], w['g_kr'], tab)


def _kv_kernel(ckv_ref, krp_ref, wuk_ref, wuv_ref, bdk_ref, kg_ref, k_ref, v_ref, *, sub):
    for r in range(ckv_ref.shape[1] // sub):
        rows = pl.ds(r * sub, sub)
        c = ckv_ref[0, rows, :].astype(BF16)
        ka = _dot(c, wuk_ref[...])
        va = _dot(c, wuv_ref[...])
        ksq = (ka * ka).astype(BF16)
        pair = 2 * HEAD_PAD
        ms = jnp.concatenate([_dot(ksq[:, c2 * pair:(c2 + 1) * pair], bdk_ref[...])
                              for c2 in range(N_HEADS // 2)], axis=-1)
        kn = ka * lax.rsqrt(ms + EPS) * kg_ref[...]
        krp = krp_ref[0, rows, :]
        lane = _lane_iota(krp.shape)
        ones_col = jnp.where(lane == V_DIM, 1.0, 0.0)
        for h in range(N_HEADS):
            k_ref[0, h, rows, :] = (kn[:, h * HEAD_PAD:(h + 1) * HEAD_PAD] + krp).astype(BF16)
            vt = (va[:, h * HEAD_PAD:(h + 1) * HEAD_PAD] + ones_col).T
            v_ref[0, h, :, rows] = vt[:VT_ROWS].astype(BF16)


def _kv_call(ckv, krp, w, *, tm):
    bsz, sk, kv_lora = ckv.shape
    hp = N_HEADS * HEAD_PAD
    full = lambda shape: pl.BlockSpec(shape, lambda b, i: (0,) * len(shape))
    return pl.pallas_call(
        functools.partial(_kv_kernel, sub=ROW_SUB if tm % ROW_SUB == 0 else tm),
        grid=(bsz, sk // tm),
        in_specs=[
            pl.BlockSpec((1, tm, kv_lora), lambda b, i: (b, i, 0)),
            pl.BlockSpec((1, tm, LANES), lambda b, i: (b, i, 0)),
            full((kv_lora, hp)), full((kv_lora, hp)), full((2 * HEAD_PAD, 2 * HEAD_PAD)), full((1, hp)),
        ],
        out_specs=[
            pl.BlockSpec((1, N_HEADS, tm, HEAD_PAD), lambda b, i: (b, 0, i, 0)),
            pl.BlockSpec((1, N_HEADS, VT_ROWS, tm), lambda b, i: (b, 0, 0, i)),
        ],
        out_shape=[
            jax.ShapeDtypeStruct((bsz, N_HEADS, sk, HEAD_PAD), BF16),
            jax.ShapeDtypeStruct((bsz, N_HEADS, VT_ROWS, sk), BF16),
        ],
        compiler_params=pltpu.CompilerParams(
            dimension_semantics=("parallel", "parallel"), vmem_limit_bytes=VMEM_LIMIT),
    )(ckv, krp, w['w_uk'], w['w_uv'], w['bd_k'], w['k_gain'])


def _attn_kernel(q_ref, k_ref, vt_ref, o_ref, *, tq, tk, q_pos0, sk_valid):
    sq = q_ref.shape[2]
    nq = sq // tq

    n_heads = q_ref.shape[1]

    cw = min(tq, ATTN_COLS)
    streams = [(hh, c) for hh in range(n_heads) for c in range(tq // cw)]
    sk = k_ref.shape[2]
    nt = (((1,), (1,)), ((), ()))
    ones_rows = jnp.ones((SUBLANES, HEAD_PAD), BF16)

    def sq_norms(x):
        xf = x.astype(F32)
        return lax.dot_general(ones_rows, (xf * xf).astype(BF16), nt, preferred_element_type=F32)[:1]

    def kmax_body(j, carry):
        k0 = pl.multiple_of(j * tk, tk)
        return tuple(jnp.maximum(carry[hh], sq_norms(k_ref[0, hh, pl.ds(k0, tk), :])) for hh in range(n_heads))

    kmax2 = lax.fori_loop(0, sk // tk, kmax_body, tuple(jnp.zeros((1, tk), F32) for _ in range(n_heads)))
    kmax2 = [jnp.max(v, axis=1, keepdims=True) for v in kmax2]

    def scores(qs, p0, j, width, masked):
        k0 = pl.multiple_of(j * width, width)
        kbs = [k_ref[0, hh, pl.ds(k0, width), :] for hh in range(n_heads)]
        ss = [lax.dot_general(kbs[hh], qs[i], nt, preferred_element_type=F32)
              for i, (hh, c) in enumerate(streams)]
        if masked:
            k_pos = k0 + lax.broadcasted_iota(jnp.int32, (width, 1), 0)
            for c in range(tq // cw):
                q_pos = p0 + c * cw + lax.broadcasted_iota(jnp.int32, (1, cw), 1)
                visible = k_pos < jnp.minimum((q_pos // CHUNK + 1) * CHUNK, sk_valid)
                for i, (hh, c2) in enumerate(streams):
                    if c2 == c:
                        ss[i] = jnp.where(visible, ss[i], NEG_INF)
        return ss

    def values(j, width):
        k0 = pl.multiple_of(j * width, width)
        return [vt_ref[0, hh, :, pl.ds(k0, width)] for hh in range(n_heads)]

    def step_running_max(qs, p0, j, carry, masked):
        ss, vts = scores(qs, p0, j, tk, masked), values(j, tk)
        new = []
        for i, (hh, c) in enumerate(streams):
            m, acc = carry[i]
            m_new = jnp.maximum(m, jnp.max(ss[i], axis=0, keepdims=True))
            acc = jnp.exp2(m - m_new) * acc + _dot(vts[hh], jnp.exp2(ss[i] - m_new).astype(BF16))
            new.append((m_new, acc))
        return tuple(new)

    def step_fixed_ref(qs, refs, p0, j, width, carry, masked):
        ss, vts = scores(qs, p0, j, width, masked), values(j, width)
        return tuple(carry[i] + _dot(vts[hh], jnp.exp2(ss[i] - refs[i]).astype(BF16))
                     for i, (hh, c) in enumerate(streams))

    def q_body(qi, carry):
        q0 = pl.multiple_of(qi * tq, tq)
        qs = [q_ref[0, hh, pl.ds(q0 + c * cw, cw), :] for hh, c in streams]
        p0 = q_pos0 + q0
        n_open = jnp.minimum((p0 // CHUNK + 1) * CHUNK, sk_valid) // tk
        v1 = jnp.minimum(((p0 + tq - 1) // CHUNK + 1) * CHUNK, sk_valid)
        n_all = (v1 + tk - 1) // tk
        refs = [jnp.sqrt(sq_norms(qs[i]) * kmax2[hh]) * REF_MARGIN for i, (hh, c) in enumerate(streams)]
        ref_max = functools.reduce(jnp.maximum, [jnp.max(r) for r in refs])

        def fixed_ref():
            st = tuple(jnp.zeros((VT_ROWS, cw), F32) for _ in streams)
            done = 0
            for span in ATTN_SPANS:
                if span * tk > sk:
                    continue
                first = done // span
                last = n_open // span
                st = lax.fori_loop(first, last,
                                   lambda j, c, span=span: step_fixed_ref(qs, refs, p0, j, span * tk, c, False), st)
                done = last * span
            return lax.fori_loop(n_open, n_all, lambda j, c: step_fixed_ref(qs, refs, p0, j, tk, c, True), st)

        def running_max():
            st = tuple((jnp.full((1, cw), NEG_INF, F32), jnp.zeros((VT_ROWS, cw), F32)) for _ in streams)
            st = lax.fori_loop(0, n_open, lambda j, c: step_running_max(qs, p0, j, c, False), st)
            st = lax.fori_loop(n_open, n_all, lambda j, c: step_running_max(qs, p0, j, c, True), st)
            return tuple(acc for _, acc in st)

        accs = lax.cond(ref_max < REF_LIMIT, fixed_ref, running_max)
        for c in range(tq // cw):
            ot = jnp.concatenate([accs[i][:V_DIM] / accs[i][V_DIM:V_DIM + 1]
                                  for i, (hh, c2) in enumerate(streams) if c2 == c], axis=0)
            o_ref[0, pl.ds(q0 + c * cw, cw), :] = ot.T.astype(o_ref.dtype)
        return carry

    lax.fori_loop(0, nq, q_body, 0)


def _attn_call(q, k, vt, *, tq, tk, q_pos0, sk_valid):
    bsz, nh, sq, _ = q.shape
    sk = k.shape[2]
    assert sq % tq == 0 and sk % tk == 0 and tq % LANES == 0 and tk % LANES == 0
    kern = functools.partial(_attn_kernel, tq=tq, tk=tk, q_pos0=q_pos0, sk_valid=sk_valid)
    return pl.pallas_call(
        kern,
        grid=(bsz, nh // 2),
        in_specs=[
            pl.BlockSpec((1, 2, sq, HEAD_PAD), lambda b, h: (b, h, 0, 0)),
            pl.BlockSpec((1, 2, sk, HEAD_PAD), lambda b, h: (b, h, 0, 0)),
            pl.BlockSpec((1, 2, VT_ROWS, sk), lambda b, h: (b, h, 0, 0)),
        ],
        out_specs=pl.BlockSpec((1, sq, 2 * V_DIM), lambda b, h: (b, 0, h)),
        out_shape=jax.ShapeDtypeStruct((bsz, sq, nh * V_DIM), BF16),
        compiler_params=pltpu.CompilerParams(
            dimension_semantics=("parallel", "parallel"), vmem_limit_bytes=VMEM_LIMIT),
    )(q, k, vt)


def _s5_kernel(u_ref, h0_ref, lre_ref, lim_ref, bbd_ref, cbd_ref, d_ref, wglu_ref, bglu_ref,
               y_ref, hout_ref, hbuf, hstate, u_tb, y_tb, *, ts, n_half, col_chunk):
    i = pl.program_id(0)
    half = n_half
    width = hbuf.shape[1]
    bsz, _, s5_width = u_ref.shape
    uw = s5_width // 2
    n_slab = s5_width // LANES

    @pl.when(i == 0)
    def _():
        hstate[...] = h0_ref[...]

    for b in range(bsz):
        ub = u_ref[b].astype(F32)
        for c in range(n_slab):
            u_tb[c, pl.ds(b, ts, stride=SUBLANES), :] = ub[:, c * LANES:(c + 1) * LANES]
    u32 = jnp.concatenate([u_tb[c] for c in range(n_slab)], axis=-1)
    u = u32.astype(BF16)
    for c in range(2):
        hbuf[:, c * 2 * half:(c + 1) * 2 * half] = _dot(u[:, c * uw:(c + 1) * uw], bbd_ref[c])

    for c in range(2):
        for cc in range(half // col_chunk):
            re0 = c * 2 * half + cc * col_chunk
            im0 = re0 + half
            l0 = c * half + cc * col_chunk
            ar = lre_ref[:, l0:l0 + col_chunk]
            ai = lim_ref[:, l0:l0 + col_chunk]

            def step(s, carry, re0=re0, im0=im0, ar=ar, ai=ai):
                hr, hi = carry
                r0 = pl.multiple_of(s * SUBLANES, SUBLANES)
                nr = ar * hr - ai * hi + hbuf[pl.ds(r0, SUBLANES), re0:re0 + col_chunk]
                ni = ar * hi + ai * hr + hbuf[pl.ds(r0, SUBLANES), im0:im0 + col_chunk]
                hbuf[pl.ds(r0, SUBLANES), re0:re0 + col_chunk] = nr
                hbuf[pl.ds(r0, SUBLANES), im0:im0 + col_chunk] = ni
                return nr, ni

            hr, hi = lax.fori_loop(
                0, ts, step,
                (hstate[:, re0:re0 + col_chunk], hstate[:, im0:im0 + col_chunk]), unroll=8)
            hstate[:, re0:re0 + col_chunk] = hr
            hstate[:, im0:im0 + col_chunk] = hi

    hout_ref[...] = hstate[...]

    y = jnp.concatenate(
        [_dot(hbuf[:, c * 2 * half:(c + 1) * 2 * half].astype(BF16), cbd_ref[c]) for c in range(2)], axis=-1)
    y = y + d_ref[...] * u32
    z = jax.nn.gelu(y)
    gate = _dot(z.astype(BF16), wglu_ref[...]) + bglu_ref[...]
    out = z * jax.nn.sigmoid(gate)
    for c in range(n_slab):
        y_tb[c] = out[:, c * LANES:(c + 1) * LANES]
    for b in range(bsz):
        y_ref[b] = jnp.concatenate([y_tb[c, pl.ds(b, ts, stride=SUBLANES), :] for c in range(n_slab)],
                                   axis=-1).astype(y_ref.dtype)


def _s5_call(u, h0, w, *, ts):
    bsz, seq, s5_width = u.shape
    assert bsz == SUBLANES, "the S5 scan keeps the batch on the sublanes"
    n_state = w['lam_re'].shape[1]
    half = n_state // 2
    width = 2 * n_state
    n_steps = seq // ts
    full = lambda shape: pl.BlockSpec(shape, lambda i: (0,) * len(shape))
    kern = functools.partial(_s5_kernel, ts=ts, n_half=half, col_chunk=512)
    return pl.pallas_call(
        kern,
        grid=(n_steps,),
        in_specs=[
            pl.BlockSpec((bsz, ts, s5_width), lambda i: (0, i, 0)),
            full((SUBLANES, width)), full((SUBLANES, n_state)), full((SUBLANES, n_state)),
            full((2, s5_width // 2, 2 * half)), full((2, 2 * half, s5_width // 2)),
            full((1, s5_width)), full((s5_width, s5_width)), full((1, s5_width)),
        ],
        out_specs=[
            pl.BlockSpec((bsz, ts, s5_width), lambda i: (0, i, 0)),
            full((SUBLANES, width)),
        ],
        out_shape=[
            jax.ShapeDtypeStruct((bsz, seq, s5_width), BF16),
            jax.ShapeDtypeStruct((SUBLANES, width), F32),
        ],
        scratch_shapes=[
            pltpu.VMEM((ts * SUBLANES, width), F32),
            pltpu.VMEM((SUBLANES, width), F32),
            pltpu.VMEM((s5_width // LANES, ts * SUBLANES, LANES), F32),
            pltpu.VMEM((s5_width // LANES, ts * SUBLANES, LANES), F32),
        ],
        compiler_params=pltpu.CompilerParams(
            dimension_semantics=("arbitrary",), vmem_limit_bytes=VMEM_LIMIT),
    )(u, h0, w['lam_re'], w['lam_im'], w['b_bd'], w['c_bd'], w['s5_d'], w['w_glu'], w['b_glu'])


def _out_kernel(x_ref, s5_ref, at_ref, gs5_ref, gat_ref, wout_ref, gffn_ref, wr_ref, br_ref,
                h_ref, route_ref, *, sub):
    for r in range(x_ref.shape[1] // sub):
        rows = pl.ds(r * sub, sub)
        s5n = _rms(s5_ref[0, rows, :].astype(F32), gs5_ref[...])
        atn = _rms(at_ref[0, rows, :].astype(F32), gat_ref[...])
        merged = jnp.concatenate([s5n, atn], axis=-1).astype(BF16)
        h = x_ref[0, rows, :] + _dot(merged, wout_ref[...])
        for j in range(h.shape[1] // LANES):
            h_ref[pl.ds(r * sub * SUBLANES + j, sub, stride=SUBLANES), :] = h[:, j * LANES:(j + 1) * LANES]
        hn = _rms(h, gffn_ref[...]).astype(BF16)

        logits = _dot(hn, wr_ref[...]) + br_ref[...]
        lane_i = _lane_iota(logits.shape)
        lane = lane_i.astype(F32)
        lane_group = (lane_i // EXPERTS_PER_GROUP).astype(F32)
        big = float(LANES)
        is_g = (lane_i >= N_EXPERTS) & (lane_i < N_EXPERTS + N_EXPERT_GROUPS)
        gl = jnp.where(is_g, logits, NEG_INF)
        gmax = jnp.max(gl, axis=-1, keepdims=True)
        g_idx = jnp.min(jnp.where(gl == gmax, lane, big), axis=-1, keepdims=True) - N_EXPERTS
        g_top = 1.0 / jnp.sum(jnp.where(is_g, jnp.exp(gl - gmax), 0.0), axis=-1, keepdims=True)
        in_group = (lane_i < N_EXPERTS) & (lane_group == g_idx)
        el = jnp.where(in_group, logits, NEG_INF)
        v1 = jnp.max(el, axis=-1, keepdims=True)
        i1 = jnp.min(jnp.where(el == v1, lane, big), axis=-1, keepdims=True)
        el2 = jnp.where(lane == i1, NEG_INF, el)
        v2 = jnp.max(el2, axis=-1, keepdims=True)
        i2 = jnp.min(jnp.where(el2 == v2, lane, big), axis=-1, keepdims=True)
        e21 = jnp.exp(v2 - v1)
        w1 = g_top / (1.0 + e21)
        w2 = g_top * e21 / (1.0 + e21)
        route_ref[0, rows, :] = (jnp.where(lane_i == 0, i1, 0.0) + jnp.where(lane_i == 1, i2, 0.0)
                                 + jnp.where(lane_i == 2, w1, 0.0) + jnp.where(lane_i == 3, w2, 0.0))


def _out_call(x, s5_out, attn, w, *, tm):
    bsz, seq, dm = x.shape
    s5_width = s5_out.shape[2]
    aw = attn.shape[2]
    full = lambda shape: pl.BlockSpec(shape, lambda b, i: (0,) * len(shape))
    tok = lambda width: pl.BlockSpec((1, tm, width), lambda b, i: (b, i, 0))
    return pl.pallas_call(
        functools.partial(_out_kernel, sub=min(tm, ROW_SUB)),
        grid=(bsz, seq // tm),
        in_specs=[
            tok(dm),
            tok(s5_width),
            tok(aw),
            full((1, s5_width)), full((1, aw)), full((s5_width + aw, dm)), full((1, dm)),
            full((dm, LANES)), full((1, LANES)),
        ],
        out_specs=[
            pl.BlockSpec((tm * SUBLANES, LANES), lambda b, i: (b * (seq // tm) + i, 0)),
            tok(LANES),
        ],
        out_shape=[
            jax.ShapeDtypeStruct((bsz * seq * SUBLANES, LANES), F32),
            jax.ShapeDtypeStruct((bsz, seq, LANES), F32),
        ],
        compiler_params=pltpu.CompilerParams(
            dimension_semantics=("parallel", "parallel"), vmem_limit_bytes=VMEM_LIMIT),
    )(x, s5_out, attn, w['g_s5'], w['g_attn'], w['w_out'], w['g_ffn'], w['w_router'], w['b_router'])


def _moe_kernel(offs_ref, toks_ref, gws_ref, h_hbm, gffn_ref, wgu_ref, wd_ref, o_ref, acc, hn_ref, gbuf, ybuf, sem,
                *, d_expert, tb, mc, norm_rows):
    b = pl.program_id(0)
    e = pl.program_id(1)
    tok_rows = SUBLANES
    dm = tok_rows * LANES

    @pl.when(e == 0)
    def _():
        cp = pltpu.make_async_copy(h_hbm.at[pl.ds(b * tb * tok_rows, tb * tok_rows)], acc, sem)
        cp.start()
        gbuf[...] = jnp.zeros_like(gbuf)
        cp.wait()

        def norm(i, carry):
            r0 = pl.multiple_of(i * norm_rows * tok_rows, norm_rows * tok_rows)
            x3 = acc[pl.ds(r0, norm_rows * tok_rows), :].reshape(norm_rows, tok_rows, LANES)
            ms = jnp.sum(jnp.sum(x3 * x3, axis=2, keepdims=True), axis=1, keepdims=True) * (1.0 / dm)
            hn3 = x3 * lax.rsqrt(ms + EPS) * gffn_ref[...][None]
            hn_ref[pl.ds(r0, norm_rows * tok_rows), :] = hn3.reshape(norm_rows * tok_rows, LANES)
            return carry

        lax.fori_loop(0, tb // norm_rows, norm, 0)

    n_exp = wgu_ref.shape[0]
    offs = [offs_ref[0, 0, e * n_exp + k] for k in range(n_exp + 1)]
    cnts = [offs[k + 1] - offs[k] for k in range(n_exp)]

    def chunk(c, carry):
        bases = [offs[k] + c * mc for k in range(n_exp)]
        ns = [jnp.clip(cnts[k] - c * mc, 0, mc) for k in range(n_exp)]

        for k in range(n_exp):
            def gather8(g, carry, k=k):
                for u in range(SUBLANES):
                    i = g * SUBLANES + u
                    row = pl.multiple_of(toks_ref[0, 0, bases[k] + i], tok_rows)
                    gbuf[k, pl.ds(pl.multiple_of(i * tok_rows, tok_rows), tok_rows), :] = (
                        hn_ref[pl.ds(row, tok_rows), :])
                return carry

            lax.fori_loop(0, (ns[k] + SUBLANES - 1) // SUBLANES, gather8, 0)

        for k in range(n_exp):
            x = jnp.concatenate([gbuf[k, pl.ds(j, mc, stride=tok_rows), :].astype(BF16) for j in range(tok_rows)],
                                axis=-1)
            gu = _dot(x, wgu_ref[k])
            hdn = jax.nn.silu(gu[:, :d_expert]) * gu[:, d_expert:]
            y = _dot(hdn.astype(BF16), wd_ref[k])
            for j in range(tok_rows):
                ybuf[k, pl.ds(j, mc, stride=tok_rows), :] = y[:, j * LANES:(j + 1) * LANES]

        for k in range(n_exp):
            def add_rows(first, count, k=k):
                idx = [first + u for u in range(count)]
                dst = [pl.multiple_of(toks_ref[0, 0, bases[k] + i], tok_rows) for i in idx]
                val = [acc[pl.ds(d, tok_rows), :] + gws_ref[0, 0, bases[k] + i]
                       * ybuf[k, pl.ds(pl.multiple_of(i * tok_rows, tok_rows), tok_rows), :]
                       for i, d in zip(idx, dst)]
                for d, v in zip(dst, val):
                    acc[pl.ds(d, tok_rows), :] = v

            n_full = ns[k] // SUBLANES

            def scatter8(g, carry, add_rows=add_rows):
                add_rows(g * SUBLANES, SUBLANES)
                return carry

            def scatter1(i, carry, add_rows=add_rows):
                add_rows(i, 1)
                return carry

            lax.fori_loop(0, n_full, scatter8, 0)
            lax.fori_loop(n_full * SUBLANES, ns[k], scatter1, 0)
        return carry

    n_chunks = functools.reduce(jnp.maximum, [(cnt + mc - 1) // mc for cnt in cnts])
    lax.fori_loop(0, n_chunks, chunk, 0)

    @pl.when(e == pl.num_programs(1) - 1)
    def _():
        for j in range(tok_rows):
            o_ref[:, j * LANES:(j + 1) * LANES] = acc[pl.ds(j, tb, stride=tok_rows), :]


def _moe_call(offs, toks, gws, h_tiles, w, *, tb, mc):
    tok_rows = SUBLANES
    dm = tok_rows * LANES
    t = h_tiles.shape[0] // tok_rows
    assert w['w_gu'].shape[1] == dm, "a token row must be exactly one (8,128) f32 tile"
    nb = t // tb
    d_expert = w['w_d'].shape[1]
    kern = functools.partial(_moe_kernel, d_expert=d_expert, tb=tb, mc=mc, norm_rows=min(tb, 256))
    smem = lambda n: pl.BlockSpec((1, 1, n), lambda b, e: (b, 0, 0), memory_space=pltpu.SMEM)
    n_exp = MOE_EXPERTS_PER_STEP
    return pl.pallas_call(
        kern,
        grid=(nb, N_EXPERTS // n_exp),
        in_specs=[
            smem(offs.shape[2]), smem(toks.shape[2]), smem(gws.shape[2]),
            pl.BlockSpec(memory_space=pl.ANY),
            pl.BlockSpec((tok_rows, LANES), lambda b, e: (0, 0)),
            pl.BlockSpec((n_exp, dm, 2 * d_expert), lambda b, e: (e, 0, 0)),
            pl.BlockSpec((n_exp, d_expert, dm), lambda b, e: (e, 0, 0)),
        ],
        out_specs=pl.BlockSpec((tb, dm), lambda b, e: (b, 0)),
        out_shape=jax.ShapeDtypeStruct((t, dm), F32),
        scratch_shapes=[
            pltpu.VMEM((tb * tok_rows, LANES), F32),
            pltpu.VMEM((tb * tok_rows, LANES), F32),
            pltpu.VMEM((n_exp, mc * tok_rows, LANES), F32),
            pltpu.VMEM((n_exp, mc * tok_rows, LANES), F32),
            pltpu.SemaphoreType.DMA(()),
        ],
        compiler_params=pltpu.CompilerParams(
            dimension_semantics=("parallel", "arbitrary"), vmem_limit_bytes=VMEM_LIMIT),
    )(offs, toks, gws, h_tiles, w['g_ffn'].reshape(tok_rows, LANES), w['w_gu'], w['w_d'])


def _route_tables(route, tb):
    t = route.shape[0]
    nb = t // tb
    ids = route[:, :2].astype(jnp.int32).reshape(nb, 2 * tb)
    wts = route[:, 2:4].reshape(nb, 2 * tb)
    order = jnp.argsort(ids, axis=1, stable=True)
    toks = (order // 2 * SUBLANES).astype(jnp.int32)
    toks = jnp.pad(toks, ((0, 0), (0, SUBLANES)))
    gws = jnp.take_along_axis(wts, order, axis=1)
    counts = jnp.sum(ids[:, :, None] == jnp.arange(N_EXPERTS, dtype=jnp.int32), axis=1, dtype=jnp.int32)
    offs = jnp.concatenate([jnp.zeros((nb, 1), jnp.int32), jnp.cumsum(counts, axis=1)], axis=1)
    offs = jnp.pad(offs, ((0, 0), (0, LANES - offs.shape[1])))
    return offs[:, None, :], toks[:, None, :], gws[:, None, :]


def _prepare_weights(p):
    dm, in_cols = p['w_in'].shape
    s5_width = p['s5_d'].shape[0]
    q_lora = p['q_lora_norm_g'].shape[0]
    kv_lora = p['kv_lora_norm_g'].shape[0]
    half_r = QK_ROPE // 2
    w = {'s5_width': s5_width, 'q_lora': q_lora, 'kv_lora': kv_lora}

    def partner(a):
        return jnp.concatenate([-a[..., half_r:], a[..., :half_r]], axis=-1)

    def swap(a):
        return jnp.concatenate([a[..., half_r:], a[..., :half_r]], axis=-1)

    w_kr = p['w_in'][:, in_cols - QK_ROPE:]
    w['w_in'] = jnp.concatenate(
        [p['w_in'], partner(w_kr), jnp.zeros((dm, LANES - 2 * QK_ROPE), F32)], axis=-1).astype(BF16)
    w['g_mix'] = p['norm_mix_g'][None]
    w['g_q'] = p['q_lora_norm_g'][None]
    w['g_kv'] = p['kv_lora_norm_g'][None]
    gkr = p['k_rope_norm_g']
    w['g_kr'] = jnp.concatenate([gkr, swap(gkr), jnp.zeros((LANES - 2 * QK_ROPE,), F32)])[None]

    wq = p['w_uq']
    wq_r = wq[..., QK_NOPE:]
    w['w_uq'] = jnp.concatenate([wq, partner(wq_r)], axis=-1).reshape(q_lora, N_HEADS * HEAD_PAD).astype(BF16)
    gr = p['q_rope_norm_g']
    w['q_gain'] = jnp.tile(jnp.concatenate([p['q_nope_norm_g'], gr, swap(gr)]), N_HEADS)[None]
    idx = jnp.arange(HEAD_PAD)
    row_nope = (idx < QK_NOPE)[:, None]
    row_rope = ((idx >= QK_NOPE) & (idx < QK_DIM))[:, None]
    col_nope = (idx < QK_NOPE)[None, :]
    m_q = jnp.where(row_nope & col_nope, 1.0 / QK_NOPE, 0.0) + jnp.where(row_rope & ~col_nope, 1.0 / QK_ROPE, 0.0)
    m_k = jnp.where(row_nope, 1.0 / QK_NOPE, 0.0) * jnp.ones((1, HEAD_PAD), F32)
    eye2 = jnp.eye(2, dtype=F32)
    w['bd_q'] = jnp.kron(eye2, m_q).astype(BF16)
    w['bd_k'] = jnp.kron(eye2, m_k).astype(BF16)

    pad_h = lambda a: jnp.pad(a, ((0, 0), (0, 0), (0, HEAD_PAD - a.shape[-1])))
    w['w_uk'] = pad_h(p['w_uk']).reshape(kv_lora, N_HEADS * HEAD_PAD).astype(BF16)
    w['w_uv'] = pad_h(p['w_uv']).reshape(kv_lora, N_HEADS * HEAD_PAD).astype(BF16)
    w['k_gain'] = jnp.tile(jnp.pad(p['k_nope_norm_g'], (0, HEAD_PAD - QK_NOPE)), N_HEADS)[None]

    dt = jnp.exp(p['s5_log_dt'])[:, None]
    ar, ai = p['s5_a_re'], p['s5_a_im']
    mag = jnp.exp(dt * ar)
    abar_re = mag * jnp.cos(dt * ai)
    abar_im = mag * jnp.sin(dt * ai)
    den = ar * ar + ai * ai
    num_re = abar_re - 1.0
    coef_re = (num_re * ar + abar_im * ai) / den
    coef_im = (abar_im * ar - num_re * ai) / den
    br, bi = p['s5_b_re'], p['s5_b_im']
    bbar_re = coef_re[..., None] * br - coef_im[..., None] * bi
    bbar_im = coef_re[..., None] * bi + coef_im[..., None] * br
    g = ar.shape[0]
    n_state = g * S5_STATE
    half = n_state // 2
    eye_g = jnp.eye(g, dtype=F32)
    b_re = jnp.einsum('gni,gh->gihn', bbar_re, eye_g).reshape(s5_width, n_state)
    b_im = jnp.einsum('gni,gh->gihn', bbar_im, eye_g).reshape(s5_width, n_state)
    c_re = jnp.einsum('gon,gh->gnho', p['s5_c_re'], eye_g).reshape(n_state, s5_width)
    c_im = jnp.einsum('gon,gh->gnho', p['s5_c_im'], eye_g).reshape(n_state, s5_width)
    uw = s5_width // 2
    w['b_bd'] = jnp.stack([
        jnp.concatenate([b_re[c * uw:(c + 1) * uw, c * half:(c + 1) * half],
                         b_im[c * uw:(c + 1) * uw, c * half:(c + 1) * half]], axis=1) for c in range(2)]).astype(BF16)
    w['c_bd'] = jnp.stack([
        jnp.concatenate([c_re[c * half:(c + 1) * half, c * uw:(c + 1) * uw],
                         -c_im[c * half:(c + 1) * half, c * uw:(c + 1) * uw]], axis=0) for c in range(2)]).astype(BF16)
    w['lam_re'] = jnp.broadcast_to(abar_re.reshape(1, n_state), (SUBLANES, n_state))
    w['lam_im'] = jnp.broadcast_to(abar_im.reshape(1, n_state), (SUBLANES, n_state))
    w['s5_d'] = p['s5_d'][None]
    w['w_glu'] = p['s5_w_glu'].astype(BF16)
    w['b_glu'] = p['s5_b_glu'][None]

    w['g_s5'] = p['out_norm_s5_g'][None]
    w['g_attn'] = p['out_norm_attn_g'][None]
    w['w_out'] = p['w_out'].astype(BF16)
    w['g_ffn'] = p['norm_ffn_g'][None]
    n_r = N_EXPERTS + N_EXPERT_GROUPS
    w['w_router'] = jnp.pad(jnp.concatenate([p['w_router_expert'], p['w_router_group']], axis=1),
                            ((0, 0), (0, LANES - n_r))).astype(BF16)
    w['b_router'] = jnp.pad(jnp.concatenate([p['b_router_expert'], p['b_router_group']]), (0, LANES - n_r))[None]
    w['w_gu'] = jnp.concatenate([p['w_e_gate'], p['w_e_up']], axis=-1).astype(BF16)
    w['w_d'] = p['w_e_down'].astype(BF16)
    return w


def _rope_table(pos):
    half = QK_ROPE // 2
    inv = ROPE_THETA ** (-jnp.arange(half, dtype=F32) / half)
    ang = pos.astype(F32)[:, None] * inv[None, :]
    cc = jnp.tile(jnp.cos(ang), (1, 2))
    ss = jnp.tile(jnp.sin(ang), (1, 2))
    n = pos.shape[0]
    scale = QK_DIM ** -0.5 * math.log2(math.e)
    tab_q = scale * jnp.concatenate([jnp.ones((n, QK_NOPE), F32), cc, ss], axis=-1)
    tab_k = jnp.concatenate([cc, ss, jnp.zeros((n, LANES - 2 * QK_ROPE), F32)], axis=-1)
    return jnp.concatenate([tab_q, tab_k], axis=-1)


def _s5_state_in(re, im):
    b = re.shape[0]
    re = re.reshape(b, -1)
    im = im.reshape(b, -1)
    half = re.shape[1] // 2
    return jnp.concatenate([re[:, :half], im[:, :half], re[:, half:], im[:, half:]], axis=1)


def _s5_state_out(h, g):
    b = h.shape[0]
    half = h.shape[1] // 4
    re = jnp.concatenate([h[:, :half], h[:, 2 * half:3 * half]], axis=1).reshape(b, g, S5_STATE)
    im = jnp.concatenate([h[:, half:2 * half], h[:, 3 * half:]], axis=1).reshape(b, g, S5_STATE)
    return re, im


def _trunk_layer(x, past_ckv, past_krope, h0_re, h0_im, w):
    bsz, seq, dm = x.shape
    assert bsz == SUBLANES, "the S5 scan keeps the batch on the sublanes"
    past_len = 0 if past_ckv is None else past_ckv.shape[1]
    g = w['lam_re'].shape[1] // S5_STATE
    tm = _tile(seq, ROW_TILE)

    tab = _rope_table(past_len + jnp.arange(seq, dtype=jnp.int32))
    u, q, new_ckv, new_krope = _proj_call(x, tab, w, tm=tm)

    if h0_re is None:
        h0 = jnp.zeros((bsz, 2 * g * S5_STATE), F32)
    else:
        h0 = _s5_state_in(h0_re.astype(F32), h0_im.astype(F32))
    s5_out, h_last = _s5_call(u, h0, w, ts=min(seq, 64))
    h_re, h_im = _s5_state_out(h_last, g)

    if past_ckv is None:
        all_ckv, all_krope = new_ckv, new_krope
    else:
        all_ckv = jnp.concatenate([past_ckv.astype(F32), new_ckv], axis=1)
        all_krope = jnp.concatenate([past_krope.astype(F32), new_krope], axis=1)
    sk = all_ckv.shape[1]
    sk_pad = -(-sk // LANES) * LANES
    all_ckv = jnp.pad(all_ckv, ((0, 0), (0, sk_pad - sk), (0, 0)))
    krp = jnp.pad(all_krope, ((0, 0), (0, sk_pad - sk), (QK_NOPE, LANES - QK_DIM)))
    whole = sk_pad % ATTN_TK != 0
    k, vt = _kv_call(all_ckv, krp, w, tm=sk_pad if whole else KV_TILE)
    sq_pad = -(-seq // LANES) * LANES
    q = jnp.pad(q, ((0, 0), (0, 0), (0, sq_pad - seq), (0, 0)))
    tq = ATTN_TQ if sq_pad % ATTN_TQ == 0 else LANES
    attn = _attn_call(q, k, vt, tq=tq, tk=sk_pad if whole else ATTN_TK, q_pos0=past_len, sk_valid=sk)[:, :seq]

    h_tiles, route = _out_call(x, s5_out, attn, w, tm=tm)
    t = bsz * seq
    tb = _tile(t, MOE_BLOCK)
    offs, toks, gws = _route_tables(route.reshape(t, LANES), tb)
    y = _moe_call(offs, toks, gws, h_tiles, w, tb=tb, mc=MOE_CHUNK)
    return y.reshape(bsz, seq, dm), new_ckv, new_krope, h_re, h_im


def kernel(x_prompt, x_sample, cache_ckv, cache_krope, state_s5_re, state_s5_im, norm_mix_g, w_in, s5_a_re, s5_a_im, s5_log_dt, s5_b_re, s5_b_im, s5_c_re, s5_c_im, s5_d, s5_w_glu, s5_b_glu, q_lora_norm_g, w_uq, kv_lora_norm_g, w_uk, w_uv, q_nope_norm_g, q_rope_norm_g, k_nope_norm_g, k_rope_norm_g, out_norm_s5_g, out_norm_attn_g, w_out, norm_ffn_g, w_router_group, b_router_group, w_router_expert, b_router_expert, w_e_gate, w_e_up, w_e_down):
    params = dict(
        norm_mix_g=norm_mix_g, w_in=w_in, s5_a_re=s5_a_re, s5_a_im=s5_a_im, s5_log_dt=s5_log_dt,
        s5_b_re=s5_b_re, s5_b_im=s5_b_im, s5_c_re=s5_c_re, s5_c_im=s5_c_im, s5_d=s5_d,
        s5_w_glu=s5_w_glu, s5_b_glu=s5_b_glu, q_lora_norm_g=q_lora_norm_g, w_uq=w_uq,
        kv_lora_norm_g=kv_lora_norm_g, w_uk=w_uk, w_uv=w_uv, q_nope_norm_g=q_nope_norm_g,
        q_rope_norm_g=q_rope_norm_g, k_nope_norm_g=k_nope_norm_g, k_rope_norm_g=k_rope_norm_g,
        out_norm_s5_g=out_norm_s5_g, out_norm_attn_g=out_norm_attn_g, w_out=w_out, norm_ffn_g=norm_ffn_g,
        w_router_group=w_router_group, b_router_group=b_router_group,
        w_router_expert=w_router_expert, b_router_expert=b_router_expert,
        w_e_gate=w_e_gate, w_e_up=w_e_up, w_e_down=w_e_down)
    depth = w_in.shape[0]
    y_p, y_s = x_prompt, x_sample
    outs = [[] for _ in range(8)]
    for l in range(depth):
        w = _prepare_weights({k: a[l] for k, a in params.items()})
        y_p, c1, k1, r1, i1 = _trunk_layer(y_p, None, None, None, None, w)
        y_s, c2, k2, r2, i2 = _trunk_layer(y_s, cache_ckv[l], cache_krope[l], state_s5_re[l], state_s5_im[l], w)
        for lst, a in zip(outs, (c1, k1, r1, i1, c2, k2, r2, i2)):
            lst.append(a)
    return (y_p, y_s) + tuple(jnp.stack(lst) for lst in outs)
```

```python
import functools
import math

import jax
import jax.numpy as jnp
from jax import lax
from jax.experimental import pallas as pl
from jax.experimental.pallas import tpu as pltpu

F32 = jnp.float32
BF16 = jnp.bfloat16

CHUNK = 64
S5_GROUP = 16
S5_STATE = 64
N_HEADS = 8
QK_NOPE = 64
QK_ROPE = 32
QK_DIM = QK_NOPE + QK_ROPE
V_DIM = 64
ROPE_THETA = 10000.0
N_EXPERT_GROUPS = 4
EXPERTS_PER_GROUP = 8
N_EXPERTS = N_EXPERT_GROUPS * EXPERTS_PER_GROUP
EPS = 1e-6
NEG_INF = -1e30

LANES = 128
SUBLANES = 8
HEAD_PAD = LANES
ATTN_TQ = 512
ATTN_TK = 512
KV_TILE = 1024
ROW_TILE = 1024
ROW_SUB = 512
MOE_BLOCK = 2048
MOE_CHUNK = 160
NORM_ROWS = 2048
MOE_EXPERTS_PER_STEP = 2
REF_MARGIN = 1.02
REF_LIMIT = 40.0
ATTN_SPANS = (4, 2, 1)
ATTN_COLS = 256
VT_ROWS = 80
VMEM_LIMIT = 48 * 1024 * 1024


def _tile(n, cap):
    for t in range(min(cap, n), 15, -1):
        if n % t == 0 and t % 16 == 0:
            return t
    return n


def _rms(x, g):
    return x * lax.rsqrt(jnp.mean(x * x, axis=-1, keepdims=True) + EPS) * g


def _dot(a, b):
    return jnp.dot(a, b, preferred_element_type=F32)


def _lane_iota(shape):
    return lax.broadcasted_iota(jnp.int32, shape, len(shape) - 1)


def _proj_kernel(x_ref, gmix_ref, win_ref, gq_ref, wuq_ref, bdq_ref, qgain_ref, gkv_ref, gkr_ref, tab_ref,
                 u_ref, q_ref, ckv_ref, kr_ref, *, s5_width, q_lora, kv_lora, sub):
    for r in range(x_ref.shape[1] // sub):
        rows = pl.ds(r * sub, sub)
        x = x_ref[0, rows, :]
        xn = _rms(x, gmix_ref[...])
        proj = _dot(xn.astype(BF16), win_ref[...])
        u_ref[0, rows, :] = proj[:, :s5_width].astype(BF16)

        o = s5_width
        cqn = _rms(proj[:, o:o + q_lora], gq_ref[...])
        o += q_lora
        ckv_ref[0, rows, :] = _rms(proj[:, o:o + kv_lora], gkv_ref[...])
        o += kv_lora

        tab = tab_ref[rows, :]
        lane = _lane_iota((sub, LANES))

        kr = proj[:, o:o + LANES]
        ms = jnp.sum(jnp.where(lane < QK_ROPE, kr * kr, 0.0), axis=-1, keepdims=True) * (1.0 / QK_ROPE)
        t = kr * lax.rsqrt(ms + EPS) * gkr_ref[...] * tab[:, LANES:]
        kro = t + pltpu.roll(t, LANES - QK_ROPE, axis=1)
        kr_ref[0, rows, :] = kro[:, :QK_ROPE]

        qa = _dot(cqn.astype(BF16), wuq_ref[...])
        qsq = (qa * qa).astype(BF16)
        pair = 2 * HEAD_PAD
        ms = jnp.concatenate([_dot(qsq[:, c * pair:(c + 1) * pair], bdq_ref[...])
                              for c in range(N_HEADS // 2)], axis=-1)
        qn = qa * lax.rsqrt(ms + EPS) * qgain_ref[...]
        is_rope = (lane >= QK_NOPE) & (lane < QK_DIM)
        for h in range(N_HEADS):
            th = qn[:, h * HEAD_PAD:(h + 1) * HEAD_PAD] * tab[:, :LANES]
            rolled = pltpu.roll(th, LANES - QK_ROPE, axis=1)
            oh = jnp.where(lane < QK_DIM, th + jnp.where(is_rope, rolled, 0.0), 0.0)
            q_ref[0, h, rows, :] = oh.astype(BF16)


def _proj_call(x, tab, w, *, tm):
    bsz, seq, dm = x.shape
    s5_width, q_lora, kv_lora = w['s5_width'], w['q_lora'], w['kv_lora']
    in_cols = w['w_in'].shape[1]
    hp = N_HEADS * HEAD_PAD
    full = lambda shape: pl.BlockSpec(shape, lambda b, i: (0,) * len(shape))
    kern = functools.partial(_proj_kernel, s5_width=s5_width, q_lora=q_lora, kv_lora=kv_lora, sub=min(tm, ROW_SUB))
    return pl.pallas_call(
        kern,
        grid=(bsz, seq // tm),
        in_specs=[
            pl.BlockSpec((1, tm, dm), lambda b, i: (b, i, 0)),
            full((1, dm)), full((dm, in_cols)), full((1, q_lora)), full((q_lora, hp)),
            full((2 * HEAD_PAD, 2 * HEAD_PAD)), full((1, hp)), full((1, kv_lora)), full((1, LANES)),
            pl.BlockSpec((tm, 2 * LANES), lambda b, i: (i, 0)),
        ],
        out_specs=[
            pl.BlockSpec((1, tm, s5_width), lambda b, i: (b, i, 0)),
            pl.BlockSpec((1, N_HEADS, tm, HEAD_PAD), lambda b, i: (b, 0, i, 0)),
            pl.BlockSpec((1, tm, kv_lora), lambda b, i: (b, i, 0)),
            pl.BlockSpec((1, tm, QK_ROPE), lambda b, i: (b, i, 0)),
        ],
        out_shape=[
            jax.ShapeDtypeStruct((bsz, seq, s5_width), BF16),
            jax.ShapeDtypeStruct((bsz, N_HEADS, seq, HEAD_PAD), BF16),
            jax.ShapeDtypeStruct((bsz, seq, kv_lora), F32),
            jax.ShapeDtypeStruct((bsz, seq, QK_ROPE), F32),
        ],
        compiler_params=pltpu.CompilerParams(
            dimension_semantics=("parallel", "parallel"), vmem_limit_bytes=VMEM_LIMIT),
    )(x, w['g_mix'], w['w_in'], w['g_q'], w['w_uq'], w['bd_q'], w['q_gain'], w['g_kv'], w['g_kr'], tab)


def _kv_kernel(ckv_ref, krp_ref, wuk_ref, wuv_ref, bdk_ref, kg_ref, k_ref, v_ref, *, sub):
    for r in range(ckv_ref.shape[1] // sub):
        rows = pl.ds(r * sub, sub)
        c = ckv_ref[0, rows, :].astype(BF16)
        ka = _dot(c, wuk_ref[...])
        va = _dot(c, wuv_ref[...])
        ksq = (ka * ka).astype(BF16)
        pair = 2 * HEAD_PAD
        ms = jnp.concatenate([_dot(ksq[:, c2 * pair:(c2 + 1) * pair], bdk_ref[...])
                              for c2 in range(N_HEADS // 2)], axis=-1)
        kn = ka * lax.rsqrt(ms + EPS) * kg_ref[...]
        krp = krp_ref[0, rows, :]
        lane = _lane_iota(krp.shape)
        ones_col = jnp.where(lane == V_DIM, 1.0, 0.0)
        for h in range(N_HEADS):
            k_ref[0, h, rows, :] = (kn[:, h * HEAD_PAD:(h + 1) * HEAD_PAD] + krp).astype(BF16)
            vt = (va[:, h * HEAD_PAD:(h + 1) * HEAD_PAD] + ones_col).T
            v_ref[0, h, :, rows] = vt[:VT_ROWS].astype(BF16)


def _kv_call(ckv, krp, w, *, tm):
    bsz, sk, kv_lora = ckv.shape
    hp = N_HEADS * HEAD_PAD
    full = lambda shape: pl.BlockSpec(shape, lambda b, i: (0,) * len(shape))
    return pl.pallas_call(
        functools.partial(_kv_kernel, sub=ROW_SUB if tm % ROW_SUB == 0 else tm),
        grid=(bsz, sk // tm),
        in_specs=[
            pl.BlockSpec((1, tm, kv_lora), lambda b, i: (b, i, 0)),
            pl.BlockSpec((1, tm, LANES), lambda b, i: (b, i, 0)),
            full((kv_lora, hp)), full((kv_lora, hp)), full((2 * HEAD_PAD, 2 * HEAD_PAD)), full((1, hp)),
        ],
        out_specs=[
            pl.BlockSpec((1, N_HEADS, tm, HEAD_PAD), lambda b, i: (b, 0, i, 0)),
            pl.BlockSpec((1, N_HEADS, VT_ROWS, tm), lambda b, i: (b, 0, 0, i)),
        ],
        out_shape=[
            jax.ShapeDtypeStruct((bsz, N_HEADS, sk, HEAD_PAD), BF16),
            jax.ShapeDtypeStruct((bsz, N_HEADS, VT_ROWS, sk), BF16),
        ],
        compiler_params=pltpu.CompilerParams(
            dimension_semantics=("parallel", "parallel"), vmem_limit_bytes=VMEM_LIMIT),
    )(ckv, krp, w['w_uk'], w['w_uv'], w['bd_k'], w['k_gain'])


def _attn_kernel(q_ref, k_ref, vt_ref, o_ref, ref_scr, *, tq, tk, q_pos0, sk_valid):
    sq = q_ref.shape[2]
    nq = sq // tq

    n_heads = q_ref.shape[1]

    cw = min(tq, ATTN_COLS)
    streams = [(hh, c) for hh in range(n_heads) for c in range(tq // cw)]
    sk = k_ref.shape[2]
    nt = (((1,), (1,)), ((), ()))
    ones_rows = jnp.ones((SUBLANES, HEAD_PAD), BF16)

    def sq_norms(x):
        xf = x.astype(F32)
        return lax.dot_general(ones_rows, (xf * xf).astype(BF16), nt, preferred_element_type=F32)[:1]

    nk = NORM_ROWS if sk % NORM_ROWS == 0 else tk

    def kmax_body(j, carry):
        k0 = pl.multiple_of(j * nk, nk)
        return tuple(jnp.maximum(carry[hh], sq_norms(k_ref[0, hh, pl.ds(k0, nk), :])) for hh in range(n_heads))

    kmax2 = lax.fori_loop(0, sk // nk, kmax_body, tuple(jnp.zeros((1, nk), F32) for _ in range(n_heads)))
    kmax2 = [jnp.max(v, axis=1, keepdims=True) for v in kmax2]

    nq_rows = NORM_ROWS if sq % NORM_ROWS == 0 else tq

    def ref_body(j, carry):
        r0 = pl.multiple_of(j * nq_rows, nq_rows)
        for hh in range(n_heads):
            ref_scr[hh, :, pl.ds(r0, nq_rows)] = jnp.sqrt(
                sq_norms(q_ref[0, hh, pl.ds(r0, nq_rows), :]) * kmax2[hh]) * REF_MARGIN
        return carry

    lax.fori_loop(0, sq // nq_rows, ref_body, 0)

    def scores(qs, p0, j, width, masked):
        k0 = pl.multiple_of(j * width, width)
        kbs = [k_ref[0, hh, pl.ds(k0, width), :] for hh in range(n_heads)]
        ss = [lax.dot_general(kbs[hh], qs[i], nt, preferred_element_type=F32)
              for i, (hh, c) in enumerate(streams)]
        if masked:
            k_pos = k0 + lax.broadcasted_iota(jnp.int32, (width, 1), 0)
            for c in range(tq // cw):
                q_pos = p0 + c * cw + lax.broadcasted_iota(jnp.int32, (1, cw), 1)
                visible = k_pos < jnp.minimum((q_pos // CHUNK + 1) * CHUNK, sk_valid)
                for i, (hh, c2) in enumerate(streams):
                    if c2 == c:
                        ss[i] = jnp.where(visible, ss[i], NEG_INF)
        return ss

    def values(j, width):
        k0 = pl.multiple_of(j * width, width)
        return [vt_ref[0, hh, :, pl.ds(k0, width)] for hh in range(n_heads)]

    def step_running_max(qs, p0, j, carry, masked):
        ss, vts = scores(qs, p0, j, tk, masked), values(j, tk)
        new = []
        for i, (hh, c) in enumerate(streams):
            m, acc = carry[i]
            m_new = jnp.maximum(m, jnp.max(ss[i], axis=0, keepdims=True))
            acc = jnp.exp2(m - m_new) * acc + _dot(vts[hh], jnp.exp2(ss[i] - m_new).astype(BF16))
            new.append((m_new, acc))
        return tuple(new)

    def step_fixed_ref(qs, refs, p0, j, width, carry, masked):
        ss, vts = scores(qs, p0, j, width, masked), values(j, width)
        return tuple(carry[i] + _dot(vts[hh], jnp.exp2(ss[i] - refs[i]).astype(BF16))
                     for i, (hh, c) in enumerate(streams))

    def q_body(qi, carry):
        q0 = pl.multiple_of(qi * tq, tq)
        qs = [q_ref[0, hh, pl.ds(q0 + c * cw, cw), :] for hh, c in streams]
        p0 = q_pos0 + q0
        n_open = jnp.minimum((p0 // CHUNK + 1) * CHUNK, sk_valid) // tk
        v1 = jnp.minimum(((p0 + tq - 1) // CHUNK + 1) * CHUNK, sk_valid)
        n_all = (v1 + tk - 1) // tk
        refs = [ref_scr[hh, :, pl.ds(q0 + c * cw, cw)] for hh, c in streams]
        ref_max = functools.reduce(jnp.maximum, [jnp.max(r) for r in refs])

        def fixed_ref():
            st = tuple(jnp.zeros((VT_ROWS, cw), F32) for _ in streams)
            done = 0
            for span in ATTN_SPANS:
                if span * tk > sk:
                    continue
                first = done // span
                last = n_open // span
                st = lax.fori_loop(first, last,
                                   lambda j, c, span=span: step_fixed_ref(qs, refs, p0, j, span * tk, c, False), st)
                done = last * span
            return lax.fori_loop(n_open, n_all, lambda j, c: step_fixed_ref(qs, refs, p0, j, tk, c, True), st)

        def running_max():
            st = tuple((jnp.full((1, cw), NEG_INF, F32), jnp.zeros((VT_ROWS, cw), F32)) for _ in streams)
            st = lax.fori_loop(0, n_open, lambda j, c: step_running_max(qs, p0, j, c, False), st)
            st = lax.fori_loop(n_open, n_all, lambda j, c: step_running_max(qs, p0, j, c, True), st)
            return tuple(acc for _, acc in st)

        accs = lax.cond(ref_max < REF_LIMIT, fixed_ref, running_max)
        for c in range(tq // cw):
            ot = jnp.concatenate([accs[i][:V_DIM] / accs[i][V_DIM:V_DIM + 1]
                                  for i, (hh, c2) in enumerate(streams) if c2 == c], axis=0)
            o_ref[0, pl.ds(q0 + c * cw, cw), :] = ot.T.astype(o_ref.dtype)
        return carry

    lax.fori_loop(0, nq, q_body, 0)


def _attn_call(q, k, vt, *, tq, tk, q_pos0, sk_valid):
    bsz, nh, sq, _ = q.shape
    sk = k.shape[2]
    assert sq % tq == 0 and sk % tk == 0 and tq % LANES == 0 and tk % LANES == 0
    kern = functools.partial(_attn_kernel, tq=tq, tk=tk, q_pos0=q_pos0, sk_valid=sk_valid)
    return pl.pallas_call(
        kern,
        grid=(bsz, nh // 2),
        in_specs=[
            pl.BlockSpec((1, 2, sq, HEAD_PAD), lambda b, h: (b, h, 0, 0)),
            pl.BlockSpec((1, 2, sk, HEAD_PAD), lambda b, h: (b, h, 0, 0)),
            pl.BlockSpec((1, 2, VT_ROWS, sk), lambda b, h: (b, h, 0, 0)),
        ],
        out_specs=pl.BlockSpec((1, sq, 2 * V_DIM), lambda b, h: (b, 0, h)),
        out_shape=jax.ShapeDtypeStruct((bsz, sq, nh * V_DIM), BF16),
        scratch_shapes=[pltpu.VMEM((2, 1, sq), F32)],
        compiler_params=pltpu.CompilerParams(
            dimension_semantics=("parallel", "parallel"), vmem_limit_bytes=VMEM_LIMIT),
    )(q, k, vt)


def _s5_kernel(u_ref, h0_ref, lre_ref, lim_ref, bbd_ref, cbd_ref, d_ref, wglu_ref, bglu_ref,
               y_ref, hout_ref, hbuf, hstate, u_tb, y_tb, *, ts, n_half, col_chunk):
    i = pl.program_id(0)
    half = n_half
    width = hbuf.shape[1]
    bsz, _, s5_width = u_ref.shape
    uw = s5_width // 2
    n_slab = s5_width // LANES

    @pl.when(i == 0)
    def _():
        hstate[...] = h0_ref[...]

    for b in range(bsz):
        ub = u_ref[b].astype(F32)
        for c in range(n_slab):
            u_tb[c, pl.ds(b, ts, stride=SUBLANES), :] = ub[:, c * LANES:(c + 1) * LANES]
    u32 = jnp.concatenate([u_tb[c] for c in range(n_slab)], axis=-1)
    u = u32.astype(BF16)
    for c in range(2):
        hbuf[:, c * 2 * half:(c + 1) * 2 * half] = _dot(u[:, c * uw:(c + 1) * uw], bbd_ref[c])

    for c in range(2):
        for cc in range(half // col_chunk):
            re0 = c * 2 * half + cc * col_chunk
            im0 = re0 + half
            l0 = c * half + cc * col_chunk
            ar = lre_ref[:, l0:l0 + col_chunk]
            ai = lim_ref[:, l0:l0 + col_chunk]

            def step(s, carry, re0=re0, im0=im0, ar=ar, ai=ai):
                hr, hi = carry
                r0 = pl.multiple_of(s * SUBLANES, SUBLANES)
                nr = ar * hr - ai * hi + hbuf[pl.ds(r0, SUBLANES), re0:re0 + col_chunk]
                ni = ar * hi + ai * hr + hbuf[pl.ds(r0, SUBLANES), im0:im0 + col_chunk]
                hbuf[pl.ds(r0, SUBLANES), re0:re0 + col_chunk] = nr
                hbuf[pl.ds(r0, SUBLANES), im0:im0 + col_chunk] = ni
                return nr, ni

            hr, hi = lax.fori_loop(
                0, ts, step,
                (hstate[:, re0:re0 + col_chunk], hstate[:, im0:im0 + col_chunk]), unroll=8)
            hstate[:, re0:re0 + col_chunk] = hr
            hstate[:, im0:im0 + col_chunk] = hi

    hout_ref[...] = hstate[...]

    y = jnp.concatenate(
        [_dot(hbuf[:, c * 2 * half:(c + 1) * 2 * half].astype(BF16), cbd_ref[c]) for c in range(2)], axis=-1)
    y = y + d_ref[...] * u32
    z = jax.nn.gelu(y)
    gate = _dot(z.astype(BF16), wglu_ref[...]) + bglu_ref[...]
    out = z * jax.nn.sigmoid(gate)
    for c in range(n_slab):
        y_tb[c] = out[:, c * LANES:(c + 1) * LANES]
    for b in range(bsz):
        y_ref[b] = jnp.concatenate([y_tb[c, pl.ds(b, ts, stride=SUBLANES), :] for c in range(n_slab)],
                                   axis=-1).astype(y_ref.dtype)


def _s5_call(u, h0, w, *, ts):
    bsz, seq, s5_width = u.shape
    assert bsz == SUBLANES, "the S5 scan keeps the batch on the sublanes"
    n_state = w['lam_re'].shape[1]
    half = n_state // 2
    width = 2 * n_state
    n_steps = seq // ts
    full = lambda shape: pl.BlockSpec(shape, lambda i: (0,) * len(shape))
    kern = functools.partial(_s5_kernel, ts=ts, n_half=half, col_chunk=512)
    return pl.pallas_call(
        kern,
        grid=(n_steps,),
        in_specs=[
            pl.BlockSpec((bsz, ts, s5_width), lambda i: (0, i, 0)),
            full((SUBLANES, width)), full((SUBLANES, n_state)), full((SUBLANES, n_state)),
            full((2, s5_width // 2, 2 * half)), full((2, 2 * half, s5_width // 2)),
            full((1, s5_width)), full((s5_width, s5_width)), full((1, s5_width)),
        ],
        out_specs=[
            pl.BlockSpec((bsz, ts, s5_width), lambda i: (0, i, 0)),
            full((SUBLANES, width)),
        ],
        out_shape=[
            jax.ShapeDtypeStruct((bsz, seq, s5_width), BF16),
            jax.ShapeDtypeStruct((SUBLANES, width), F32),
        ],
        scratch_shapes=[
            pltpu.VMEM((ts * SUBLANES, width), F32),
            pltpu.VMEM((SUBLANES, width), F32),
            pltpu.VMEM((s5_width // LANES, ts * SUBLANES, LANES), F32),
            pltpu.VMEM((s5_width // LANES, ts * SUBLANES, LANES), F32),
        ],
        compiler_params=pltpu.CompilerParams(
            dimension_semantics=("arbitrary",), vmem_limit_bytes=VMEM_LIMIT),
    )(u, h0, w['lam_re'], w['lam_im'], w['b_bd'], w['c_bd'], w['s5_d'], w['w_glu'], w['b_glu'])


def _out_kernel(x_ref, s5_ref, at_ref, gs5_ref, gat_ref, wout_ref, gffn_ref, wr_ref, br_ref,
                h_ref, route_ref, *, sub):
    for r in range(x_ref.shape[1] // sub):
        rows = pl.ds(r * sub, sub)
        s5n = _rms(s5_ref[0, rows, :].astype(F32), gs5_ref[...])
        atn = _rms(at_ref[0, rows, :].astype(F32), gat_ref[...])
        merged = jnp.concatenate([s5n, atn], axis=-1).astype(BF16)
        h = x_ref[0, rows, :] + _dot(merged, wout_ref[...])
        for j in range(h.shape[1] // LANES):
            h_ref[pl.ds(r * sub * SUBLANES + j, sub, stride=SUBLANES), :] = h[:, j * LANES:(j + 1) * LANES]
        hn = _rms(h, gffn_ref[...]).astype(BF16)

        logits = _dot(hn, wr_ref[...]) + br_ref[...]
        lane_i = _lane_iota(logits.shape)
        lane = lane_i.astype(F32)
        lane_group = (lane_i // EXPERTS_PER_GROUP).astype(F32)
        big = float(LANES)
        is_g = (lane_i >= N_EXPERTS) & (lane_i < N_EXPERTS + N_EXPERT_GROUPS)
        gl = jnp.where(is_g, logits, NEG_INF)
        gmax = jnp.max(gl, axis=-1, keepdims=True)
        g_idx = jnp.min(jnp.where(gl == gmax, lane, big), axis=-1, keepdims=True) - N_EXPERTS
        g_top = 1.0 / jnp.sum(jnp.where(is_g, jnp.exp(gl - gmax), 0.0), axis=-1, keepdims=True)
        in_group = (lane_i < N_EXPERTS) & (lane_group == g_idx)
        el = jnp.where(in_group, logits, NEG_INF)
        v1 = jnp.max(el, axis=-1, keepdims=True)
        i1 = jnp.min(jnp.where(el == v1, lane, big), axis=-1, keepdims=True)
        el2 = jnp.where(lane == i1, NEG_INF, el)
        v2 = jnp.max(el2, axis=-1, keepdims=True)
        i2 = jnp.min(jnp.where(el2 == v2, lane, big), axis=-1, keepdims=True)
        e21 = jnp.exp(v2 - v1)
        w1 = g_top / (1.0 + e21)
        w2 = g_top * e21 / (1.0 + e21)
        route_ref[0, rows, :] = (jnp.where(lane_i == 0, i1, 0.0) + jnp.where(lane_i == 1, i2, 0.0)
                                 + jnp.where(lane_i == 2, w1, 0.0) + jnp.where(lane_i == 3, w2, 0.0))


def _out_call(x, s5_out, attn, w, *, tm):
    bsz, seq, dm = x.shape
    s5_width = s5_out.shape[2]
    aw = attn.shape[2]
    full = lambda shape: pl.BlockSpec(shape, lambda b, i: (0,) * len(shape))
    tok = lambda width: pl.BlockSpec((1, tm, width), lambda b, i: (b, i, 0))
    return pl.pallas_call(
        functools.partial(_out_kernel, sub=min(tm, ROW_SUB)),
        grid=(bsz, seq // tm),
        in_specs=[
            tok(dm),
            tok(s5_width),
            tok(aw),
            full((1, s5_width)), full((1, aw)), full((s5_width + aw, dm)), full((1, dm)),
            full((dm, LANES)), full((1, LANES)),
        ],
        out_specs=[
            pl.BlockSpec((tm * SUBLANES, LANES), lambda b, i: (b * (seq // tm) + i, 0)),
            tok(LANES),
        ],
        out_shape=[
            jax.ShapeDtypeStruct((bsz * seq * SUBLANES, LANES), F32),
            jax.ShapeDtypeStruct((bsz, seq, LANES), F32),
        ],
        compiler_params=pltpu.CompilerParams(
            dimension_semantics=("parallel", "parallel"), vmem_limit_bytes=VMEM_LIMIT),
    )(x, s5_out, attn, w['g_s5'], w['g_attn'], w['w_out'], w['g_ffn'], w['w_router'], w['b_router'])


def _moe_kernel(offs_ref, toks_ref, gws_ref, h_hbm, gffn_ref, wgu_ref, wd_ref, o_ref, acc, hn_ref, gbuf, ybuf, sem,
                *, d_expert, tb, mc, norm_rows):
    b = pl.program_id(0)
    e = pl.program_id(1)
    tok_rows = SUBLANES
    dm = tok_rows * LANES

    @pl.when(e == 0)
    def _():
        cp = pltpu.make_async_copy(h_hbm.at[pl.ds(b * tb * tok_rows, tb * tok_rows)], acc, sem)
        cp.start()
        gbuf[...] = jnp.zeros_like(gbuf)
        cp.wait()

        def norm(i, carry):
            r0 = pl.multiple_of(i * norm_rows * tok_rows, norm_rows * tok_rows)
            x3 = acc[pl.ds(r0, norm_rows * tok_rows), :].reshape(norm_rows, tok_rows, LANES)
            ms = jnp.sum(jnp.sum(x3 * x3, axis=2, keepdims=True), axis=1, keepdims=True) * (1.0 / dm)
            hn3 = x3 * lax.rsqrt(ms + EPS) * gffn_ref[...][None]
            hn_ref[pl.ds(r0, norm_rows * tok_rows), :] = hn3.reshape(norm_rows * tok_rows, LANES)
            return carry

        lax.fori_loop(0, tb // norm_rows, norm, 0)

    n_exp = wgu_ref.shape[0]
    offs = [offs_ref[0, 0, e * n_exp + k] for k in range(n_exp + 1)]
    cnts = [offs[k + 1] - offs[k] for k in range(n_exp)]

    def chunk(c, carry):
        bases = [offs[k] + c * mc for k in range(n_exp)]
        ns = [jnp.clip(cnts[k] - c * mc, 0, mc) for k in range(n_exp)]

        for k in range(n_exp):
            def gather8(g, carry, k=k):
                for u in range(SUBLANES):
                    i = g * SUBLANES + u
                    row = pl.multiple_of(toks_ref[0, 0, bases[k] + i], tok_rows)
                    gbuf[k, pl.ds(pl.multiple_of(i * tok_rows, tok_rows), tok_rows), :] = (
                        hn_ref[pl.ds(row, tok_rows), :])
                return carry

            lax.fori_loop(0, (ns[k] + SUBLANES - 1) // SUBLANES, gather8, 0)

        for k in range(n_exp):
            x = jnp.concatenate([gbuf[k, pl.ds(j, mc, stride=tok_rows), :].astype(BF16) for j in range(tok_rows)],
                                axis=-1)
            gu = _dot(x, wgu_ref[k])
            hdn = jax.nn.silu(gu[:, :d_expert]) * gu[:, d_expert:]
            y = _dot(hdn.astype(BF16), wd_ref[k])
            for j in range(tok_rows):
                ybuf[k, pl.ds(j, mc, stride=tok_rows), :] = y[:, j * LANES:(j + 1) * LANES]

        for k in range(n_exp):
            def add_rows(first, count, k=k):
                idx = [first + u for u in range(count)]
                dst = [pl.multiple_of(toks_ref[0, 0, bases[k] + i], tok_rows) for i in idx]
                val = [acc[pl.ds(d, tok_rows), :] + gws_ref[0, 0, bases[k] + i]
                       * ybuf[k, pl.ds(pl.multiple_of(i * tok_rows, tok_rows), tok_rows), :]
                       for i, d in zip(idx, dst)]
                for d, v in zip(dst, val):
                    acc[pl.ds(d, tok_rows), :] = v

            n_full = ns[k] // SUBLANES

            def scatter8(g, carry, add_rows=add_rows):
                add_rows(g * SUBLANES, SUBLANES)
                return carry

            def scatter1(i, carry, add_rows=add_rows):
                add_rows(i, 1)
                return carry

            lax.fori_loop(0, n_full, scatter8, 0)
            lax.fori_loop(n_full * SUBLANES, ns[k], scatter1, 0)
        return carry

    n_chunks = functools.reduce(jnp.maximum, [(cnt + mc - 1) // mc for cnt in cnts])
    lax.fori_loop(0, n_chunks, chunk, 0)

    @pl.when(e == pl.num_programs(1) - 1)
    def _():
        for j in range(tok_rows):
            o_ref[:, j * LANES:(j + 1) * LANES] = acc[pl.ds(j, tb, stride=tok_rows), :]


def _moe_call(offs, toks, gws, h_tiles, w, *, tb, mc):
    tok_rows = SUBLANES
    dm = tok_rows * LANES
    t = h_tiles.shape[0] // tok_rows
    assert w['w_gu'].shape[1] == dm, "a token row must be exactly one (8,128) f32 tile"
    nb = t // tb
    d_expert = w['w_d'].shape[1]
    kern = functools.partial(_moe_kernel, d_expert=d_expert, tb=tb, mc=mc, norm_rows=min(tb, 256))
    smem = lambda n: pl.BlockSpec((1, 1, n), lambda b, e: (b, 0, 0), memory_space=pltpu.SMEM)
    n_exp = MOE_EXPERTS_PER_STEP
    return pl.pallas_call(
        kern,
        grid=(nb, N_EXPERTS // n_exp),
        in_specs=[
            smem(offs.shape[2]), smem(toks.shape[2]), smem(gws.shape[2]),
            pl.BlockSpec(memory_space=pl.ANY),
            pl.BlockSpec((tok_rows, LANES), lambda b, e: (0, 0)),
            pl.BlockSpec((n_exp, dm, 2 * d_expert), lambda b, e: (e, 0, 0)),
            pl.BlockSpec((n_exp, d_expert, dm), lambda b, e: (e, 0, 0)),
        ],
        out_specs=pl.BlockSpec((tb, dm), lambda b, e: (b, 0)),
        out_shape=jax.ShapeDtypeStruct((t, dm), F32),
        scratch_shapes=[
            pltpu.VMEM((tb * tok_rows, LANES), F32),
            pltpu.VMEM((tb * tok_rows, LANES), F32),
            pltpu.VMEM((n_exp, mc * tok_rows, LANES), F32),
            pltpu.VMEM((n_exp, mc * tok_rows, LANES), F32),
            pltpu.SemaphoreType.DMA(()),
        ],
        compiler_params=pltpu.CompilerParams(
            dimension_semantics=("parallel", "arbitrary"), vmem_limit_bytes=VMEM_LIMIT),
    )(offs, toks, gws, h_tiles, w['g_ffn'].reshape(tok_rows, LANES), w['w_gu'], w['w_d'])


def _route_tables(route, tb):
    t = route.shape[0]
    nb = t // tb
    ids = route[:, :2].astype(jnp.int32).reshape(nb, 2 * tb)
    wts = route[:, 2:4].reshape(nb, 2 * tb)
    order = jnp.argsort(ids, axis=1, stable=True)
    toks = (order // 2 * SUBLANES).astype(jnp.int32)
    toks = jnp.pad(toks, ((0, 0), (0, SUBLANES)))
    gws = jnp.take_along_axis(wts, order, axis=1)
    counts = jnp.sum(ids[:, :, None] == jnp.arange(N_EXPERTS, dtype=jnp.int32), axis=1, dtype=jnp.int32)
    offs = jnp.concatenate([jnp.zeros((nb, 1), jnp.int32), jnp.cumsum(counts, axis=1)], axis=1)
    offs = jnp.pad(offs, ((0, 0), (0, LANES - offs.shape[1])))
    return offs[:, None, :], toks[:, None, :], gws[:, None, :]


def _prepare_weights(p):
    dm, in_cols = p['w_in'].shape
    s5_width = p['s5_d'].shape[0]
    q_lora = p['q_lora_norm_g'].shape[0]
    kv_lora = p['kv_lora_norm_g'].shape[0]
    half_r = QK_ROPE // 2
    w = {'s5_width': s5_width, 'q_lora': q_lora, 'kv_lora': kv_lora}

    def partner(a):
        return jnp.concatenate([-a[..., half_r:], a[..., :half_r]], axis=-1)

    def swap(a):
        return jnp.concatenate([a[..., half_r:], a[..., :half_r]], axis=-1)

    w_kr = p['w_in'][:, in_cols - QK_ROPE:]
    w['w_in'] = jnp.concatenate(
        [p['w_in'], partner(w_kr), jnp.zeros((dm, LANES - 2 * QK_ROPE), F32)], axis=-1).astype(BF16)
    w['g_mix'] = p['norm_mix_g'][None]
    w['g_q'] = p['q_lora_norm_g'][None]
    w['g_kv'] = p['kv_lora_norm_g'][None]
    gkr = p['k_rope_norm_g']
    w['g_kr'] = jnp.concatenate([gkr, swap(gkr), jnp.zeros((LANES - 2 * QK_ROPE,), F32)])[None]

    wq = p['w_uq']
    wq_r = wq[..., QK_NOPE:]
    w['w_uq'] = jnp.concatenate([wq, partner(wq_r)], axis=-1).reshape(q_lora, N_HEADS * HEAD_PAD).astype(BF16)
    gr = p['q_rope_norm_g']
    w['q_gain'] = jnp.tile(jnp.concatenate([p['q_nope_norm_g'], gr, swap(gr)]), N_HEADS)[None]
    idx = jnp.arange(HEAD_PAD)
    row_nope = (idx < QK_NOPE)[:, None]
    row_rope = ((idx >= QK_NOPE) & (idx < QK_DIM))[:, None]
    col_nope = (idx < QK_NOPE)[None, :]
    m_q = jnp.where(row_nope & col_nope, 1.0 / QK_NOPE, 0.0) + jnp.where(row_rope & ~col_nope, 1.0 / QK_ROPE, 0.0)
    m_k = jnp.where(row_nope, 1.0 / QK_NOPE, 0.0) * jnp.ones((1, HEAD_PAD), F32)
    eye2 = jnp.eye(2, dtype=F32)
    w['bd_q'] = jnp.kron(eye2, m_q).astype(BF16)
    w['bd_k'] = jnp.kron(eye2, m_k).astype(BF16)

    pad_h = lambda a: jnp.pad(a, ((0, 0), (0, 0), (0, HEAD_PAD - a.shape[-1])))
    w['w_uk'] = pad_h(p['w_uk']).reshape(kv_lora, N_HEADS * HEAD_PAD).astype(BF16)
    w['w_uv'] = pad_h(p['w_uv']).reshape(kv_lora, N_HEADS * HEAD_PAD).astype(BF16)
    w['k_gain'] = jnp.tile(jnp.pad(p['k_nope_norm_g'], (0, HEAD_PAD - QK_NOPE)), N_HEADS)[None]

    dt = jnp.exp(p['s5_log_dt'])[:, None]
    ar, ai = p['s5_a_re'], p['s5_a_im']
    mag = jnp.exp(dt * ar)
    abar_re = mag * jnp.cos(dt * ai)
    abar_im = mag * jnp.sin(dt * ai)
    den = ar * ar + ai * ai
    num_re = abar_re - 1.0
    coef_re = (num_re * ar + abar_im * ai) / den
    coef_im = (abar_im * ar - num_re * ai) / den
    br, bi = p['s5_b_re'], p['s5_b_im']
    bbar_re = coef_re[..., None] * br - coef_im[..., None] * bi
    bbar_im = coef_re[..., None] * bi + coef_im[..., None] * br
    g = ar.shape[0]
    n_state = g * S5_STATE
    half = n_state // 2
    eye_g = jnp.eye(g, dtype=F32)
    b_re = jnp.einsum('gni,gh->gihn', bbar_re, eye_g).reshape(s5_width, n_state)
    b_im = jnp.einsum('gni,gh->gihn', bbar_im, eye_g).reshape(s5_width, n_state)
    c_re = jnp.einsum('gon,gh->gnho', p['s5_c_re'], eye_g).reshape(n_state, s5_width)
    c_im = jnp.einsum('gon,gh->gnho', p['s5_c_im'], eye_g).reshape(n_state, s5_width)
    uw = s5_width // 2
    w['b_bd'] = jnp.stack([
        jnp.concatenate([b_re[c * uw:(c + 1) * uw, c * half:(c + 1) * half],
                         b_im[c * uw:(c + 1) * uw, c * half:(c + 1) * half]], axis=1) for c in range(2)]).astype(BF16)
    w['c_bd'] = jnp.stack([
        jnp.concatenate([c_re[c * half:(c + 1) * half, c * uw:(c + 1) * uw],
                         -c_im[c * half:(c + 1) * half, c * uw:(c + 1) * uw]], axis=0) for c in range(2)]).astype(BF16)
    w['lam_re'] = jnp.broadcast_to(abar_re.reshape(1, n_state), (SUBLANES, n_state))
    w['lam_im'] = jnp.broadcast_to(abar_im.reshape(1, n_state), (SUBLANES, n_state))
    w['s5_d'] = p['s5_d'][None]
    w['w_glu'] = p['s5_w_glu'].astype(BF16)
    w['b_glu'] = p['s5_b_glu'][None]

    w['g_s5'] = p['out_norm_s5_g'][None]
    w['g_attn'] = p['out_norm_attn_g'][None]
    w['w_out'] = p['w_out'].astype(BF16)
    w['g_ffn'] = p['norm_ffn_g'][None]
    n_r = N_EXPERTS + N_EXPERT_GROUPS
    w['w_router'] = jnp.pad(jnp.concatenate([p['w_router_expert'], p['w_router_group']], axis=1),
                            ((0, 0), (0, LANES - n_r))).astype(BF16)
    w['b_router'] = jnp.pad(jnp.concatenate([p['b_router_expert'], p['b_router_group']]), (0, LANES - n_r))[None]
    w['w_gu'] = jnp.concatenate([p['w_e_gate'], p['w_e_up']], axis=-1).astype(BF16)
    w['w_d'] = p['w_e_down'].astype(BF16)
    return w


def _rope_table(pos):
    half = QK_ROPE // 2
    inv = ROPE_THETA ** (-jnp.arange(half, dtype=F32) / half)
    ang = pos.astype(F32)[:, None] * inv[None, :]
    cc = jnp.tile(jnp.cos(ang), (1, 2))
    ss = jnp.tile(jnp.sin(ang), (1, 2))
    n = pos.shape[0]
    scale = QK_DIM ** -0.5 * math.log2(math.e)
    tab_q = scale * jnp.concatenate([jnp.ones((n, QK_NOPE), F32), cc, ss], axis=-1)
    tab_k = jnp.concatenate([cc, ss, jnp.zeros((n, LANES - 2 * QK_ROPE), F32)], axis=-1)
    return jnp.concatenate([tab_q, tab_k], axis=-1)


def _s5_state_in(re, im):
    b = re.shape[0]
    re = re.reshape(b, -1)
    im = im.reshape(b, -1)
    half = re.shape[1] // 2
    return jnp.concatenate([re[:, :half], im[:, :half], re[:, half:], im[:, half:]], axis=1)


def _s5_state_out(h, g):
    b = h.shape[0]
    half = h.shape[1] // 4
    re = jnp.concatenate([h[:, :half], h[:, 2 * half:3 * half]], axis=1).reshape(b, g, S5_STATE)
    im = jnp.concatenate([h[:, half:2 * half], h[:, 3 * half:]], axis=1).reshape(b, g, S5_STATE)
    return re, im


def _trunk_layer(x, past_ckv, past_krope, h0_re, h0_im, w):
    bsz, seq, dm = x.shape
    assert bsz == SUBLANES, "the S5 scan keeps the batch on the sublanes"
    past_len = 0 if past_ckv is None else past_ckv.shape[1]
    g = w['lam_re'].shape[1] // S5_STATE
    tm = _tile(seq, ROW_TILE)

    tab = _rope_table(past_len + jnp.arange(seq, dtype=jnp.int32))
    u, q, new_ckv, new_krope = _proj_call(x, tab, w, tm=tm)

    if h0_re is None:
        h0 = jnp.zeros((bsz, 2 * g * S5_STATE), F32)
    else:
        h0 = _s5_state_in(h0_re.astype(F32), h0_im.astype(F32))
    s5_out, h_last = _s5_call(u, h0, w, ts=min(seq, 64))
    h_re, h_im = _s5_state_out(h_last, g)

    if past_ckv is None:
        all_ckv, all_krope = new_ckv, new_krope
    else:
        all_ckv = jnp.concatenate([past_ckv.astype(F32), new_ckv], axis=1)
        all_krope = jnp.concatenate([past_krope.astype(F32), new_krope], axis=1)
    sk = all_ckv.shape[1]
    sk_pad = -(-sk // LANES) * LANES
    all_ckv = jnp.pad(all_ckv, ((0, 0), (0, sk_pad - sk), (0, 0)))
    krp = jnp.pad(all_krope, ((0, 0), (0, sk_pad - sk), (QK_NOPE, LANES - QK_DIM)))
    whole = sk_pad % ATTN_TK != 0
    k, vt = _kv_call(all_ckv, krp, w, tm=sk_pad if whole else KV_TILE)
    sq_pad = -(-seq // LANES) * LANES
    q = jnp.pad(q, ((0, 0), (0, 0), (0, sq_pad - seq), (0, 0)))
    tq = ATTN_TQ if sq_pad % ATTN_TQ == 0 else LANES
    attn = _attn_call(q, k, vt, tq=tq, tk=sk_pad if whole else ATTN_TK, q_pos0=past_len, sk_valid=sk)[:, :seq]

    h_tiles, route = _out_call(x, s5_out, attn, w, tm=tm)
    t = bsz * seq
    tb = _tile(t, MOE_BLOCK)
    offs, toks, gws = _route_tables(route.reshape(t, LANES), tb)
    y = _moe_call(offs, toks, gws, h_tiles, w, tb=tb, mc=MOE_CHUNK)
    return y.reshape(bsz, seq, dm), new_ckv, new_krope, h_re, h_im


def kernel(x_prompt, x_sample, cache_ckv, cache_krope, state_s5_re, state_s5_im, norm_mix_g, w_in, s5_a_re, s5_a_im, s5_log_dt, s5_b_re, s5_b_im, s5_c_re, s5_c_im, s5_d, s5_w_glu, s5_b_glu, q_lora_norm_g, w_uq, kv_lora_norm_g, w_uk, w_uv, q_nope_norm_g, q_rope_norm_g, k_nope_norm_g, k_rope_norm_g, out_norm_s5_g, out_norm_attn_g, w_out, norm_ffn_g, w_router_group, b_router_group, w_router_expert, b_router_expert, w_e_gate, w_e_up, w_e_down):
    params = dict(
        norm_mix_g=norm_mix_g, w_in=w_in, s5_a_re=s5_a_re, s5_a_im=s5_a_im, s5_log_dt=s5_log_dt,
        s5_b_re=s5_b_re, s5_b_im=s5_b_im, s5_c_re=s5_c_re, s5_c_im=s5_c_im, s5_d=s5_d,
        s5_w_glu=s5_w_glu, s5_b_glu=s5_b_glu, q_lora_norm_g=q_lora_norm_g, w_uq=w_uq,
        kv_lora_norm_g=kv_lora_norm_g, w_uk=w_uk, w_uv=w_uv, q_nope_norm_g=q_nope_norm_g,
        q_rope_norm_g=q_rope_norm_g, k_nope_norm_g=k_nope_norm_g, k_rope_norm_g=k_rope_norm_g,
        out_norm_s5_g=out_norm_s5_g, out_norm_attn_g=out_norm_attn_g, w_out=w_out, norm_ffn_g=norm_ffn_g,
        w_router_group=w_router_group, b_router_group=b_router_group,
        w_router_expert=w_router_expert, b_router_expert=b_router_expert,
        w_e_gate=w_e_gate, w_e_up=w_e_up, w_e_down=w_e_down)
    depth = w_in.shape[0]
    y_p, y_s = x_prompt, x_sample
    outs = [[] for _ in range(8)]
    for l in range(depth):
        w = _prepare_weights({k: a[l] for k, a in params.items()})
        y_p, c1, k1, r1, i1 = _trunk_layer(y_p, None, None, None, None, w)
        y_s, c2, k2, r2, i2 = _trunk_layer(y_s, cache_ckv[l], cache_krope[l], state_s5_re[l], state_s5_im[l], w)
        for lst, a in zip(outs, (c1, k1, r1, i1, c2, k2, r2, i2)):
            lst.append(a)
    return (y_p, y_s) + tuple(jnp.stack(lst) for lst in outs)
```
